```python
import jax, jax.numpy as jnp
from jax import lax
import numpy as np

D_MODEL = 2048
BATCH = 2
SEQ = 8192
DEPTH = 2

N_A = DEPTH // 2
N_B = DEPTH - N_A
N_HEADS = 16
HEAD_DIM = D_MODEL // N_HEADS
D_FF = ((8 * D_MODEL // 3 + 255) // 256) * 256
CONV_WIDTH = 3
Q_BLOCK = 128
N_SUB = 3
N_MOD = 3
EPS = 1e-6

kernel_name = "yoco_shortconv_fox_macaron_adaln"


def rmsnorm(x, g):
    x32 = x.astype(jnp.float32)
    y = x32 * lax.rsqrt(jnp.mean(x32 * x32, axis=-1, keepdims=True) + EPS)
    return y.astype(x.dtype) * g


def modulate(h, shift, scale):
    return h * (1.0 + scale[:, None, :]) + shift[:, None, :]


def swiglu(h, w_in, w_out):
    a, b = jnp.split(h @ w_in, 2, axis=-1)
    return (jax.nn.silu(a) * b) @ w_out


def short_gated_conv(h, w_in, conv_w, conv_b, w_out):
    bg, cg, xv = jnp.split(h @ w_in, 3, axis=-1)
    u = cg * xv
    u = lax.conv_general_dilated(
        u, conv_w[:, None, :], window_strides=(1,),
        padding=[(CONV_WIDTH - 1, 0)],
        dimension_numbers=("NWC", "WIO", "NWC"),
        feature_group_count=D_MODEL) + conv_b
    return (bg * u) @ w_out


def forgetting_attention(q, k, v, fcum):
    b, s_len, h, dh = q.shape
    nb = s_len // Q_BLOCK
    scale = 1.0 / float(np.sqrt(dh))
    qb = q.reshape(b, nb, Q_BLOCK, h, dh).transpose(1, 0, 2, 3, 4)
    fb = fcum.reshape(b, h, nb, Q_BLOCK).transpose(2, 0, 1, 3)
    kpos = jnp.arange(s_len)

    def one_block(args):
        qi, fi, i = args
        logits = jnp.einsum("bqhd,bkhd->bhqk", qi, k,
                            preferred_element_type=jnp.float32) * scale
        logits = logits + fi[..., :, None] - fcum[:, :, None, :]
        qpos = i * Q_BLOCK + jnp.arange(Q_BLOCK)
        mask = qpos[:, None] >= kpos[None, :]
        logits = jnp.where(mask[None, None], logits, -jnp.inf)
        p = jax.nn.softmax(logits, axis=-1)
        return jnp.einsum("bhqk,bkhd->bqhd", p.astype(v.dtype), v)

    out = lax.map(one_block, (qb, fb, jnp.arange(nb)))
    return out.transpose(1, 0, 2, 3, 4).reshape(b, s_len, h * dh)


def setup_inputs(seed: int = 0) -> dict:
    key = jax.random.key(seed)
    ks = jax.random.split(key, 24)
    D, F, H = D_MODEL, D_FF, N_HEADS
    nrm = lambda k, shape, fan_in: jax.random.normal(k, shape, jnp.float32) * (fan_in ** -0.5)
    return {
        "x": jax.random.normal(ks[0], (BATCH, SEQ, D), jnp.float32),
        "c": jax.random.normal(ks[1], (BATCH, D), jnp.float32),
        "norm_g": 1.0 + 0.05 * jax.random.normal(ks[2], (DEPTH, N_SUB, D), jnp.float32),
        "w_ada": nrm(ks[3], (DEPTH, D, N_SUB * N_MOD * D), D),
        "b_ada": 0.02 * jax.random.normal(ks[4], (DEPTH, N_SUB * N_MOD * D), jnp.float32),
        "w_ffn_in": nrm(ks[5], (DEPTH, 2, D, 2 * F), D),
        "w_ffn_out": nrm(ks[6], (DEPTH, 2, F, D), F),
        "w_conv_in": nrm(ks[7], (N_A, D, 3 * D), D),
        "conv_w": nrm(ks[8], (N_A, CONV_WIDTH, D), CONV_WIDTH),
        "conv_b": 0.02 * jax.random.normal(ks[9], (N_A, D), jnp.float32),
        "w_conv_out": nrm(ks[10], (N_A, D, D), D),
        "kv_norm_g": 1.0 + 0.05 * jax.random.normal(ks[11], (D,), jnp.float32),
        "w_ada_kv": nrm(ks[12], (D, 2 * D), D),
        "b_ada_kv": 0.02 * jax.random.normal(ks[13], (2 * D,), jnp.float32),
        "w_kvf": jnp.concatenate([nrm(ks[14], (D, 2 * D), D),
                                  0.1 * nrm(ks[15], (D, H), D)], axis=-1),
        "b_fgate": jax.random.uniform(ks[16], (H,), jnp.float32, 1.0, 6.0),
        "w_q": nrm(ks[17], (N_B, D, D), D),
        "w_o": nrm(ks[18], (N_B, D, D), D),
        "final_g": 1.0 + 0.05 * jax.random.normal(ks[19], (D,), jnp.float32),
    }


def reference(x, c, norm_g, w_ada, b_ada, w_ffn_in, w_ffn_out, w_conv_in, conv_w, conv_b,
              w_conv_out, kv_norm_g, w_ada_kv, b_ada_kv, w_kvf, b_fgate, w_q, w_o, final_g):
    b, s_len, d = x.shape
    cond = jax.nn.silu(c)
    k = v = fcum = None
    for l in range(DEPTH):
        ada = (cond @ w_ada[l] + b_ada[l]).reshape(b, N_SUB, N_MOD, d)

        h = modulate(rmsnorm(x, norm_g[l, 0]), ada[:, 0, 0], ada[:, 0, 1])
        x = x + 0.5 * ada[:, 0, 2][:, None, :] * swiglu(h, w_ffn_in[l, 0], w_ffn_out[l, 0])

        h = modulate(rmsnorm(x, norm_g[l, 1]), ada[:, 1, 0], ada[:, 1, 1])
        if l < N_A:
            y = short_gated_conv(h, w_conv_in[l], conv_w[l], conv_b[l], w_conv_out[l])
        else:
            if l == N_A:
                ada_kv = (cond @ w_ada_kv + b_ada_kv).reshape(b, 2, d)
                hkv = modulate(rmsnorm(x, kv_norm_g), ada_kv[:, 0], ada_kv[:, 1])
                kvf = hkv @ w_kvf
                k = kvf[..., :d].reshape(b, s_len, N_HEADS, HEAD_DIM)
                v = kvf[..., d:2 * d].reshape(b, s_len, N_HEADS, HEAD_DIM)
                zf = (kvf[..., 2 * d:] + b_fgate).astype(jnp.float32)
                fcum = jnp.cumsum(jax.nn.log_sigmoid(zf), axis=1).transpose(0, 2, 1)
            j = l - N_A
            q = (h @ w_q[j]).reshape(b, s_len, N_HEADS, HEAD_DIM)
            y = forgetting_attention(q, k, v, fcum) @ w_o[j]
        x = x + ada[:, 1, 2][:, None, :] * y

        h = modulate(rmsnorm(x, norm_g[l, 2]), ada[:, 2, 0], ada[:, 2, 1])
        x = x + 0.5 * ada[:, 2, 2][:, None, :] * swiglu(h, w_ffn_in[l, 1], w_ffn_out[l, 1])
    return rmsnorm(x, final_g)
```

```python
import functools
import math

import jax
import jax.numpy as jnp
from jax import lax
from jax.experimental import pallas as pl
from jax.experimental.pallas import tpu as pltpu

EPS = 1e-6
N_HEADS = 16
HEAD_DIM = 128
CONV_WIDTH = 3
LOG2E = math.log2(math.e)
LANES = 128
SUBLANES = 8
VMEM_LIMIT_BYTES = 56 * 1024 * 1024

F32 = jnp.float32
BF16 = jnp.bfloat16


def _params(semantics):
    return pltpu.CompilerParams(dimension_semantics=semantics,
                                vmem_limit_bytes=VMEM_LIMIT_BYTES)


def _dot(a, b):
    return jnp.dot(a, b, preferred_element_type=F32)


def _rmsnorm(x, g):
    return x * lax.rsqrt(jnp.mean(x * x, axis=-1, keepdims=True) + EPS) * g


def _modulated_norm(x, g, ada_ref, sub):
    shift = ada_ref[0, 3 * sub:3 * sub + 1, :]
    scale = ada_ref[0, 3 * sub + 1:3 * sub + 2, :]
    return _rmsnorm(x, g) * (1.0 + scale) + shift


def _ada_kernel(c_ref, w_ref, b_ref, o_ref):
    c = c_ref[...]
    cond = c * jax.nn.sigmoid(c)
    o_ref[...] = _dot(cond.astype(BF16), w_ref[...].astype(BF16)) + b_ref[...]


def _ada(c_pad, w, b, tn=1024):
    d, n = w.shape
    return pl.pallas_call(
        _ada_kernel,
        grid=(n // tn,),
        in_specs=[pl.BlockSpec((SUBLANES, d), lambda j: (0, 0)),
                  pl.BlockSpec((d, tn), lambda j: (0, j)),
                  pl.BlockSpec((1, tn), lambda j: (0, j))],
        out_specs=pl.BlockSpec((SUBLANES, tn), lambda j: (0, j)),
        out_shape=jax.ShapeDtypeStruct((SUBLANES, n), F32),
        compiler_params=_params(("parallel",)),
        name="ada",
    )(c_pad, w, b.reshape(1, n))


def _ffn_kernel(x_ref, ada_ref, g_ref, wg_ref, wu_ref, wo_ref, *rest, sub, final):
    if final:
        fg_ref, o_ref, h_ref = rest
    else:
        o_ref, h_ref = rest
    f = pl.program_id(1)

    @pl.when(f == 0)
    def _():
        h_ref[...] = _modulated_norm(x_ref[...], g_ref[...], ada_ref, sub).astype(BF16)
        o_ref[...] = jnp.zeros_like(o_ref)

    h = h_ref[...]
    a = _dot(h, wg_ref[...])
    b = _dot(h, wu_ref[...])
    act = (a * jax.nn.sigmoid(a) * b).astype(BF16)
    o_ref[...] += _dot(act, wo_ref[...])

    @pl.when(f == pl.num_programs(1) - 1)
    def _():
        gate = ada_ref[0, 3 * sub + 2:3 * sub + 3, :]
        y = x_ref[...] + (0.5 * gate) * o_ref[...]
        if final:
            y = _rmsnorm(y, fg_ref[...])
        o_ref[...] = y


def _ffn(x, ada, g, w_in, w_out, sub, seq, final_g=None, tm=512, tf=512):
    t, d = x.shape
    ff = w_out.shape[0]
    nf = ff // tf
    per_batch = seq // tm
    final = final_g is not None
    in_specs = [pl.BlockSpec((tm, d), lambda i, f: (i, 0)),
                pl.BlockSpec((1, 9, d), lambda i, f: (i // per_batch, 0, 0)),
                pl.BlockSpec((1, d), lambda i, f: (0, 0)),
                pl.BlockSpec((d, tf), lambda i, f: (0, f)),
                pl.BlockSpec((d, tf), lambda i, f: (0, nf + f)),
                pl.BlockSpec((tf, d), lambda i, f: (f, 0))]
    args = [x, ada, g.reshape(1, d), w_in, w_in, w_out]
    if final:
        in_specs.append(pl.BlockSpec((1, d), lambda i, f: (0, 0)))
        args.append(final_g.reshape(1, d))
    return pl.pallas_call(
        functools.partial(_ffn_kernel, sub=sub, final=final),
        grid=(t // tm, nf),
        in_specs=in_specs,
        out_specs=pl.BlockSpec((tm, d), lambda i, f: (i, 0)),
        out_shape=jax.ShapeDtypeStruct((t, d), F32),
        scratch_shapes=[pltpu.VMEM((tm, d), BF16)],
        compiler_params=_params(("parallel", "arbitrary")),
        name="ffn_final" if final else "ffn",
    )(*args)


def _conv_kernel(x_ref, ada_ref, g_ref, wb_ref, wc_ref, wx_ref, cw_ref, cb_ref, wo_ref,
                 o_ref, h_ref, tail_ref, *, per_batch):
    i = pl.program_id(0)
    c = pl.program_id(1)
    tm = x_ref.shape[0]

    @pl.when(c == 0)
    def _():
        h_ref[...] = _modulated_norm(x_ref[...], g_ref[...], ada_ref, 1).astype(BF16)
        o_ref[...] = jnp.zeros_like(o_ref)

    @pl.when(i % per_batch == 0)
    def _():
        tail_ref[c] = jnp.zeros(tail_ref.shape[1:], F32)

    h = h_ref[...]
    bg = _dot(h, wb_ref[...])
    cg = _dot(h, wc_ref[...])
    xv = _dot(h, wx_ref[...])
    u = cg * xv
    tail = tail_ref[c]
    p1 = tail[SUBLANES - 1:SUBLANES, :]
    p2 = tail[SUBLANES - 2:SUBLANES - 1, :]
    row = lax.broadcasted_iota(jnp.int32, u.shape, 0)
    u1 = jnp.where(row == 0, p1, pltpu.roll(u, 1, 0))
    u2 = jnp.where(row == 0, p2, jnp.where(row == 1, p1, pltpu.roll(u, 2, 0)))
    conv = cw_ref[0:1, :] * u2 + cw_ref[1:2, :] * u1 + cw_ref[2:3, :] * u + cb_ref[...]
    tail_ref[c] = u[tm - SUBLANES:, :]
    o_ref[...] += _dot((bg * conv).astype(BF16), wo_ref[...])

    @pl.when(c == pl.num_programs(1) - 1)
    def _():
        gate = ada_ref[0, 5:6, :]
        o_ref[...] = x_ref[...] + gate * o_ref[...]


def _conv(x, ada, g, w_in, conv_w, conv_b, w_out, seq, tm=512, tc=512):
    t, d = x.shape
    nc = d // tc
    per_batch = seq // tm
    return pl.pallas_call(
        functools.partial(_conv_kernel, per_batch=per_batch),
        grid=(t // tm, nc),
        in_specs=[pl.BlockSpec((tm, d), lambda i, c: (i, 0)),
                  pl.BlockSpec((1, 9, d), lambda i, c: (i // per_batch, 0, 0)),
                  pl.BlockSpec((1, d), lambda i, c: (0, 0)),
                  pl.BlockSpec((d, tc), lambda i, c: (0, c)),
                  pl.BlockSpec((d, tc), lambda i, c: (0, nc + c)),
                  pl.BlockSpec((d, tc), lambda i, c: (0, 2 * nc + c)),
                  pl.BlockSpec((CONV_WIDTH, tc), lambda i, c: (0, c)),
                  pl.BlockSpec((1, tc), lambda i, c: (0, c)),
                  pl.BlockSpec((tc, d), lambda i, c: (c, 0))],
        out_specs=pl.BlockSpec((tm, d), lambda i, c: (i, 0)),
        out_shape=jax.ShapeDtypeStruct((t, d), F32),
        scratch_shapes=[pltpu.VMEM((tm, d), BF16),
                        pltpu.VMEM((nc, SUBLANES, tc), F32)],
        compiler_params=_params(("arbitrary", "arbitrary")),
        name="conv",
    )(x, ada, g.reshape(1, d), w_in, w_in, w_in, conv_w, conv_b.reshape(1, d), w_out)


def _qkv_kernel(x_ref, ada_ref, adakv_ref, g_ref, gkv_ref, wq_ref, wk_ref, wv_ref, wf_ref, bf_ref,
                q_ref, k_ref, v_ref, fc_ref, ft_ref, hq_ref, hkv_ref, carry_ref, *, per_batch, q_scale):
    i = pl.program_id(0)
    c = pl.program_id(1)
    tm = x_ref.shape[0]

    @pl.when(c == 0)
    def _():
        x = x_ref[...]
        hq_ref[...] = _modulated_norm(x, g_ref[...], ada_ref, 1).astype(BF16)
        hkv = (_rmsnorm(x, gkv_ref[...]) * (1.0 + adakv_ref[0, 1:2, :]) + adakv_ref[0, 0:1, :]).astype(BF16)
        hkv_ref[...] = hkv

        @pl.when(i % per_batch == 0)
        def _():
            carry_ref[...] = jnp.zeros_like(carry_ref)

        zf = _dot(hkv, wf_ref[...]) + bf_ref[...]
        ls = jnp.minimum(zf, 0.0) - jnp.log1p(jnp.exp(-jnp.abs(zf)))
        hi = ls.astype(BF16)
        r1 = ls - hi.astype(F32)
        mid = r1.astype(BF16)
        lo = (r1 - mid.astype(F32)).astype(BF16)
        rr = lax.broadcasted_iota(jnp.int32, (tm, tm), 0)
        cc = lax.broadcasted_iota(jnp.int32, (tm, tm), 1)
        tri = (rr >= cc).astype(BF16)
        cum = _dot(tri, hi) + _dot(tri, mid) + _dot(tri, lo) + carry_ref[0:1, :]
        carry_ref[...] = jnp.broadcast_to(cum[tm - 1:tm, :], carry_ref.shape)
        fc_ref[...] = cum
        ft_ref[0] = cum.T[:N_HEADS, :]

    hkv = hkv_ref[...]
    q_ref[...] = (_dot(hq_ref[...], wq_ref[...]) * q_scale).astype(BF16)
    k_ref[...] = _dot(hkv, wk_ref[...]).astype(BF16)
    v_ref[...] = _dot(hkv, wv_ref[...]).astype(BF16)


def _qkv(x, ada, ada_kv, g, gkv, w_q, w_kv, w_f, b_f, seq, tm=512, tn=512):
    t, d = x.shape
    nn = d // tn
    per_batch = seq // tm
    batch = t // seq
    q_scale = LOG2E / math.sqrt(HEAD_DIM)
    row_blk = pl.BlockSpec((tm, tn), lambda i, c: (i, c))
    return pl.pallas_call(
        functools.partial(_qkv_kernel, per_batch=per_batch, q_scale=q_scale),
        grid=(t // tm, nn),
        in_specs=[pl.BlockSpec((tm, d), lambda i, c: (i, 0)),
                  pl.BlockSpec((1, 9, d), lambda i, c: (i // per_batch, 0, 0)),
                  pl.BlockSpec((1, 2, d), lambda i, c: (i // per_batch, 0, 0)),
                  pl.BlockSpec((1, d), lambda i, c: (0, 0)),
                  pl.BlockSpec((1, d), lambda i, c: (0, 0)),
                  pl.BlockSpec((d, tn), lambda i, c: (0, c)),
                  pl.BlockSpec((d, tn), lambda i, c: (0, c)),
                  pl.BlockSpec((d, tn), lambda i, c: (0, nn + c)),
                  pl.BlockSpec((d, LANES), lambda i, c: (0, 0)),
                  pl.BlockSpec((1, LANES), lambda i, c: (0, 0))],
        out_specs=[row_blk, row_blk, row_blk,
                   pl.BlockSpec((tm, LANES), lambda i, c: (i, 0)),
                   pl.BlockSpec((1, N_HEADS, tm), lambda i, c: (i // per_batch, 0, i % per_batch))],
        out_shape=[jax.ShapeDtypeStruct((t, d), BF16),
                   jax.ShapeDtypeStruct((t, d), BF16),
                   jax.ShapeDtypeStruct((t, d), BF16),
                   jax.ShapeDtypeStruct((t, LANES), F32),
                   jax.ShapeDtypeStruct((batch, N_HEADS, seq), F32)],
        scratch_shapes=[pltpu.VMEM((tm, d), BF16),
                        pltpu.VMEM((tm, d), BF16),
                        pltpu.VMEM((SUBLANES, LANES), F32)],
        compiler_params=_params(("arbitrary", "arbitrary")),
        name="qkv",
    )(x, ada, ada_kv, g.reshape(1, d), gkv.reshape(1, d), w_q, w_kv, w_kv, w_f, b_f)


def _attn_kernel(q_ref, k_ref, v_ref, fc_ref, ft_ref, o_ref, *, tq):
    hh = pl.program_id(1)
    i = pl.program_id(2)
    q = q_ref[0]
    lane = lax.broadcasted_iota(jnp.int32, (tq, LANES), 1)
    fq = jnp.sum(jnp.where(lane == hh, fc_ref[0], 0.0), axis=1, keepdims=True) * LOG2E

    def chunk(j, carry, masked):
        m, l, acc = carry
        start = pl.multiple_of(j * tq, tq)
        kj = k_ref[0, pl.ds(start, tq), :]
        vj = v_ref[0, pl.ds(start, tq), :]
        fk = ft_ref[0, pl.ds(hh, 1), pl.ds(start, tq)] * LOG2E
        s = lax.dot_general(q, kj, (((1,), (1,)), ((), ())), preferred_element_type=F32)
        s = s + (fq - fk)
        if masked:
            rr = lax.broadcasted_iota(jnp.int32, (tq, tq), 0)
            cc = lax.broadcasted_iota(jnp.int32, (tq, tq), 1)
            s = jnp.where(rr >= cc, s, -jnp.inf)
        m_new = jnp.maximum(m, jnp.max(s, axis=1, keepdims=True))
        alpha = jnp.exp2(m - m_new)
        p = jnp.exp2(s - m_new)
        l = alpha * l + jnp.sum(p, axis=1, keepdims=True)
        acc = alpha * acc + _dot(p.astype(BF16), vj)
        return m_new, l, acc

    init = (jnp.full((tq, 1), -jnp.inf, F32), jnp.zeros((tq, 1), F32), jnp.zeros((tq, HEAD_DIM), F32))
    carry = lax.fori_loop(0, i, functools.partial(chunk, masked=False), init)
    _, l, acc = chunk(i, carry, True)
    o_ref[0] = (acc / l).astype(BF16)


def _attn(q, k, v, fc, ft, tq=512):
    b, s, d = q.shape
    return pl.pallas_call(
        functools.partial(_attn_kernel, tq=tq),
        grid=(b, N_HEADS, s // tq),
        in_specs=[pl.BlockSpec((1, tq, HEAD_DIM), lambda bb, h, i: (bb, i, h)),
                  pl.BlockSpec((1, s, HEAD_DIM), lambda bb, h, i: (bb, 0, h)),
                  pl.BlockSpec((1, s, HEAD_DIM), lambda bb, h, i: (bb, 0, h)),
                  pl.BlockSpec((1, tq, LANES), lambda bb, h, i: (bb, i, 0)),
                  pl.BlockSpec((1, N_HEADS, s), lambda bb, h, i: (bb, 0, 0))],
        out_specs=pl.BlockSpec((1, tq, HEAD_DIM), lambda bb, h, i: (bb, i, h)),
        out_shape=jax.ShapeDtypeStruct((b, s, d), BF16),
        compiler_params=_params(("parallel", "parallel", "parallel")),
        name="attn",
    )(q, k, v, fc, ft)


def _oproj_kernel(x_ref, a_ref, ada_ref, w_ref, o_ref):
    gate = ada_ref[0, 5:6, :]
    o_ref[...] = x_ref[...] + gate * _dot(a_ref[...], w_ref[...])


def _oproj(x, attn, ada, w_o, seq, tm=512):
    t, d = x.shape
    per_batch = seq // tm
    return pl.pallas_call(
        _oproj_kernel,
        grid=(t // tm,),
        in_specs=[pl.BlockSpec((tm, d), lambda i: (i, 0)),
                  pl.BlockSpec((tm, d), lambda i: (i, 0)),
                  pl.BlockSpec((1, 9, d), lambda i: (i // per_batch, 0, 0)),
                  pl.BlockSpec((d, d), lambda i: (0, 0))],
        out_specs=pl.BlockSpec((tm, d), lambda i: (i, 0)),
        out_shape=jax.ShapeDtypeStruct((t, d), F32),
        compiler_params=_params(("parallel",)),
        name="oproj",
    )(x, attn, ada, w_o)


def kernel(x, c, norm_g, w_ada, b_ada, w_ffn_in, w_ffn_out, w_conv_in, conv_w, conv_b, w_conv_out,
           kv_norm_g, w_ada_kv, b_ada_kv, w_kvf, b_fgate, w_q, w_o, final_g):
    b, s, d = x.shape
    depth = norm_g.shape[0]
    n_a = w_conv_in.shape[0]
    t = b * s

    w_ffn_in_b = w_ffn_in.astype(BF16)
    w_ffn_out_b = w_ffn_out.astype(BF16)
    w_conv_in_b = w_conv_in.astype(BF16)
    w_conv_out_b = w_conv_out.astype(BF16)
    w_kv_b = w_kvf[:, :2 * d].astype(BF16)
    w_f_b = jnp.pad(w_kvf[:, 2 * d:], ((0, 0), (0, LANES - N_HEADS))).astype(BF16)
    b_f = jnp.pad(b_fgate, (0, LANES - N_HEADS)).reshape(1, LANES)
    w_q_b = w_q.astype(BF16)
    w_o_b = w_o.astype(BF16)

    c_pad = jnp.pad(c, ((0, SUBLANES - b), (0, 0)))
    xf = x.reshape(t, d)
    for l in range(depth):
        ada = _ada(c_pad, w_ada[l], b_ada[l])[:b].reshape(b, 9, d)
        xf = _ffn(xf, ada, norm_g[l, 0], w_ffn_in_b[l, 0], w_ffn_out_b[l, 0], 0, s)
        if l < n_a:
            xf = _conv(xf, ada, norm_g[l, 1], w_conv_in_b[l], conv_w[l], conv_b[l], w_conv_out_b[l], s)
        else:
            j = l - n_a
            if j == 0:
                ada_kv = _ada(c_pad, w_ada_kv, b_ada_kv)[:b].reshape(b, 2, d)
            q, k, v, fc, ft = _qkv(xf, ada, ada_kv, norm_g[l, 1], kv_norm_g, w_q_b[j], w_kv_b, w_f_b, b_f, s)
            attn = _attn(q.reshape(b, s, d), k.reshape(b, s, d), v.reshape(b, s, d),
                         fc.reshape(b, s, LANES), ft)
            xf = _oproj(xf, attn.reshape(t, d), ada, w_o_b[j], s)
        last = l == depth - 1
        xf = _ffn(xf, ada, norm_g[l, 2], w_ffn_in_b[l, 1], w_ffn_out_b[l, 1], 2, s,
                  final_g=final_g if last else None)
    return xf.reshape(b, s, d)
```

```python
import functools
import math

import jax
import jax.numpy as jnp
from jax import lax
from jax.experimental import pallas as pl
from jax.experimental.pallas import tpu as pltpu

EPS = 1e-6
N_HEADS = 16
HEAD_DIM = 128
CONV_WIDTH = 3
LOG2E = math.log2(math.e)
LANES = 128
SUBLANES = 8
MXU_DIM = 256
VMEM_LIMIT_BYTES = 56 * 1024 * 1024

F32 = jnp.float32
BF16 = jnp.bfloat16


def _params(semantics):
    return pltpu.CompilerParams(dimension_semantics=semantics,
                                vmem_limit_bytes=VMEM_LIMIT_BYTES)


def _dot(a, b):
    return jnp.dot(a, b, preferred_element_type=F32)


def _rmsnorm(x, g):
    return x * lax.rsqrt(jnp.mean(x * x, axis=-1, keepdims=True) + EPS) * g


def _modulated_norm(x, g, ada_ref, sub):
    shift = ada_ref[0, 3 * sub:3 * sub + 1, :]
    scale = ada_ref[0, 3 * sub + 1:3 * sub + 2, :]
    return _rmsnorm(x, g) * (1.0 + scale) + shift


def _ada_kernel(c_ref, w_ref, b_ref, o_ref):
    c = c_ref[...]
    cond = c * jax.nn.sigmoid(c)
    o_ref[...] = _dot(cond.astype(BF16), w_ref[...].astype(BF16)) + b_ref[...]


def _ada(c_pad, w, b, l, tn=1024):
    nl, d, n = w.shape
    return pl.pallas_call(
        _ada_kernel,
        grid=(n // tn,),
        in_specs=[pl.BlockSpec((SUBLANES, d), lambda j: (0, 0)),
                  pl.BlockSpec((None, d, tn), lambda j: (l, 0, j)),
                  pl.BlockSpec((None, 1, tn), lambda j: (l, 0, j))],
        out_specs=pl.BlockSpec((SUBLANES, tn), lambda j: (0, j)),
        out_shape=jax.ShapeDtypeStruct((SUBLANES, n), F32),
        compiler_params=_params(("parallel",)),
        name="ada",
    )(c_pad, w, b.reshape(nl, 1, n))


def _ffn_kernel(x_ref, ada_ref, g_ref, wg_ref, wu_ref, wo_ref, *rest, sub, final):
    if final:
        fg_ref, o_ref, h_ref = rest
    else:
        o_ref, h_ref = rest
    f = pl.program_id(1)

    @pl.when(f == 0)
    def _():
        h_ref[...] = _modulated_norm(x_ref[...], g_ref[...], ada_ref, sub).astype(BF16)
        o_ref[...] = jnp.zeros_like(o_ref)

    h = h_ref[...]
    a = _dot(h, wg_ref[...])
    b = _dot(h, wu_ref[...])
    act = (a * jax.nn.sigmoid(a) * b).astype(BF16)
    o_ref[...] += _dot(act, wo_ref[...])

    @pl.when(f == pl.num_programs(1) - 1)
    def _():
        gate = ada_ref[0, 3 * sub + 2:3 * sub + 3, :]
        y = x_ref[...] + (0.5 * gate) * o_ref[...]
        if final:
            y = _rmsnorm(y, fg_ref[...])
        o_ref[...] = y


def _ffn(x, ada, g, w_in, w_out, l, idx, seq, final_g=None, tm=512, tf=512):
    t, d = x.shape
    ff = w_out.shape[2]
    nf = ff // tf
    per_batch = seq // tm
    sub = 2 * idx
    final = final_g is not None
    in_specs = [pl.BlockSpec((tm, d), lambda i, f: (i, 0)),
                pl.BlockSpec((1, 9, d), lambda i, f: (i // per_batch, 0, 0)),
                pl.BlockSpec((1, d), lambda i, f: (0, 0)),
                pl.BlockSpec((None, None, d, tf), lambda i, f: (l, idx, 0, f)),
                pl.BlockSpec((None, None, d, tf), lambda i, f: (l, idx, 0, nf + f)),
                pl.BlockSpec((None, None, tf, d), lambda i, f: (l, idx, f, 0))]
    args = [x, ada, g.reshape(1, d), w_in, w_in, w_out]
    if final:
        in_specs.append(pl.BlockSpec((1, d), lambda i, f: (0, 0)))
        args.append(final_g.reshape(1, d))
    return pl.pallas_call(
        functools.partial(_ffn_kernel, sub=sub, final=final),
        grid=(t // tm, nf),
        in_specs=in_specs,
        out_specs=pl.BlockSpec((tm, d), lambda i, f: (i, 0)),
        out_shape=jax.ShapeDtypeStruct((t, d), F32),
        scratch_shapes=[pltpu.VMEM((tm, d), BF16)],
        compiler_params=_params(("parallel", "arbitrary")),
        name="ffn_final" if final else "ffn",
    )(*args)


def _conv_kernel(x_ref, ada_ref, g_ref, wb_ref, wc_ref, wx_ref, cw_ref, cb_ref, wo_ref,
                 o_ref, h_ref, tail_ref, *, per_batch):
    i = pl.program_id(0)
    c = pl.program_id(1)
    tm = x_ref.shape[0]

    @pl.when(c == 0)
    def _():
        h_ref[...] = _modulated_norm(x_ref[...], g_ref[...], ada_ref, 1).astype(BF16)
        o_ref[...] = jnp.zeros_like(o_ref)

    @pl.when(i % per_batch == 0)
    def _():
        tail_ref[c] = jnp.zeros(tail_ref.shape[1:], F32)

    h = h_ref[...]
    bg = _dot(h, wb_ref[...])
    cg = _dot(h, wc_ref[...])
    xv = _dot(h, wx_ref[...])
    u = cg * xv
    tail = tail_ref[c]
    p1 = tail[SUBLANES - 1:SUBLANES, :]
    p2 = tail[SUBLANES - 2:SUBLANES - 1, :]
    row = lax.broadcasted_iota(jnp.int32, u.shape, 0)
    u1 = jnp.where(row == 0, p1, pltpu.roll(u, 1, 0))
    u2 = jnp.where(row == 0, p2, jnp.where(row == 1, p1, pltpu.roll(u, 2, 0)))
    conv = cw_ref[0:1, :] * u2 + cw_ref[1:2, :] * u1 + cw_ref[2:3, :] * u + cb_ref[...]
    tail_ref[c] = u[tm - SUBLANES:, :]
    o_ref[...] += _dot((bg * conv).astype(BF16), wo_ref[...])

    @pl.when(c == pl.num_programs(1) - 1)
    def _():
        gate = ada_ref[0, 5:6, :]
        o_ref[...] = x_ref[...] + gate * o_ref[...]


def _conv(x, ada, g, w_in, conv_w, conv_b, w_out, l, seq, tm=512, tc=512):
    t, d = x.shape
    nc = d // tc
    per_batch = seq // tm
    return pl.pallas_call(
        functools.partial(_conv_kernel, per_batch=per_batch),
        grid=(t // tm, nc),
        in_specs=[pl.BlockSpec((tm, d), lambda i, c: (i, 0)),
                  pl.BlockSpec((1, 9, d), lambda i, c: (i // per_batch, 0, 0)),
                  pl.BlockSpec((1, d), lambda i, c: (0, 0)),
                  pl.BlockSpec((None, d, tc), lambda i, c: (l, 0, c)),
                  pl.BlockSpec((None, d, tc), lambda i, c: (l, 0, nc + c)),
                  pl.BlockSpec((None, d, tc), lambda i, c: (l, 0, 2 * nc + c)),
                  pl.BlockSpec((None, CONV_WIDTH, tc), lambda i, c: (l, 0, c)),
                  pl.BlockSpec((None, 1, tc), lambda i, c: (l, 0, c)),
                  pl.BlockSpec((None, tc, d), lambda i, c: (l, c, 0))],
        out_specs=pl.BlockSpec((tm, d), lambda i, c: (i, 0)),
        out_shape=jax.ShapeDtypeStruct((t, d), F32),
        scratch_shapes=[pltpu.VMEM((tm, d), BF16),
                        pltpu.VMEM((nc, SUBLANES, tc), F32)],
        compiler_params=_params(("arbitrary", "arbitrary")),
        name="conv",
    )(x, ada, g.reshape(1, d), w_in, w_in, w_in, conv_w, conv_b, w_out)


def _qkv_kernel(x_ref, ada_ref, adakv_ref, g_ref, gkv_ref, wq_ref, wk_ref, wv_ref, wf_ref, bf_ref,
                q_ref, k_ref, v_ref, fc_ref, hq_ref, hkv_ref, carry_ref, *, per_batch, q_scale):
    i = pl.program_id(0)
    c = pl.program_id(1)
    tm = x_ref.shape[0]

    @pl.when(c == 0)
    def _():
        x = x_ref[...]
        hq_ref[...] = _modulated_norm(x, g_ref[...], ada_ref, 1).astype(BF16)
        hkv = (_rmsnorm(x, gkv_ref[...]) * (1.0 + adakv_ref[0, 1:2, :]) + adakv_ref[0, 0:1, :]).astype(BF16)
        hkv_ref[...] = hkv

        @pl.when(i % per_batch == 0)
        def _():
            carry_ref[...] = jnp.zeros_like(carry_ref)

        zf = _dot(hkv, wf_ref[...]) + bf_ref[...]
        ls = jnp.minimum(zf, 0.0) - jnp.log1p(jnp.exp(-jnp.abs(zf)))
        hi = ls.astype(BF16)
        r1 = ls - hi.astype(F32)
        mid = r1.astype(BF16)
        lo = (r1 - mid.astype(F32)).astype(BF16)
        rr = lax.broadcasted_iota(jnp.int32, (tm, tm), 0)
        cc = lax.broadcasted_iota(jnp.int32, (tm, tm), 1)
        tri = (rr >= cc).astype(BF16)
        cum = _dot(tri, hi) + _dot(tri, mid) + _dot(tri, lo) + carry_ref[0:1, :]
        carry_ref[...] = jnp.broadcast_to(cum[tm - 1:tm, :], carry_ref.shape)
        fc_ref[...] = cum

    hkv = hkv_ref[...]
    q_ref[...] = (_dot(hq_ref[...], wq_ref[...]) * q_scale).astype(BF16)
    k_ref[...] = _dot(hkv, wk_ref[...]).astype(BF16)
    v_ref[...] = _dot(hkv, wv_ref[...]).astype(BF16)


def _qkv(x, ada, ada_kv, g, gkv, w_q, w_kv, w_f, b_f, j, seq, tm=512, tn=512):
    t, d = x.shape
    nn = d // tn
    per_batch = seq // tm
    q_scale = LOG2E / math.sqrt(HEAD_DIM)
    row_blk = pl.BlockSpec((tm, tn), lambda i, c: (i, c))
    return pl.pallas_call(
        functools.partial(_qkv_kernel, per_batch=per_batch, q_scale=q_scale),
        grid=(t // tm, nn),
        in_specs=[pl.BlockSpec((tm, d), lambda i, c: (i, 0)),
                  pl.BlockSpec((1, 9, d), lambda i, c: (i // per_batch, 0, 0)),
                  pl.BlockSpec((1, 2, d), lambda i, c: (i // per_batch, 0, 0)),
                  pl.BlockSpec((1, d), lambda i, c: (0, 0)),
                  pl.BlockSpec((1, d), lambda i, c: (0, 0)),
                  pl.BlockSpec((None, d, tn), lambda i, c: (j, 0, c)),
                  pl.BlockSpec((d, tn), lambda i, c: (0, c)),
                  pl.BlockSpec((d, tn), lambda i, c: (0, nn + c)),
                  pl.BlockSpec((d, LANES), lambda i, c: (0, 0)),
                  pl.BlockSpec((1, LANES), lambda i, c: (0, 0))],
        out_specs=[row_blk, row_blk, row_blk,
                   pl.BlockSpec((tm, LANES), lambda i, c: (i, 0))],
        out_shape=[jax.ShapeDtypeStruct((t, d), BF16),
                   jax.ShapeDtypeStruct((t, d), BF16),
                   jax.ShapeDtypeStruct((t, d), BF16),
                   jax.ShapeDtypeStruct((t, LANES), F32)],
        scratch_shapes=[pltpu.VMEM((tm, d), BF16),
                        pltpu.VMEM((tm, d), BF16),
                        pltpu.VMEM((SUBLANES, LANES), F32)],
        compiler_params=_params(("arbitrary", "arbitrary")),
        name="qkv",
    )(x, ada, ada_kv, g.reshape(1, d), gkv.reshape(1, d), w_q, w_kv, w_kv, w_f, b_f)


def _split3(f):
    hi = f.astype(BF16)
    r1 = f - hi.astype(F32)
    mid = r1.astype(BF16)
    lo = (r1 - mid.astype(F32)).astype(BF16)
    return hi.astype(F32), mid.astype(F32), lo.astype(F32)


def _attn_kernel(q_ref, k_ref, v_ref, fc_ref, o_ref, kaug_ref, vt_ref, st_ref, *, tq, tk, hp):
    hg = pl.program_id(1)
    i = pl.program_id(2)
    seq = k_ref.shape[1]

    def head_col(blk, head):
        lane = lax.broadcasted_iota(jnp.int32, blk.shape, 1)
        return jnp.sum(jnp.where(lane == head, blk, 0.0), axis=1, keepdims=True) * LOG2E

    @pl.when(i == 0)
    def _():
        lane = lax.broadcasted_iota(jnp.int32, (tk, LANES), 1)

        def build(r, _):
            start = pl.multiple_of(r * tk, tk)
            blk = fc_ref[0, pl.ds(start, tk), :]
            for hh in range(hp):
                hi, mid, lo = _split3(head_col(blk, hg * hp + hh))
                aug = jnp.where(lane < 3, 1.0,
                                jnp.where(lane == 3, -hi,
                                          jnp.where(lane == 4, -mid, jnp.where(lane == 5, -lo, 0.0))))
                kaug_ref[hh, pl.ds(start, tk), :HEAD_DIM] = k_ref[0, pl.ds(start, tk),
                                                                  hh * HEAD_DIM:(hh + 1) * HEAD_DIM]
                kaug_ref[hh, pl.ds(start, tk), HEAD_DIM:] = aug.astype(BF16)
                vblk = v_ref[0, pl.ds(start, tk), hh * HEAD_DIM:(hh + 1) * HEAD_DIM]
                vt_ref[hh, :, pl.ds(start, tk)] = vblk.astype(F32).T.astype(BF16)
            return 0
        lax.fori_loop(0, seq // tk, build, 0)

    fblk = fc_ref[0, pl.ds(pl.multiple_of(i * tq, tq), tq), :]
    lane = lax.broadcasted_iota(jnp.int32, (tq, LANES), 1)
    qts = []
    for hh in range(hp):
        hi, mid, lo = _split3(head_col(fblk, hg * hp + hh))
        aq = jnp.where(lane == 0, hi,
                       jnp.where(lane == 1, mid,
                                 jnp.where(lane == 2, lo, jnp.where(lane < 6, 1.0, 0.0))))
        qh = q_ref[0, :, hh * HEAD_DIM:(hh + 1) * HEAD_DIM].astype(F32)
        qts.append(jnp.concatenate([qh.T, aq.T], axis=0).astype(BF16))

    def scores(j, slot):
        start = pl.multiple_of(j * tk, tk)
        for hh in range(hp):
            st_ref[slot, hh] = _dot(kaug_ref[hh, pl.ds(start, tk), :], qts[hh])

    def softmax_pv(j, slot, carry, diag):
        start = pl.multiple_of(j * tk, tk)
        out = []
        for hh in range(hp):
            m, l, acc = carry[hh]
            st = st_ref[slot, hh]
            if diag is not None:
                key = lax.broadcasted_iota(jnp.int32, (tk, tq), 0) + diag * tk
                qry = lax.broadcasted_iota(jnp.int32, (tk, tq), 1)
                st = jnp.where(key <= qry, st, -jnp.inf)
            m_new = jnp.maximum(m, jnp.max(st, axis=0, keepdims=True))
            alpha = jnp.exp2(m - m_new)
            pt = jnp.exp2(st - m_new)
            l = alpha * l + jnp.sum(pt, axis=0, keepdims=True)
            acc = alpha * acc + _dot(vt_ref[hh, :, pl.ds(start, tk)], pt.astype(BF16))
            out.append((m_new, l, acc))
        return tuple(out)

    def pair(jj, carry):
        a = 2 * jj
        scores(a + 1, 1)
        carry = softmax_pv(a, 0, carry, None)
        scores(a + 2, 0)
        return softmax_pv(a + 1, 1, carry, None)

    init = tuple((jnp.full((1, tq), -jnp.inf, F32), jnp.zeros((1, tq), F32),
                  jnp.zeros((HEAD_DIM, tq), F32)) for _ in range(hp))
    scores(0, 0)
    carry = lax.fori_loop(0, i, pair, init)
    scores(2 * i + 1, 1)
    carry = softmax_pv(2 * i, 0, carry, 0)
    carry = softmax_pv(2 * i + 1, 1, carry, 1)
    for hh in range(hp):
        _, l, acc = carry[hh]
        o_ref[0, :, hh * HEAD_DIM:(hh + 1) * HEAD_DIM] = (acc / l).T.astype(BF16)


def _attn(q, k, v, fc, tq=1024, hp=2):
    b, s, d = q.shape
    w = hp * HEAD_DIM
    tk = tq // 2
    return pl.pallas_call(
        functools.partial(_attn_kernel, tq=tq, tk=tk, hp=hp),
        grid=(b, N_HEADS // hp, s // tq),
        in_specs=[pl.BlockSpec((1, tq, w), lambda bb, h, i: (bb, i, h)),
                  pl.BlockSpec((1, s, w), lambda bb, h, i: (bb, 0, h)),
                  pl.BlockSpec((1, s, w), lambda bb, h, i: (bb, 0, h)),
                  pl.BlockSpec((1, s, LANES), lambda bb, h, i: (bb, 0, 0))],
        out_specs=pl.BlockSpec((1, tq, w), lambda bb, h, i: (bb, i, h)),
        out_shape=jax.ShapeDtypeStruct((b, s, d), BF16),
        scratch_shapes=[pltpu.VMEM((hp, s, 2 * HEAD_DIM), BF16),
                        pltpu.VMEM((hp, HEAD_DIM, s), BF16),
                        pltpu.VMEM((2, hp, tk, tq), F32)],
        compiler_params=_params(("parallel", "parallel", "arbitrary")),
        name="attn",
    )(q, k, v, fc)


def _oproj_kernel(x_ref, a_ref, ada_ref, w_ref, o_ref):
    gate = ada_ref[0, 5:6, :]
    o_ref[...] = x_ref[...] + gate * _dot(a_ref[...], w_ref[...])


def _oproj(x, attn, ada, w_o, j, seq, tm=512):
    t, d = x.shape
    per_batch = seq // tm
    return pl.pallas_call(
        _oproj_kernel,
        grid=(t // tm,),
        in_specs=[pl.BlockSpec((tm, d), lambda i: (i, 0)),
                  pl.BlockSpec((tm, d), lambda i: (i, 0)),
                  pl.BlockSpec((1, 9, d), lambda i: (i // per_batch, 0, 0)),
                  pl.BlockSpec((None, d, d), lambda i: (j, 0, 0))],
        out_specs=pl.BlockSpec((tm, d), lambda i: (i, 0)),
        out_shape=jax.ShapeDtypeStruct((t, d), F32),
        compiler_params=_params(("parallel",)),
        name="oproj",
    )(x, attn, ada, w_o)


def kernel(x, c, norm_g, w_ada, b_ada, w_ffn_in, w_ffn_out, w_conv_in, conv_w, conv_b, w_conv_out,
           kv_norm_g, w_ada_kv, b_ada_kv, w_kvf, b_fgate, w_q, w_o, final_g):
    b, s, d = x.shape
    depth = norm_g.shape[0]
    n_a = w_conv_in.shape[0]
    t = b * s
    assert w_q.shape[0] == 1 and depth == n_a + 1

    w_ffn_in_b = w_ffn_in.astype(BF16)
    w_ffn_out_b = w_ffn_out.astype(BF16)
    w_conv_in_b = w_conv_in.astype(BF16)
    w_conv_out_b = w_conv_out.astype(BF16)
    w_kv_b = w_kvf[:, :2 * d].astype(BF16)
    w_f_b = jnp.pad(w_kvf[:, 2 * d:], ((0, 0), (0, LANES - N_HEADS))).astype(BF16)
    b_f = jnp.pad(b_fgate, (0, LANES - N_HEADS)).reshape(1, LANES)
    w_q_b = w_q.astype(BF16)
    w_o_b = w_o.astype(BF16)
    conv_b3 = conv_b.reshape(n_a, 1, d)

    c_pad = jnp.pad(c, ((0, SUBLANES - b), (0, 0)))
    xf = x.reshape(t, d)
    for l in range(depth):
        ada = _ada(c_pad, w_ada, b_ada, l)[:b].reshape(b, 9, d)
        xf = _ffn(xf, ada, norm_g[l, 0], w_ffn_in_b, w_ffn_out_b, l, 0, s)
        if l < n_a:
            xf = _conv(xf, ada, norm_g[l, 1], w_conv_in_b, conv_w, conv_b3, w_conv_out_b, l, s)
        else:
            j = l - n_a
            ada_kv = _ada(c_pad, w_ada_kv[None], b_ada_kv[None], 0)[:b].reshape(b, 2, d)
            q, k, v, fc = _qkv(xf, ada, ada_kv, norm_g[l, 1], kv_norm_g, w_q_b, w_kv_b, w_f_b, b_f, j, s)
            attn = _attn(q.reshape(b, s, d), k.reshape(b, s, d), v.reshape(b, s, d),
                         fc.reshape(b, s, LANES))
            xf = _oproj(xf, attn.reshape(t, d), ada, w_o_b, j, s)
        last = l == depth - 1
        xf = _ffn(xf, ada, norm_g[l, 2], w_ffn_in_b, w_ffn_out_b, l, 1, s,
                  final_g=final_g if last else None)
    return xf.reshape(b, s, d)
```

```python
import functools
import math

import jax
import jax.numpy as jnp
from jax import lax
from jax.experimental import pallas as pl
from jax.experimental.pallas import tpu as pltpu

EPS = 1e-6
N_HEADS = 16
HEAD_DIM = 128
CONV_WIDTH = 3
LOG2E = math.log2(math.e)
LANES = 128
SUBLANES = 8
MXU_DIM = 256
VMEM_LIMIT_BYTES = 56 * 1024 * 1024

F32 = jnp.float32
BF16 = jnp.bfloat16


def _params(semantics):
    return pltpu.CompilerParams(dimension_semantics=semantics,
                                vmem_limit_bytes=VMEM_LIMIT_BYTES)


def _dot(a, b):
    return jnp.dot(a, b, preferred_element_type=F32)


def _rmsnorm(x, g):
    return x * lax.rsqrt(jnp.mean(x * x, axis=-1, keepdims=True) + EPS) * g


NORM_ROWS = 16
NORM_UNROLL = 8
SUB_ROWS = 128


def _modulated_norm_to(h_ref, x_ref, g_ref, shift, scale):
    g = g_ref[...]
    scale1 = 1.0 + scale

    def body(r, _):
        rows = pl.ds(pl.multiple_of(r * NORM_ROWS, NORM_ROWS), NORM_ROWS)
        h_ref[rows, :] = (_rmsnorm(x_ref[rows, :], g) * scale1 + shift).astype(BF16)
        return 0
    lax.fori_loop(0, x_ref.shape[0] // NORM_ROWS, body, 0, unroll=NORM_UNROLL)


def _row_subblocks(tm):
    return [pl.ds(r, SUB_ROWS) for r in range(0, tm, SUB_ROWS)]


def _ada_rows(ada_ref, sub):
    return tuple(ada_ref[0, 3 * sub + k:3 * sub + k + 1, :] for k in range(3))


def _ada_kernel(c_ref, w_ref, b_ref, o_ref):
    c = c_ref[...]
    cond = c * jax.nn.sigmoid(c)
    o_ref[...] = _dot(cond.astype(BF16), w_ref[...].astype(BF16)) + b_ref[...]


def _ada(c_pad, w, b, l, tn=1024):
    nl, d, n = w.shape
    return pl.pallas_call(
        _ada_kernel,
        grid=(n // tn,),
        in_specs=[pl.BlockSpec((SUBLANES, d), lambda j: (0, 0)),
                  pl.BlockSpec((None, d, tn), lambda j: (l, 0, j)),
                  pl.BlockSpec((None, 1, tn), lambda j: (l, 0, j))],
        out_specs=pl.BlockSpec((SUBLANES, tn), lambda j: (0, j)),
        out_shape=jax.ShapeDtypeStruct((SUBLANES, n), F32),
        compiler_params=_params(("parallel",)),
        name="ada",
    )(c_pad, w, b.reshape(nl, 1, n))


def _ffn_kernel(x_ref, ada_ref, g_ref, wg_ref, wu_ref, wo_ref, *rest, sub, final):
    if final:
        fg_ref, o_ref, h_ref = rest
    else:
        o_ref, h_ref = rest
    f = pl.program_id(1)

    shift, scale, gate = _ada_rows(ada_ref, sub)

    @pl.when(f == 0)
    def _():
        _modulated_norm_to(h_ref, x_ref, g_ref, shift, scale)
        o_ref[...] = jnp.zeros_like(o_ref)

    subs = _row_subblocks(x_ref.shape[0])
    ab = [(_dot(h_ref[rows, :], wg_ref[...]), _dot(h_ref[rows, :], wu_ref[...])) for rows in subs]
    for rows, (a, b) in zip(subs, ab):
        act = (a * jax.nn.sigmoid(a) * b).astype(BF16)
        o_ref[rows, :] += _dot(act, wo_ref[...])

    @pl.when(f == pl.num_programs(1) - 1)
    def _():
        y = x_ref[...] + (0.5 * gate) * o_ref[...]
        if final:
            y = _rmsnorm(y, fg_ref[...])
        o_ref[...] = y


def _ffn(x, ada, g, w_in, w_out, l, idx, seq, final_g=None, tm=512, tf=512):
    t, d = x.shape
    ff = w_out.shape[2]
    nf = ff // tf
    per_batch = seq // tm
    sub = 2 * idx
    final = final_g is not None
    in_specs = [pl.BlockSpec((tm, d), lambda i, f: (i, 0)),
                pl.BlockSpec((1, 9, d), lambda i, f: (i // per_batch, 0, 0)),
                pl.BlockSpec((1, d), lambda i, f: (0, 0)),
                pl.BlockSpec((None, None, d, tf), lambda i, f: (l, idx, 0, f)),
                pl.BlockSpec((None, None, d, tf), lambda i, f: (l, idx, 0, nf + f)),
                pl.BlockSpec((None, None, tf, d), lambda i, f: (l, idx, f, 0))]
    args = [x, ada, g.reshape(1, d), w_in, w_in, w_out]
    if final:
        in_specs.append(pl.BlockSpec((1, d), lambda i, f: (0, 0)))
        args.append(final_g.reshape(1, d))
    return pl.pallas_call(
        functools.partial(_ffn_kernel, sub=sub, final=final),
        grid=(t // tm, nf),
        in_specs=in_specs,
        out_specs=pl.BlockSpec((tm, d), lambda i, f: (i, 0)),
        out_shape=jax.ShapeDtypeStruct((t, d), F32),
        scratch_shapes=[pltpu.VMEM((tm, d), BF16)],
        compiler_params=_params(("parallel", "arbitrary")),
        name="ffn_final" if final else "ffn",
    )(*args)


def _conv_kernel(x_ref, ada_ref, g_ref, wb_ref, wc_ref, wx_ref, cw_ref, cb_ref, wo_ref,
                 o_ref, h_ref, tail_ref, *, per_batch):
    i = pl.program_id(0)
    c = pl.program_id(1)
    tm = x_ref.shape[0]

    shift, scale, gate = _ada_rows(ada_ref, 1)

    @pl.when(c == 0)
    def _():
        _modulated_norm_to(h_ref, x_ref, g_ref, shift, scale)
        o_ref[...] = jnp.zeros_like(o_ref)

    @pl.when(i % per_batch == 0)
    def _():
        tail_ref[c] = jnp.zeros(tail_ref.shape[1:], F32)

    subs = _row_subblocks(tm)
    proj = [tuple(_dot(h_ref[rows, :], w[...]) for w in (wb_ref, wc_ref, wx_ref)) for rows in subs]
    tail = tail_ref[c]
    row = lax.broadcasted_iota(jnp.int32, (SUB_ROWS, tail.shape[1]), 0)
    for rows, (bg, cg, xv) in zip(subs, proj):
        u = cg * xv
        p1 = tail[SUBLANES - 1:SUBLANES, :]
        p2 = tail[SUBLANES - 2:SUBLANES - 1, :]
        u1 = jnp.where(row == 0, p1, pltpu.roll(u, 1, 0))
        u2 = jnp.where(row == 0, p2, jnp.where(row == 1, p1, pltpu.roll(u, 2, 0)))
        conv = cw_ref[0:1, :] * u2 + cw_ref[1:2, :] * u1 + cw_ref[2:3, :] * u + cb_ref[...]
        tail = u[SUB_ROWS - SUBLANES:, :]
        o_ref[rows, :] += _dot((bg * conv).astype(BF16), wo_ref[...])
    tail_ref[c] = tail

    @pl.when(c == pl.num_programs(1) - 1)
    def _():
        o_ref[...] = x_ref[...] + gate * o_ref[...]


def _conv(x, ada, g, w_in, conv_w, conv_b, w_out, l, seq, tm=512, tc=512):
    t, d = x.shape
    nc = d // tc
    per_batch = seq // tm
    return pl.pallas_call(
        functools.partial(_conv_kernel, per_batch=per_batch),
        grid=(t // tm, nc),
        in_specs=[pl.BlockSpec((tm, d), lambda i, c: (i, 0)),
                  pl.BlockSpec((1, 9, d), lambda i, c: (i // per_batch, 0, 0)),
                  pl.BlockSpec((1, d), lambda i, c: (0, 0)),
                  pl.BlockSpec((None, d, tc), lambda i, c: (l, 0, c)),
                  pl.BlockSpec((None, d, tc), lambda i, c: (l, 0, nc + c)),
                  pl.BlockSpec((None, d, tc), lambda i, c: (l, 0, 2 * nc + c)),
                  pl.BlockSpec((None, CONV_WIDTH, tc), lambda i, c: (l, 0, c)),
                  pl.BlockSpec((None, 1, tc), lambda i, c: (l, 0, c)),
                  pl.BlockSpec((None, tc, d), lambda i, c: (l, c, 0))],
        out_specs=pl.BlockSpec((tm, d), lambda i, c: (i, 0)),
        out_shape=jax.ShapeDtypeStruct((t, d), F32),
        scratch_shapes=[pltpu.VMEM((tm, d), BF16),
                        pltpu.VMEM((nc, SUBLANES, tc), F32)],
        compiler_params=_params(("arbitrary", "arbitrary")),
        name="conv",
    )(x, ada, g.reshape(1, d), w_in, w_in, w_in, conv_w, conv_b, w_out)


def _qkv_kernel(x_ref, ada_ref, adakv_ref, g_ref, gkv_ref, wq_ref, wk_ref, wv_ref, wf_ref, bf_ref,
                q_ref, k_ref, v_ref, fc_ref, hq_ref, hkv_ref, carry_ref, *, per_batch, q_scale):
    i = pl.program_id(0)
    c = pl.program_id(1)
    tm = x_ref.shape[0]

    @pl.when(c == 0)
    def _():
        shift, scale, _ = _ada_rows(ada_ref, 1)
        _modulated_norm_to(hq_ref, x_ref, g_ref, shift, scale)
        _modulated_norm_to(hkv_ref, x_ref, gkv_ref, adakv_ref[0, 0:1, :], adakv_ref[0, 1:2, :])
        hkv = hkv_ref[...]

        @pl.when(i % per_batch == 0)
        def _():
            carry_ref[...] = jnp.zeros_like(carry_ref)

        zf = _dot(hkv, wf_ref[...]) + bf_ref[...]
        ls = jnp.minimum(zf, 0.0) - jnp.log1p(jnp.exp(-jnp.abs(zf)))
        hi = ls.astype(BF16)
        r1 = ls - hi.astype(F32)
        mid = r1.astype(BF16)
        lo = (r1 - mid.astype(F32)).astype(BF16)
        rr = lax.broadcasted_iota(jnp.int32, (tm, tm), 0)
        cc = lax.broadcasted_iota(jnp.int32, (tm, tm), 1)
        tri = (rr >= cc).astype(BF16)
        parts = _dot(tri, jnp.concatenate([hi, mid, lo], axis=1))
        cum = (parts[:, :LANES] + parts[:, LANES:2 * LANES]) + parts[:, 2 * LANES:] + carry_ref[0:1, :]
        carry_ref[...] = jnp.broadcast_to(cum[tm - 1:tm, :], carry_ref.shape)
        fc_ref[...] = cum

    hkv = hkv_ref[...]
    q_ref[...] = (_dot(hq_ref[...], wq_ref[...]) * q_scale).astype(BF16)
    k_ref[...] = _dot(hkv, wk_ref[...]).astype(BF16)
    v_ref[...] = _dot(hkv, wv_ref[...]).astype(BF16)


def _qkv(x, ada, ada_kv, g, gkv, w_q, w_kv, w_f, b_f, j, seq, tm=512, tn=512):
    t, d = x.shape
    nn = d // tn
    per_batch = seq // tm
    q_scale = LOG2E / math.sqrt(HEAD_DIM)
    row_blk = pl.BlockSpec((tm, tn), lambda i, c: (i, c))
    return pl.pallas_call(
        functools.partial(_qkv_kernel, per_batch=per_batch, q_scale=q_scale),
        grid=(t // tm, nn),
        in_specs=[pl.BlockSpec((tm, d), lambda i, c: (i, 0)),
                  pl.BlockSpec((1, 9, d), lambda i, c: (i // per_batch, 0, 0)),
                  pl.BlockSpec((1, 2, d), lambda i, c: (i // per_batch, 0, 0)),
                  pl.BlockSpec((1, d), lambda i, c: (0, 0)),
                  pl.BlockSpec((1, d), lambda i, c: (0, 0)),
                  pl.BlockSpec((None, d, tn), lambda i, c: (j, 0, c)),
                  pl.BlockSpec((d, tn), lambda i, c: (0, c)),
                  pl.BlockSpec((d, tn), lambda i, c: (0, nn + c)),
                  pl.BlockSpec((d, LANES), lambda i, c: (0, 0)),
                  pl.BlockSpec((1, LANES), lambda i, c: (0, 0))],
        out_specs=[row_blk, row_blk, row_blk,
                   pl.BlockSpec((tm, LANES), lambda i, c: (i, 0))],
        out_shape=[jax.ShapeDtypeStruct((t, d), BF16),
                   jax.ShapeDtypeStruct((t, d), BF16),
                   jax.ShapeDtypeStruct((t, d), BF16),
                   jax.ShapeDtypeStruct((t, LANES), F32)],
        scratch_shapes=[pltpu.VMEM((tm, d), BF16),
                        pltpu.VMEM((tm, d), BF16),
                        pltpu.VMEM((SUBLANES, LANES), F32)],
        compiler_params=_params(("arbitrary", "arbitrary")),
        name="qkv",
    )(x, ada, ada_kv, g.reshape(1, d), gkv.reshape(1, d), w_q, w_kv, w_kv, w_f, b_f)


def _split3(f):
    hi = f.astype(BF16)
    r1 = f - hi.astype(F32)
    mid = r1.astype(BF16)
    lo = (r1 - mid.astype(F32)).astype(BF16)
    return hi.astype(F32), mid.astype(F32), lo.astype(F32)


def _attn_kernel(q_ref, k_ref, v_ref, fc_ref, o_ref, kaug_ref, vt_ref, st_ref, mx_ref, *, tq, tk, hp):
    hg = pl.program_id(1)
    i = pl.program_id(2)
    seq = k_ref.shape[1]

    def head_col(blk, head):
        lane = lax.broadcasted_iota(jnp.int32, blk.shape, 1)
        return jnp.sum(jnp.where(lane == head, blk, 0.0), axis=1, keepdims=True) * LOG2E

    @pl.when(i == 0)
    def _():
        lane = lax.broadcasted_iota(jnp.int32, (tk, LANES), 1)

        def build(r, _):
            start = pl.multiple_of(r * tk, tk)
            blk = fc_ref[0, pl.ds(start, tk), :]
            for hh in range(hp):
                hi, mid, lo = _split3(head_col(blk, hg * hp + hh))
                aug = jnp.where(lane < 3, 1.0,
                                jnp.where(lane == 3, -hi,
                                          jnp.where(lane == 4, -mid, jnp.where(lane == 5, -lo, 0.0))))
                kaug_ref[hh, pl.ds(start, tk), :HEAD_DIM] = k_ref[0, pl.ds(start, tk),
                                                                  hh * HEAD_DIM:(hh + 1) * HEAD_DIM]
                kaug_ref[hh, pl.ds(start, tk), HEAD_DIM:] = aug.astype(BF16)
                vblk = v_ref[0, pl.ds(start, tk), hh * HEAD_DIM:(hh + 1) * HEAD_DIM]
                vt_ref[hh, :, pl.ds(start, tk)] = vblk.astype(F32).T.astype(BF16)
            return 0
        lax.fori_loop(0, seq // tk, build, 0)

    fblk = fc_ref[0, pl.ds(pl.multiple_of(i * tq, tq), tq), :]
    lane = lax.broadcasted_iota(jnp.int32, (tq, LANES), 1)
    qts = []
    for hh in range(hp):
        hi, mid, lo = _split3(head_col(fblk, hg * hp + hh))
        aq = jnp.where(lane == 0, hi,
                       jnp.where(lane == 1, mid,
                                 jnp.where(lane == 2, lo, jnp.where(lane < 6, 1.0, 0.0))))
        qh = q_ref[0, :, hh * HEAD_DIM:(hh + 1) * HEAD_DIM].astype(F32)
        qts.append(jnp.concatenate([qh.T, aq.T], axis=0).astype(BF16))

    def scores(j, slot):
        start = pl.multiple_of(j * tk, tk)
        for hh in range(hp):
            st = _dot(kaug_ref[hh, pl.ds(start, tk), :], qts[hh])
            st_ref[slot, hh] = st
            mx_ref[slot, hh] = jnp.max(st, axis=0, keepdims=True)

    def softmax_pv(j, slot, carry, diag):
        start = pl.multiple_of(j * tk, tk)
        out = []
        for hh in range(hp):
            m, l, acc = carry[hh]
            st = st_ref[slot, hh]
            if diag is None:
                mx = mx_ref[slot, hh]
            else:
                key = lax.broadcasted_iota(jnp.int32, (tk, tq), 0) + diag * tk
                qry = lax.broadcasted_iota(jnp.int32, (tk, tq), 1)
                st = jnp.where(key <= qry, st, -jnp.inf)
                mx = jnp.max(st, axis=0, keepdims=True)
            m_new = jnp.maximum(m, mx)
            alpha = jnp.exp2(m - m_new)
            pt = jnp.exp2(st - m_new)
            l = alpha * l + jnp.sum(pt, axis=0, keepdims=True)
            acc = alpha * acc + _dot(vt_ref[hh, :, pl.ds(start, tk)], pt.astype(BF16))
            out.append((m_new, l, acc))
        return tuple(out)

    def pair(jj, carry):
        a = 2 * jj
        scores(a + 1, 1)
        carry = softmax_pv(a, 0, carry, None)
        scores(a + 2, 0)
        return softmax_pv(a + 1, 1, carry, None)

    init = tuple((jnp.full((1, tq), -jnp.inf, F32), jnp.zeros((1, tq), F32),
                  jnp.zeros((HEAD_DIM, tq), F32)) for _ in range(hp))
    scores(0, 0)
    carry = lax.fori_loop(0, i, pair, init)
    scores(2 * i + 1, 1)
    carry = softmax_pv(2 * i, 0, carry, 0)
    carry = softmax_pv(2 * i + 1, 1, carry, 1)
    for hh in range(hp):
        _, l, acc = carry[hh]
        o_ref[0, :, hh * HEAD_DIM:(hh + 1) * HEAD_DIM] = (acc / l).T.astype(BF16)


def _attn(q, k, v, fc, tq=1024, hp=2):
    b, s, d = q.shape
    w = hp * HEAD_DIM
    tk = tq // 2
    return pl.pallas_call(
        functools.partial(_attn_kernel, tq=tq, tk=tk, hp=hp),
        grid=(b, N_HEADS // hp, s // tq),
        in_specs=[pl.BlockSpec((1, tq, w), lambda bb, h, i: (bb, i, h)),
                  pl.BlockSpec((1, s, w), lambda bb, h, i: (bb, 0, h)),
                  pl.BlockSpec((1, s, w), lambda bb, h, i: (bb, 0, h)),
                  pl.BlockSpec((1, s, LANES), lambda bb, h, i: (bb, 0, 0))],
        out_specs=pl.BlockSpec((1, tq, w), lambda bb, h, i: (bb, i, h)),
        out_shape=jax.ShapeDtypeStruct((b, s, d), BF16),
        scratch_shapes=[pltpu.VMEM((hp, s, 2 * HEAD_DIM), BF16),
                        pltpu.VMEM((hp, HEAD_DIM, s), BF16),
                        pltpu.VMEM((2, hp, tk, tq), F32),
                        pltpu.VMEM((2, hp, 1, tq), F32)],
        compiler_params=_params(("parallel", "parallel", "arbitrary")),
        name="attn",
    )(q, k, v, fc)


def _oproj_kernel(x_ref, a_ref, ada_ref, w_ref, o_ref):
    gate = ada_ref[0, 5:6, :]
    o_ref[...] = x_ref[...] + gate * _dot(a_ref[...], w_ref[...])


def _oproj(x, attn, ada, w_o, j, seq, tm=512):
    t, d = x.shape
    per_batch = seq // tm
    return pl.pallas_call(
        _oproj_kernel,
        grid=(t // tm,),
        in_specs=[pl.BlockSpec((tm, d), lambda i: (i, 0)),
                  pl.BlockSpec((tm, d), lambda i: (i, 0)),
                  pl.BlockSpec((1, 9, d), lambda i: (i // per_batch, 0, 0)),
                  pl.BlockSpec((None, d, d), lambda i: (j, 0, 0))],
        out_specs=pl.BlockSpec((tm, d), lambda i: (i, 0)),
        out_shape=jax.ShapeDtypeStruct((t, d), F32),
        compiler_params=_params(("parallel",)),
        name="oproj",
    )(x, attn, ada, w_o)


def kernel(x, c, norm_g, w_ada, b_ada, w_ffn_in, w_ffn_out, w_conv_in, conv_w, conv_b, w_conv_out,
           kv_norm_g, w_ada_kv, b_ada_kv, w_kvf, b_fgate, w_q, w_o, final_g):
    b, s, d = x.shape
    depth = norm_g.shape[0]
    n_a = w_conv_in.shape[0]
    t = b * s
    assert w_q.shape[0] == 1 and depth == n_a + 1

    w_ffn_in_b = w_ffn_in.astype(BF16)
    w_ffn_out_b = w_ffn_out.astype(BF16)
    w_conv_in_b = w_conv_in.astype(BF16)
    w_conv_out_b = w_conv_out.astype(BF16)
    w_kv_b = w_kvf[:, :2 * d].astype(BF16)
    w_f_b = jnp.pad(w_kvf[:, 2 * d:], ((0, 0), (0, LANES - N_HEADS))).astype(BF16)
    b_f = jnp.pad(b_fgate, (0, LANES - N_HEADS)).reshape(1, LANES)
    w_q_b = w_q.astype(BF16)
    w_o_b = w_o.astype(BF16)
    conv_b3 = conv_b.reshape(n_a, 1, d)

    c_pad = jnp.pad(c, ((0, SUBLANES - b), (0, 0)))
    xf = x.reshape(t, d)
    for l in range(depth):
        ada = _ada(c_pad, w_ada, b_ada, l)[:b].reshape(b, 9, d)
        xf = _ffn(xf, ada, norm_g[l, 0], w_ffn_in_b, w_ffn_out_b, l, 0, s)
        if l < n_a:
            xf = _conv(xf, ada, norm_g[l, 1], w_conv_in_b, conv_w, conv_b3, w_conv_out_b, l, s)
        else:
            j = l - n_a
            ada_kv = _ada(c_pad, w_ada_kv[None], b_ada_kv[None], 0)[:b].reshape(b, 2, d)
            q, k, v, fc = _qkv(xf, ada, ada_kv, norm_g[l, 1], kv_norm_g, w_q_b, w_kv_b, w_f_b, b_f, j, s)
            attn = _attn(q.reshape(b, s, d), k.reshape(b, s, d), v.reshape(b, s, d),
                         fc.reshape(b, s, LANES))
            xf = _oproj(xf, attn.reshape(t, d), ada, w_o_b, j, s)
        last = l == depth - 1
        xf = _ffn(xf, ada, norm_g[l, 2], w_ffn_in_b, w_ffn_out_b, l, 1, s,
                  final_g=final_g if last else None)
    return xf.reshape(b, s, d)
```

```python
import functools
import math

import jax
import jax.numpy as jnp
from jax import lax
from jax.experimental import pallas as pl
from jax.experimental.pallas import tpu as pltpu

EPS = 1e-6
N_HEADS = 16
HEAD_DIM = 128
CONV_WIDTH = 3
LOG2E = math.log2(math.e)
LANES = 128
SUBLANES = 8
MXU_DIM = 256
VMEM_LIMIT_BYTES = 56 * 1024 * 1024

F32 = jnp.float32
BF16 = jnp.bfloat16


def _params(semantics):
    return pltpu.CompilerParams(dimension_semantics=semantics,
                                vmem_limit_bytes=VMEM_LIMIT_BYTES)


def _dot(a, b):
    return jnp.dot(a, b, preferred_element_type=F32)


def _rmsnorm(x, g):
    return x * lax.rsqrt(jnp.mean(x * x, axis=-1, keepdims=True) + EPS) * g


NORM_ROWS = 16
NORM_UNROLL = 8

def _modulated_norm_to(h_ref, x_ref, g_ref, shift, scale):
    g = g_ref[...]
    scale1 = 1.0 + scale

    def body(r, _):
        rows = pl.ds(pl.multiple_of(r * NORM_ROWS, NORM_ROWS), NORM_ROWS)
        h_ref[rows, :] = (_rmsnorm(x_ref[rows, :], g) * scale1 + shift).astype(BF16)
        return 0
    lax.fori_loop(0, x_ref.shape[0] // NORM_ROWS, body, 0, unroll=NORM_UNROLL)


def _residual_to(o_ref, x_ref, coef, final_g):
    def body(r, _):
        rows = pl.ds(pl.multiple_of(r * NORM_ROWS, NORM_ROWS), NORM_ROWS)
        y = x_ref[rows, :] + coef * o_ref[rows, :]
        o_ref[rows, :] = y if final_g is None else _rmsnorm(y, final_g)
        return 0
    lax.fori_loop(0, x_ref.shape[0] // NORM_ROWS, body, 0, unroll=NORM_UNROLL)


def _ada_rows(ada_ref, sub):
    return tuple(ada_ref[0, 3 * sub + k:3 * sub + k + 1, :] for k in range(3))


def _ada_kernel(c_ref, w_ref, b_ref, o_ref):
    c = c_ref[...]
    cond = c * jax.nn.sigmoid(c)
    o_ref[...] = _dot(cond.astype(BF16), w_ref[...].astype(BF16)) + b_ref[...]


def _ada(c_pad, w, b, l, tn=1024):
    nl, d, n = w.shape
    return pl.pallas_call(
        _ada_kernel,
        grid=(n // tn,),
        in_specs=[pl.BlockSpec((SUBLANES, d), lambda j: (0, 0)),
                  pl.BlockSpec((None, d, tn), lambda j: (l, 0, j)),
                  pl.BlockSpec((None, 1, tn), lambda j: (l, 0, j))],
        out_specs=pl.BlockSpec((SUBLANES, tn), lambda j: (0, j)),
        out_shape=jax.ShapeDtypeStruct((SUBLANES, n), F32),
        compiler_params=_params(("parallel",)),
        name="ada",
    )(c_pad, w, b.reshape(nl, 1, n))


def _ffn_kernel(x_ref, ada_ref, g_ref, wg_ref, wu_ref, wo_ref, *rest, sub, final):
    if final:
        fg_ref, o_ref, h_ref = rest
    else:
        o_ref, h_ref = rest
    f = pl.program_id(1)

    shift, scale, gate = _ada_rows(ada_ref, sub)

    @pl.when(f == 0)
    def _():
        _modulated_norm_to(h_ref, x_ref, g_ref, shift, scale)
        o_ref[...] = jnp.zeros_like(o_ref)

    h = h_ref[...]
    a = _dot(h, wg_ref[...])
    b = _dot(h, wu_ref[...])
    act = (a * jax.nn.sigmoid(a) * b).astype(BF16)
    o_ref[...] += _dot(act, wo_ref[...])

    @pl.when(f == pl.num_programs(1) - 1)
    def _():
        _residual_to(o_ref, x_ref, 0.5 * gate, fg_ref[...] if final else None)


def _ffn(x, ada, g, w_in, w_out, l, idx, seq, final_g=None, tm=1024, tf=512):
    t, d = x.shape
    ff = w_out.shape[2]
    nf = ff // tf
    per_batch = seq // tm
    sub = 2 * idx
    final = final_g is not None
    in_specs = [pl.BlockSpec((tm, d), lambda i, f: (i, 0)),
                pl.BlockSpec((1, 9, d), lambda i, f: (i // per_batch, 0, 0)),
                pl.BlockSpec((1, d), lambda i, f: (0, 0)),
                pl.BlockSpec((None, None, d, tf), lambda i, f: (l, idx, 0, f)),
                pl.BlockSpec((None, None, d, tf), lambda i, f: (l, idx, 0, nf + f)),
                pl.BlockSpec((None, None, tf, d), lambda i, f: (l, idx, f, 0))]
    args = [x, ada, g.reshape(1, d), w_in, w_in, w_out]
    if final:
        in_specs.append(pl.BlockSpec((1, d), lambda i, f: (0, 0)))
        args.append(final_g.reshape(1, d))
    return pl.pallas_call(
        functools.partial(_ffn_kernel, sub=sub, final=final),
        grid=(t // tm, nf),
        in_specs=in_specs,
        out_specs=pl.BlockSpec((tm, d), lambda i, f: (i, 0)),
        out_shape=jax.ShapeDtypeStruct((t, d), F32),
        scratch_shapes=[pltpu.VMEM((tm, d), BF16)],
        compiler_params=_params(("parallel", "arbitrary")),
        name="ffn_final" if final else "ffn",
    )(*args)


def _conv_kernel(x_ref, ada_ref, g_ref, wb_ref, wc_ref, wx_ref, cw_ref, cb_ref, wo_ref,
                 o_ref, h_ref, tail_ref, *, per_batch):
    i = pl.program_id(0)
    c = pl.program_id(1)
    tm = x_ref.shape[0]

    shift, scale, gate = _ada_rows(ada_ref, 1)

    @pl.when(c == 0)
    def _():
        _modulated_norm_to(h_ref, x_ref, g_ref, shift, scale)
        o_ref[...] = jnp.zeros_like(o_ref)

    @pl.when(i % per_batch == 0)
    def _():
        tail_ref[c] = jnp.zeros(tail_ref.shape[1:], F32)

    h = h_ref[...]
    bg = _dot(h, wb_ref[...])
    cg = _dot(h, wc_ref[...])
    xv = _dot(h, wx_ref[...])
    u = cg * xv
    tail = tail_ref[c]
    p1 = tail[SUBLANES - 1:SUBLANES, :]
    p2 = tail[SUBLANES - 2:SUBLANES - 1, :]
    row = lax.broadcasted_iota(jnp.int32, u.shape, 0)
    u1 = jnp.where(row == 0, p1, pltpu.roll(u, 1, 0))
    u2 = jnp.where(row == 0, p2, jnp.where(row == 1, p1, pltpu.roll(u, 2, 0)))
    conv = cw_ref[0:1, :] * u2 + cw_ref[1:2, :] * u1 + cw_ref[2:3, :] * u + cb_ref[...]
    tail_ref[c] = u[tm - SUBLANES:, :]
    o_ref[...] += _dot((bg * conv).astype(BF16), wo_ref[...])

    @pl.when(c == pl.num_programs(1) - 1)
    def _():
        _residual_to(o_ref, x_ref, gate, None)


def _conv(x, ada, g, w_in, conv_w, conv_b, w_out, l, seq, tm=1024, tc=256):
    t, d = x.shape
    nc = d // tc
    per_batch = seq // tm
    return pl.pallas_call(
        functools.partial(_conv_kernel, per_batch=per_batch),
        grid=(t // tm, nc),
        in_specs=[pl.BlockSpec((tm, d), lambda i, c: (i, 0)),
                  pl.BlockSpec((1, 9, d), lambda i, c: (i // per_batch, 0, 0)),
                  pl.BlockSpec((1, d), lambda i, c: (0, 0)),
                  pl.BlockSpec((None, d, tc), lambda i, c: (l, 0, c)),
                  pl.BlockSpec((None, d, tc), lambda i, c: (l, 0, nc + c)),
                  pl.BlockSpec((None, d, tc), lambda i, c: (l, 0, 2 * nc + c)),
                  pl.BlockSpec((None, CONV_WIDTH, tc), lambda i, c: (l, 0, c)),
                  pl.BlockSpec((None, 1, tc), lambda i, c: (l, 0, c)),
                  pl.BlockSpec((None, tc, d), lambda i, c: (l, c, 0))],
        out_specs=pl.BlockSpec((tm, d), lambda i, c: (i, 0)),
        out_shape=jax.ShapeDtypeStruct((t, d), F32),
        scratch_shapes=[pltpu.VMEM((tm, d), BF16),
                        pltpu.VMEM((nc, SUBLANES, tc), F32)],
        compiler_params=_params(("arbitrary", "arbitrary")),
        name="conv",
    )(x, ada, g.reshape(1, d), w_in, w_in, w_in, conv_w, conv_b, w_out)


def _qkv_kernel(x_ref, ada_ref, adakv_ref, g_ref, gkv_ref, wq_ref, wk_ref, wv_ref, wf_ref, bf_ref,
                q_ref, k_ref, v_ref, fc_ref, hq_ref, hkv_ref, carry_ref, *, per_batch, q_scale):
    i = pl.program_id(0)
    c = pl.program_id(1)
    tm = x_ref.shape[0]

    @pl.when(c == 0)
    def _():
        shift, scale, _ = _ada_rows(ada_ref, 1)
        _modulated_norm_to(hq_ref, x_ref, g_ref, shift, scale)
        _modulated_norm_to(hkv_ref, x_ref, gkv_ref, adakv_ref[0, 0:1, :], adakv_ref[0, 1:2, :])
        hkv = hkv_ref[...]

        @pl.when(i % per_batch == 0)
        def _():
            carry_ref[...] = jnp.zeros_like(carry_ref)

        zf = _dot(hkv, wf_ref[...]) + bf_ref[...]
        ls = jnp.minimum(zf, 0.0) - jnp.log1p(jnp.exp(-jnp.abs(zf)))
        hi = ls.astype(BF16)
        r1 = ls - hi.astype(F32)
        mid = r1.astype(BF16)
        lo = (r1 - mid.astype(F32)).astype(BF16)
        rr = lax.broadcasted_iota(jnp.int32, (tm, tm), 0)
        cc = lax.broadcasted_iota(jnp.int32, (tm, tm), 1)
        tri = (rr >= cc).astype(BF16)
        parts = _dot(tri, jnp.concatenate([hi, mid, lo], axis=1))
        cum = (parts[:, :LANES] + parts[:, LANES:2 * LANES]) + parts[:, 2 * LANES:] + carry_ref[0:1, :]
        carry_ref[...] = jnp.broadcast_to(cum[tm - 1:tm, :], carry_ref.shape)
        fc_ref[...] = cum

    hkv = hkv_ref[...]
    q_ref[...] = (_dot(hq_ref[...], wq_ref[...]) * q_scale).astype(BF16)
    k_ref[...] = _dot(hkv, wk_ref[...]).astype(BF16)
    v_ref[...] = _dot(hkv, wv_ref[...]).astype(BF16)


def _qkv(x, ada, ada_kv, g, gkv, w_q, w_kv, w_f, b_f, j, seq, tm=1024, tn=512):
    t, d = x.shape
    nn = d // tn
    per_batch = seq // tm
    q_scale = LOG2E / math.sqrt(HEAD_DIM)
    row_blk = pl.BlockSpec((tm, tn), lambda i, c: (i, c))
    return pl.pallas_call(
        functools.partial(_qkv_kernel, per_batch=per_batch, q_scale=q_scale),
        grid=(t // tm, nn),
        in_specs=[pl.BlockSpec((tm, d), lambda i, c: (i, 0)),
                  pl.BlockSpec((1, 9, d), lambda i, c: (i // per_batch, 0, 0)),
                  pl.BlockSpec((1, 2, d), lambda i, c: (i // per_batch, 0, 0)),
                  pl.BlockSpec((1, d), lambda i, c: (0, 0)),
                  pl.BlockSpec((1, d), lambda i, c: (0, 0)),
                  pl.BlockSpec((None, d, tn), lambda i, c: (j, 0, c)),
                  pl.BlockSpec((d, tn), lambda i, c: (0, c)),
                  pl.BlockSpec((d, tn), lambda i, c: (0, nn + c)),
                  pl.BlockSpec((d, LANES), lambda i, c: (0, 0)),
                  pl.BlockSpec((1, LANES), lambda i, c: (0, 0))],
        out_specs=[row_blk, row_blk, row_blk,
                   pl.BlockSpec((tm, LANES), lambda i, c: (i, 0))],
        out_shape=[jax.ShapeDtypeStruct((t, d), BF16),
                   jax.ShapeDtypeStruct((t, d), BF16),
                   jax.ShapeDtypeStruct((t, d), BF16),
                   jax.ShapeDtypeStruct((t, LANES), F32)],
        scratch_shapes=[pltpu.VMEM((tm, d), BF16),
                        pltpu.VMEM((tm, d), BF16),
                        pltpu.VMEM((SUBLANES, LANES), F32)],
        compiler_params=_params(("arbitrary", "arbitrary")),
        name="qkv",
    )(x, ada, ada_kv, g.reshape(1, d), gkv.reshape(1, d), w_q, w_kv, w_kv, w_f, b_f)


def _split3(f):
    hi = f.astype(BF16)
    r1 = f - hi.astype(F32)
    mid = r1.astype(BF16)
    lo = (r1 - mid.astype(F32)).astype(BF16)
    return hi.astype(F32), mid.astype(F32), lo.astype(F32)


def _attn_kernel(q_ref, k_ref, v_ref, fc_ref, o_ref, kaug_ref, vt_ref, st_ref, *, tq, tk, hp):
    hg = pl.program_id(1)
    i = pl.program_id(2)
    seq = k_ref.shape[1]

    def head_col(blk, head):
        lane = lax.broadcasted_iota(jnp.int32, blk.shape, 1)
        return jnp.sum(jnp.where(lane == head, blk, 0.0), axis=1, keepdims=True) * LOG2E

    @pl.when(i == 0)
    def _():
        lane = lax.broadcasted_iota(jnp.int32, (tk, LANES), 1)

        def build(r, _):
            start = pl.multiple_of(r * tk, tk)
            blk = fc_ref[0, pl.ds(start, tk), :]
            for hh in range(hp):
                hi, mid, lo = _split3(head_col(blk, hg * hp + hh))
                aug = jnp.where(lane < 3, 1.0,
                                jnp.where(lane == 3, -hi,
                                          jnp.where(lane == 4, -mid, jnp.where(lane == 5, -lo, 0.0))))
                kaug_ref[hh, pl.ds(start, tk), :HEAD_DIM] = k_ref[0, pl.ds(start, tk),
                                                                  hh * HEAD_DIM:(hh + 1) * HEAD_DIM]
                kaug_ref[hh, pl.ds(start, tk), HEAD_DIM:] = aug.astype(BF16)
                vblk = v_ref[0, pl.ds(start, tk), hh * HEAD_DIM:(hh + 1) * HEAD_DIM]
                vt_ref[hh, :, pl.ds(start, tk)] = vblk.astype(F32).T.astype(BF16)
            return 0
        lax.fori_loop(0, seq // tk, build, 0)

    fblk = fc_ref[0, pl.ds(pl.multiple_of(i * tq, tq), tq), :]
    lane = lax.broadcasted_iota(jnp.int32, (tq, LANES), 1)
    qts = []
    for hh in range(hp):
        hi, mid, lo = _split3(head_col(fblk, hg * hp + hh))
        aq = jnp.where(lane == 0, hi,
                       jnp.where(lane == 1, mid,
                                 jnp.where(lane == 2, lo, jnp.where(lane < 6, 1.0, 0.0))))
        qh = q_ref[0, :, hh * HEAD_DIM:(hh + 1) * HEAD_DIM].astype(F32)
        qts.append(jnp.concatenate([qh.T, aq.T], axis=0).astype(BF16))

    def scores(j, slot):
        start = pl.multiple_of(j * tk, tk)
        for hh in range(hp):
            st_ref[slot, hh] = _dot(kaug_ref[hh, pl.ds(start, tk), :], qts[hh])

    def softmax_pv(j, slot, carry, diag):
        start = pl.multiple_of(j * tk, tk)
        out = []
        for hh in range(hp):
            m, l, acc = carry[hh]
            st = st_ref[slot, hh]
            if diag is not None:
                key = lax.broadcasted_iota(jnp.int32, (tk, tq), 0) + diag * tk
                qry = lax.broadcasted_iota(jnp.int32, (tk, tq), 1)
                st = jnp.where(key <= qry, st, -jnp.inf)
            m_new = jnp.maximum(m, jnp.max(st, axis=0, keepdims=True))
            alpha = jnp.exp2(m - m_new)
            pt = jnp.exp2(st - m_new)
            l = alpha * l + jnp.sum(pt, axis=0, keepdims=True)
            acc = alpha * acc + _dot(vt_ref[hh, :, pl.ds(start, tk)], pt.astype(BF16))
            out.append((m_new, l, acc))
        return tuple(out)

    def pair(jj, carry):
        a = 2 * jj
        scores(a + 1, 1)
        carry = softmax_pv(a, 0, carry, None)
        scores(a + 2, 0)
        return softmax_pv(a + 1, 1, carry, None)

    init = tuple((jnp.full((1, tq), -jnp.inf, F32), jnp.zeros((1, tq), F32),
                  jnp.zeros((HEAD_DIM, tq), F32)) for _ in range(hp))
    scores(0, 0)
    carry = lax.fori_loop(0, i, pair, init)
    scores(2 * i + 1, 1)
    carry = softmax_pv(2 * i, 0, carry, 0)
    carry = softmax_pv(2 * i + 1, 1, carry, 1)
    for hh in range(hp):
        _, l, acc = carry[hh]
        o_ref[0, :, hh * HEAD_DIM:(hh + 1) * HEAD_DIM] = (acc / l).T.astype(BF16)


def _attn(q, k, v, fc, tq=1024, hp=2):
    b, s, d = q.shape
    w = hp * HEAD_DIM
    tk = tq // 2
    return pl.pallas_call(
        functools.partial(_attn_kernel, tq=tq, tk=tk, hp=hp),
        grid=(b, N_HEADS // hp, s // tq),
        in_specs=[pl.BlockSpec((1, tq, w), lambda bb, h, i: (bb, i, h)),
                  pl.BlockSpec((1, s, w), lambda bb, h, i: (bb, 0, h)),
                  pl.BlockSpec((1, s, w), lambda bb, h, i: (bb, 0, h)),
                  pl.BlockSpec((1, s, LANES), lambda bb, h, i: (bb, 0, 0))],
        out_specs=pl.BlockSpec((1, tq, w), lambda bb, h, i: (bb, i, h)),
        out_shape=jax.ShapeDtypeStruct((b, s, d), BF16),
        scratch_shapes=[pltpu.VMEM((hp, s, 2 * HEAD_DIM), BF16),
                        pltpu.VMEM((hp, HEAD_DIM, s), BF16),
                        pltpu.VMEM((2, hp, tk, tq), F32)],
        compiler_params=_params(("parallel", "parallel", "arbitrary")),
        name="attn",
    )(q, k, v, fc)


def _oproj_kernel(x_ref, a_ref, ada_ref, w_ref, o_ref):
    gate = ada_ref[0, 5:6, :]
    o_ref[...] = x_ref[...] + gate * _dot(a_ref[...], w_ref[...])


def _oproj(x, attn, ada, w_o, j, seq, tm=1024):
    t, d = x.shape
    per_batch = seq // tm
    return pl.pallas_call(
        _oproj_kernel,
        grid=(t // tm,),
        in_specs=[pl.BlockSpec((tm, d), lambda i: (i, 0)),
                  pl.BlockSpec((tm, d), lambda i: (i, 0)),
                  pl.BlockSpec((1, 9, d), lambda i: (i // per_batch, 0, 0)),
                  pl.BlockSpec((None, d, d), lambda i: (j, 0, 0), pipeline_mode=pl.Buffered(1))],
        out_specs=pl.BlockSpec((tm, d), lambda i: (i, 0)),
        out_shape=jax.ShapeDtypeStruct((t, d), F32),
        compiler_params=_params(("parallel",)),
        name="oproj",
    )(x, attn, ada, w_o)


def kernel(x, c, norm_g, w_ada, b_ada, w_ffn_in, w_ffn_out, w_conv_in, conv_w, conv_b, w_conv_out,
           kv_norm_g, w_ada_kv, b_ada_kv, w_kvf, b_fgate, w_q, w_o, final_g):
    b, s, d = x.shape
    depth = norm_g.shape[0]
    n_a = w_conv_in.shape[0]
    t = b * s
    assert w_q.shape[0] == 1 and depth == n_a + 1

    w_ffn_in_b = w_ffn_in.astype(BF16)
    w_ffn_out_b = w_ffn_out.astype(BF16)
    w_conv_in_b = w_conv_in.astype(BF16)
    w_conv_out_b = w_conv_out.astype(BF16)
    w_kv_b = w_kvf[:, :2 * d].astype(BF16)
    w_f_b = jnp.pad(w_kvf[:, 2 * d:], ((0, 0), (0, LANES - N_HEADS))).astype(BF16)
    b_f = jnp.pad(b_fgate, (0, LANES - N_HEADS)).reshape(1, LANES)
    w_q_b = w_q.astype(BF16)
    w_o_b = w_o.astype(BF16)
    conv_b3 = conv_b.reshape(n_a, 1, d)

    c_pad = jnp.pad(c, ((0, SUBLANES - b), (0, 0)))
    xf = x.reshape(t, d)
    for l in range(depth):
        ada = _ada(c_pad, w_ada, b_ada, l)[:b].reshape(b, 9, d)
        xf = _ffn(xf, ada, norm_g[l, 0], w_ffn_in_b, w_ffn_out_b, l, 0, s)
        if l < n_a:
            xf = _conv(xf, ada, norm_g[l, 1], w_conv_in_b, conv_w, conv_b3, w_conv_out_b, l, s)
        else:
            j = l - n_a
            ada_kv = _ada(c_pad, w_ada_kv[None], b_ada_kv[None], 0)[:b].reshape(b, 2, d)
            q, k, v, fc = _qkv(xf, ada, ada_kv, norm_g[l, 1], kv_norm_g, w_q_b, w_kv_b, w_f_b, b_f, j, s)
            attn = _attn(q.reshape(b, s, d), k.reshape(b, s, d), v.reshape(b, s, d),
                         fc.reshape(b, s, LANES))
            xf = _oproj(xf, attn.reshape(t, d), ada, w_o_b, j, s)
        last = l == depth - 1
        xf = _ffn(xf, ada, norm_g[l, 2], w_ffn_in_b, w_ffn_out_b, l, 1, s,
                  final_g=final_g if last else None)
    return xf.reshape(b, s, d)
```

```python
import functools
import math

import jax
import jax.numpy as jnp
from jax import lax
from jax.experimental import pallas as pl
from jax.experimental.pallas import tpu as pltpu

EPS = 1e-6
N_HEADS = 16
HEAD_DIM = 128
CONV_WIDTH = 3
LOG2E = math.log2(math.e)
LANES = 128
SUBLANES = 8
MXU_DIM = 256
VMEM_LIMIT_BYTES = 56 * 1024 * 1024

F32 = jnp.float32
BF16 = jnp.bfloat16


def _params(semantics):
    return pltpu.CompilerParams(dimension_semantics=semantics,
                                vmem_limit_bytes=VMEM_LIMIT_BYTES)


def _dot(a, b):
    return jnp.dot(a, b, preferred_element_type=F32)


def _rmsnorm(x, g):
    return x * lax.rsqrt(jnp.mean(x * x, axis=-1, keepdims=True) + EPS) * g


NORM_ROWS = 16
NORM_UNROLL = 8

def _modulated_norm_to(h_ref, x_ref, g_ref, shift, scale):
    g = g_ref[...]
    scale1 = 1.0 + scale

    def body(r, _):
        rows = pl.ds(pl.multiple_of(r * NORM_ROWS, NORM_ROWS), NORM_ROWS)
        h_ref[rows, :] = (_rmsnorm(x_ref[rows, :], g) * scale1 + shift).astype(BF16)
        return 0
    lax.fori_loop(0, x_ref.shape[0] // NORM_ROWS, body, 0, unroll=NORM_UNROLL)


def _residual_to(o_ref, x_ref, coef, final_g):
    y = x_ref[...] + coef * o_ref[...]
    o_ref[...] = y if final_g is None else _rmsnorm(y, final_g)


def _ada_rows(ada_ref, sub):
    return tuple(ada_ref[0, 3 * sub + k:3 * sub + k + 1, :] for k in range(3))


def _ada_kernel(c_ref, w_ref, b_ref, o_ref):
    c = c_ref[...]
    cond = c * jax.nn.sigmoid(c)
    o_ref[...] = _dot(cond.astype(BF16), w_ref[...].astype(BF16)) + b_ref[...]


def _ada(c_pad, w, b, l, tn=1024):
    nl, d, n = w.shape
    return pl.pallas_call(
        _ada_kernel,
        grid=(n // tn,),
        in_specs=[pl.BlockSpec((SUBLANES, d), lambda j: (0, 0)),
                  pl.BlockSpec((None, d, tn), lambda j: (l, 0, j)),
                  pl.BlockSpec((None, 1, tn), lambda j: (l, 0, j))],
        out_specs=pl.BlockSpec((SUBLANES, tn), lambda j: (0, j)),
        out_shape=jax.ShapeDtypeStruct((SUBLANES, n), F32),
        compiler_params=_params(("parallel",)),
        name="ada",
    )(c_pad, w, b.reshape(nl, 1, n))


def _ffn_kernel(x_ref, ada_ref, g_ref, wg_ref, wu_ref, wo_ref, *rest, sub, final):
    if final:
        fg_ref, o_ref, h_ref = rest
    else:
        o_ref, h_ref = rest
    f = pl.program_id(1)

    shift, scale, gate = _ada_rows(ada_ref, sub)

    @pl.when(f == 0)
    def _():
        _modulated_norm_to(h_ref, x_ref, g_ref, shift, scale)
        o_ref[...] = jnp.zeros_like(o_ref)

    h = h_ref[...]
    a = _dot(h, wg_ref[...])
    b = _dot(h, wu_ref[...])
    act = (a * jax.nn.sigmoid(a) * b).astype(BF16)
    o_ref[...] += _dot(act, wo_ref[...])

    @pl.when(f == pl.num_programs(1) - 1)
    def _():
        _residual_to(o_ref, x_ref, 0.5 * gate, fg_ref[...] if final else None)


def _ffn(x, ada, g, w_in, w_out, l, idx, seq, final_g=None, tf=512):
    t, d = x.shape
    ff = w_out.shape[2]
    nf = ff // tf
    sub = 2 * idx
    final = final_g is not None
    tm = 512 if final else 1024
    per_batch = seq // tm
    in_specs = [pl.BlockSpec((tm, d), lambda i, f: (i, 0)),
                pl.BlockSpec((1, 9, d), lambda i, f: (i // per_batch, 0, 0)),
                pl.BlockSpec((1, d), lambda i, f: (0, 0)),
                pl.BlockSpec((None, None, d, tf), lambda i, f: (l, idx, 0, f)),
                pl.BlockSpec((None, None, d, tf), lambda i, f: (l, idx, 0, nf + f)),
                pl.BlockSpec((None, None, tf, d), lambda i, f: (l, idx, f, 0))]
    args = [x, ada, g.reshape(1, d), w_in, w_in, w_out]
    if final:
        in_specs.append(pl.BlockSpec((1, d), lambda i, f: (0, 0)))
        args.append(final_g.reshape(1, d))
    return pl.pallas_call(
        functools.partial(_ffn_kernel, sub=sub, final=final),
        grid=(t // tm, nf),
        in_specs=in_specs,
        out_specs=pl.BlockSpec((tm, d), lambda i, f: (i, 0)),
        out_shape=jax.ShapeDtypeStruct((t, d), F32),
        scratch_shapes=[pltpu.VMEM((tm, d), BF16)],
        compiler_params=_params(("parallel", "arbitrary")),
        name="ffn_final" if final else "ffn",
    )(*args)


def _conv_kernel(x_ref, ada_ref, g_ref, wb_ref, wc_ref, wx_ref, cw_ref, cb_ref, wo_ref,
                 o_ref, h_ref, tail_ref, *, per_batch):
    i = pl.program_id(0)
    c = pl.program_id(1)
    tm = x_ref.shape[0]

    shift, scale, gate = _ada_rows(ada_ref, 1)

    @pl.when(c == 0)
    def _():
        _modulated_norm_to(h_ref, x_ref, g_ref, shift, scale)
        o_ref[...] = jnp.zeros_like(o_ref)

    @pl.when(i % per_batch == 0)
    def _():
        tail_ref[c] = jnp.zeros(tail_ref.shape[1:], F32)

    h = h_ref[...]
    bg = _dot(h, wb_ref[...])
    cg = _dot(h, wc_ref[...])
    xv = _dot(h, wx_ref[...])
    u = cg * xv
    tail = tail_ref[c]
    p1 = tail[SUBLANES - 1:SUBLANES, :]
    p2 = tail[SUBLANES - 2:SUBLANES - 1, :]
    row = lax.broadcasted_iota(jnp.int32, u.shape, 0)
    u1 = jnp.where(row == 0, p1, pltpu.roll(u, 1, 0))
    u2 = jnp.where(row == 0, p2, jnp.where(row == 1, p1, pltpu.roll(u, 2, 0)))
    conv = cw_ref[0:1, :] * u2 + cw_ref[1:2, :] * u1 + cw_ref[2:3, :] * u + cb_ref[...]
    tail_ref[c] = u[tm - SUBLANES:, :]
    o_ref[...] += _dot((bg * conv).astype(BF16), wo_ref[...])

    @pl.when(c == pl.num_programs(1) - 1)
    def _():
        _residual_to(o_ref, x_ref, gate, None)


def _conv(x, ada, g, w_in, conv_w, conv_b, w_out, l, seq, tm=512, tc=512):
    t, d = x.shape
    nc = d // tc
    per_batch = seq // tm
    return pl.pallas_call(
        functools.partial(_conv_kernel, per_batch=per_batch),
        grid=(t // tm, nc),
        in_specs=[pl.BlockSpec((tm, d), lambda i, c: (i, 0)),
                  pl.BlockSpec((1, 9, d), lambda i, c: (i // per_batch, 0, 0)),
                  pl.BlockSpec((1, d), lambda i, c: (0, 0)),
                  pl.BlockSpec((None, d, tc), lambda i, c: (l, 0, c)),
                  pl.BlockSpec((None, d, tc), lambda i, c: (l, 0, nc + c)),
                  pl.BlockSpec((None, d, tc), lambda i, c: (l, 0, 2 * nc + c)),
                  pl.BlockSpec((None, CONV_WIDTH, tc), lambda i, c: (l, 0, c)),
                  pl.BlockSpec((None, 1, tc), lambda i, c: (l, 0, c)),
                  pl.BlockSpec((None, tc, d), lambda i, c: (l, c, 0))],
        out_specs=pl.BlockSpec((tm, d), lambda i, c: (i, 0)),
        out_shape=jax.ShapeDtypeStruct((t, d), F32),
        scratch_shapes=[pltpu.VMEM((tm, d), BF16),
                        pltpu.VMEM((nc, SUBLANES, tc), F32)],
        compiler_params=_params(("arbitrary", "arbitrary")),
        name="conv",
    )(x, ada, g.reshape(1, d), w_in, w_in, w_in, conv_w, conv_b, w_out)


def _qkv_kernel(x_ref, ada_ref, adakv_ref, g_ref, gkv_ref, wq_ref, wk_ref, wv_ref, wf_ref, bf_ref,
                q_ref, k_ref, v_ref, fc_ref, hq_ref, hkv_ref, carry_ref, *, per_batch, q_scale):
    i = pl.program_id(0)
    c = pl.program_id(1)
    tm = x_ref.shape[0]

    @pl.when(c == 0)
    def _():
        shift, scale, _ = _ada_rows(ada_ref, 1)
        _modulated_norm_to(hq_ref, x_ref, g_ref, shift, scale)
        _modulated_norm_to(hkv_ref, x_ref, gkv_ref, adakv_ref[0, 0:1, :], adakv_ref[0, 1:2, :])
        hkv = hkv_ref[...]

        @pl.when(i % per_batch == 0)
        def _():
            carry_ref[...] = jnp.zeros_like(carry_ref)

        zf = _dot(hkv, wf_ref[...]) + bf_ref[...]
        ls = jnp.minimum(zf, 0.0) - jnp.log1p(jnp.exp(-jnp.abs(zf)))
        hi = ls.astype(BF16)
        r1 = ls - hi.astype(F32)
        mid = r1.astype(BF16)
        lo = (r1 - mid.astype(F32)).astype(BF16)
        rr = lax.broadcasted_iota(jnp.int32, (tm, tm), 0)
        cc = lax.broadcasted_iota(jnp.int32, (tm, tm), 1)
        tri = (rr >= cc).astype(BF16)
        parts = _dot(tri, jnp.concatenate([hi, mid, lo], axis=1))
        cum = (parts[:, :LANES] + parts[:, LANES:2 * LANES]) + parts[:, 2 * LANES:] + carry_ref[0:1, :]
        carry_ref[...] = jnp.broadcast_to(cum[tm - 1:tm, :], carry_ref.shape)
        fc_ref[...] = cum

    hkv = hkv_ref[...]
    q_ref[...] = (_dot(hq_ref[...], wq_ref[...]) * q_scale).astype(BF16)
    k_ref[...] = _dot(hkv, wk_ref[...]).astype(BF16)
    v_ref[...] = _dot(hkv, wv_ref[...]).astype(BF16)


def _qkv(x, ada, ada_kv, g, gkv, w_q, w_kv, w_f, b_f, j, seq, tm=1024, tn=512):
    t, d = x.shape
    nn = d // tn
    per_batch = seq // tm
    q_scale = LOG2E / math.sqrt(HEAD_DIM)
    row_blk = pl.BlockSpec((tm, tn), lambda i, c: (i, c))
    return pl.pallas_call(
        functools.partial(_qkv_kernel, per_batch=per_batch, q_scale=q_scale),
        grid=(t // tm, nn),
        in_specs=[pl.BlockSpec((tm, d), lambda i, c: (i, 0)),
                  pl.BlockSpec((1, 9, d), lambda i, c: (i // per_batch, 0, 0)),
                  pl.BlockSpec((1, 2, d), lambda i, c: (i // per_batch, 0, 0)),
                  pl.BlockSpec((1, d), lambda i, c: (0, 0)),
                  pl.BlockSpec((1, d), lambda i, c: (0, 0)),
                  pl.BlockSpec((None, d, tn), lambda i, c: (j, 0, c)),
                  pl.BlockSpec((d, tn), lambda i, c: (0, c)),
                  pl.BlockSpec((d, tn), lambda i, c: (0, nn + c)),
                  pl.BlockSpec((d, LANES), lambda i, c: (0, 0)),
                  pl.BlockSpec((1, LANES), lambda i, c: (0, 0))],
        out_specs=[row_blk, row_blk, row_blk,
                   pl.BlockSpec((tm, LANES), lambda i, c: (i, 0))],
        out_shape=[jax.ShapeDtypeStruct((t, d), BF16),
                   jax.ShapeDtypeStruct((t, d), BF16),
                   jax.ShapeDtypeStruct((t, d), BF16),
                   jax.ShapeDtypeStruct((t, LANES), F32)],
        scratch_shapes=[pltpu.VMEM((tm, d), BF16),
                        pltpu.VMEM((tm, d), BF16),
                        pltpu.VMEM((SUBLANES, LANES), F32)],
        compiler_params=_params(("arbitrary", "arbitrary")),
        name="qkv",
    )(x, ada, ada_kv, g.reshape(1, d), gkv.reshape(1, d), w_q, w_kv, w_kv, w_f, b_f)


def _split3(f):
    hi = f.astype(BF16)
    r1 = f - hi.astype(F32)
    mid = r1.astype(BF16)
    lo = (r1 - mid.astype(F32)).astype(BF16)
    return hi.astype(F32), mid.astype(F32), lo.astype(F32)


def _attn_kernel(q_ref, k_ref, v_ref, fc_ref, o_ref, kaug_ref, vt_ref, st_ref, *, tq, tk, hp):
    hg = pl.program_id(1)
    i = pl.program_id(2)
    seq = k_ref.shape[1]

    def head_col(blk, head):
        lane = lax.broadcasted_iota(jnp.int32, blk.shape, 1)
        return jnp.sum(jnp.where(lane == head, blk, 0.0), axis=1, keepdims=True) * LOG2E

    @pl.when(i == 0)
    def _():
        lane = lax.broadcasted_iota(jnp.int32, (tk, LANES), 1)

        def build(r, _):
            start = pl.multiple_of(r * tk, tk)
            blk = fc_ref[0, pl.ds(start, tk), :]
            for hh in range(hp):
                hi, mid, lo = _split3(head_col(blk, hg * hp + hh))
                aug = jnp.where(lane < 3, 1.0,
                                jnp.where(lane == 3, -hi,
                                          jnp.where(lane == 4, -mid, jnp.where(lane == 5, -lo, 0.0))))
                kaug_ref[hh, pl.ds(start, tk), :HEAD_DIM] = k_ref[0, pl.ds(start, tk),
                                                                  hh * HEAD_DIM:(hh + 1) * HEAD_DIM]
                kaug_ref[hh, pl.ds(start, tk), HEAD_DIM:] = aug.astype(BF16)
                vblk = v_ref[0, pl.ds(start, tk), hh * HEAD_DIM:(hh + 1) * HEAD_DIM]
                vt_ref[hh, :, pl.ds(start, tk)] = vblk.astype(F32).T.astype(BF16)
            return 0
        lax.fori_loop(0, seq // tk, build, 0)

    fblk = fc_ref[0, pl.ds(pl.multiple_of(i * tq, tq), tq), :]
    lane = lax.broadcasted_iota(jnp.int32, (tq, LANES), 1)
    qts = []
    for hh in range(hp):
        hi, mid, lo = _split3(head_col(fblk, hg * hp + hh))
        aq = jnp.where(lane == 0, hi,
                       jnp.where(lane == 1, mid,
                                 jnp.where(lane == 2, lo, jnp.where(lane < 6, 1.0, 0.0))))
        qh = q_ref[0, :, hh * HEAD_DIM:(hh + 1) * HEAD_DIM].astype(F32)
        qts.append(jnp.concatenate([qh.T, aq.T], axis=0).astype(BF16))

    def scores(j, slot):
        start = pl.multiple_of(j * tk, tk)
        for hh in range(hp):
            st_ref[slot, hh] = _dot(kaug_ref[hh, pl.ds(start, tk), :], qts[hh])

    def softmax_pv(j, slot, carry, diag):
        start = pl.multiple_of(j * tk, tk)
        out = []
        for hh in range(hp):
            m, l, acc = carry[hh]
            st = st_ref[slot, hh]
            if diag is not None:
                key = lax.broadcasted_iota(jnp.int32, (tk, tq), 0) + diag * tk
                qry = lax.broadcasted_iota(jnp.int32, (tk, tq), 1)
                st = jnp.where(key <= qry, st, -jnp.inf)
            m_new = jnp.maximum(m, jnp.max(st, axis=0, keepdims=True))
            alpha = jnp.exp2(m - m_new)
            pt = jnp.exp2(st - m_new)
            l = alpha * l + jnp.sum(pt, axis=0, keepdims=True)
            acc = alpha * acc + _dot(vt_ref[hh, :, pl.ds(start, tk)], pt.astype(BF16))
            out.append((m_new, l, acc))
        return tuple(out)

    def pair(jj, carry):
        a = 2 * jj
        scores(a + 1, 1)
        carry = softmax_pv(a, 0, carry, None)
        scores(a + 2, 0)
        return softmax_pv(a + 1, 1, carry, None)

    init = tuple((jnp.full((1, tq), -jnp.inf, F32), jnp.zeros((1, tq), F32),
                  jnp.zeros((HEAD_DIM, tq), F32)) for _ in range(hp))
    scores(0, 0)
    carry = lax.fori_loop(0, i, pair, init)
    scores(2 * i + 1, 1)
    carry = softmax_pv(2 * i, 0, carry, 0)
    carry = softmax_pv(2 * i + 1, 1, carry, 1)
    for hh in range(hp):
        _, l, acc = carry[hh]
        o_ref[0, :, hh * HEAD_DIM:(hh + 1) * HEAD_DIM] = (acc / l).T.astype(BF16)


def _attn(q, k, v, fc, tq=1024, hp=2):
    b, s, d = q.shape
    w = hp * HEAD_DIM
    tk = tq // 2
    return pl.pallas_call(
        functools.partial(_attn_kernel, tq=tq, tk=tk, hp=hp),
        grid=(b, N_HEADS // hp, s // tq),
        in_specs=[pl.BlockSpec((1, tq, w), lambda bb, h, i: (bb, i, h)),
                  pl.BlockSpec((1, s, w), lambda bb, h, i: (bb, 0, h)),
                  pl.BlockSpec((1, s, w), lambda bb, h, i: (bb, 0, h)),
                  pl.BlockSpec((1, s, LANES), lambda bb, h, i: (bb, 0, 0))],
        out_specs=pl.BlockSpec((1, tq, w), lambda bb, h, i: (bb, i, h)),
        out_shape=jax.ShapeDtypeStruct((b, s, d), BF16),
        scratch_shapes=[pltpu.VMEM((hp, s, 2 * HEAD_DIM), BF16),
                        pltpu.VMEM((hp, HEAD_DIM, s), BF16),
                        pltpu.VMEM((2, hp, tk, tq), F32)],
        compiler_params=_params(("parallel", "parallel", "arbitrary")),
        name="attn",
    )(q, k, v, fc)


def _oproj_kernel(x_ref, a_ref, ada_ref, w_ref, o_ref):
    gate = ada_ref[0, 5:6, :]
    o_ref[...] = x_ref[...] + gate * _dot(a_ref[...], w_ref[...])


def _oproj(x, attn, ada, w_o, j, seq, tm=1024):
    t, d = x.shape
    per_batch = seq // tm
    return pl.pallas_call(
        _oproj_kernel,
        grid=(t // tm,),
        in_specs=[pl.BlockSpec((tm, d), lambda i: (i, 0)),
                  pl.BlockSpec((tm, d), lambda i: (i, 0)),
                  pl.BlockSpec((1, 9, d), lambda i: (i // per_batch, 0, 0)),
                  pl.BlockSpec((None, d, d), lambda i: (j, 0, 0), pipeline_mode=pl.Buffered(1))],
        out_specs=pl.BlockSpec((tm, d), lambda i: (i, 0)),
        out_shape=jax.ShapeDtypeStruct((t, d), F32),
        compiler_params=_params(("parallel",)),
        name="oproj",
    )(x, attn, ada, w_o)


def kernel(x, c, norm_g, w_ada, b_ada, w_ffn_in, w_ffn_out, w_conv_in, conv_w, conv_b, w_conv_out,
           kv_norm_g, w_ada_kv, b_ada_kv, w_kvf, b_fgate, w_q, w_o, final_g):
    b, s, d = x.shape
    depth = norm_g.shape[0]
    n_a = w_conv_in.shape[0]
    t = b * s
    assert w_q.shape[0] == 1 and depth == n_a + 1

    w_ffn_in_b = w_ffn_in.astype(BF16)
    w_ffn_out_b = w_ffn_out.astype(BF16)
    w_conv_in_b = w_conv_in.astype(BF16)
    w_conv_out_b = w_conv_out.astype(BF16)
    w_kv_b = w_kvf[:, :2 * d].astype(BF16)
    w_f_b = jnp.pad(w_kvf[:, 2 * d:], ((0, 0), (0, LANES - N_HEADS))).astype(BF16)
    b_f = jnp.pad(b_fgate, (0, LANES - N_HEADS)).reshape(1, LANES)
    w_q_b = w_q.astype(BF16)
    w_o_b = w_o.astype(BF16)
    conv_b3 = conv_b.reshape(n_a, 1, d)

    c_pad = jnp.pad(c, ((0, SUBLANES - b), (0, 0)))
    xf = x.reshape(t, d)
    for l in range(depth):
        ada = _ada(c_pad, w_ada, b_ada, l)[:b].reshape(b, 9, d)
        xf = _ffn(xf, ada, norm_g[l, 0], w_ffn_in_b, w_ffn_out_b, l, 0, s)
        if l < n_a:
            xf = _conv(xf, ada, norm_g[l, 1], w_conv_in_b, conv_w, conv_b3, w_conv_out_b, l, s)
        else:
            j = l - n_a
            ada_kv = _ada(c_pad, w_ada_kv[None], b_ada_kv[None], 0)[:b].reshape(b, 2, d)
            q, k, v, fc = _qkv(xf, ada, ada_kv, norm_g[l, 1], kv_norm_g, w_q_b, w_kv_b, w_f_b, b_f, j, s)
            attn = _attn(q.reshape(b, s, d), k.reshape(b, s, d), v.reshape(b, s, d),
                         fc.reshape(b, s, LANES))
            xf = _oproj(xf, attn.reshape(t, d), ada, w_o_b, j, s)
        last = l == depth - 1
        xf = _ffn(xf, ada, norm_g[l, 2], w_ffn_in_b, w_ffn_out_b, l, 1, s,
                  final_g=final_g if last else None)
    return xf.reshape(b, s, d)
```

```python
import functools
import math

import jax
import jax.numpy as jnp
from jax import lax
from jax.experimental import pallas as pl
from jax.experimental.pallas import tpu as pltpu

EPS = 1e-6
N_HEADS = 16
HEAD_DIM = 128
CONV_WIDTH = 3
LOG2E = math.log2(math.e)
LANES = 128
SUBLANES = 8
MXU_DIM = 256
VMEM_LIMIT_BYTES = 59 * 1024 * 1024

F32 = jnp.float32
BF16 = jnp.bfloat16


def _params(semantics):
    return pltpu.CompilerParams(dimension_semantics=semantics,
                                vmem_limit_bytes=VMEM_LIMIT_BYTES)


def _dot(a, b):
    return jnp.dot(a, b, preferred_element_type=F32)


def _rmsnorm(x, g):
    return x * lax.rsqrt(jnp.mean(x * x, axis=-1, keepdims=True) + EPS) * g


NORM_ROWS = 16
NORM_UNROLL = 8
def _modulated_norm_to(x_ref, *targets):
    params = [(h_ref, g_ref[...], shift, 1.0 + scale) for h_ref, g_ref, shift, scale in targets]

    def body(r, _):
        rows = pl.ds(pl.multiple_of(r * NORM_ROWS, NORM_ROWS), NORM_ROWS)
        x = x_ref[rows, :]
        xhat = x * lax.rsqrt(jnp.mean(x * x, axis=-1, keepdims=True) + EPS)
        for h_ref, g, shift, scale1 in params:
            h_ref[rows, :] = (xhat * g * scale1 + shift).astype(BF16)
        return 0
    lax.fori_loop(0, x_ref.shape[0] // NORM_ROWS, body, 0, unroll=NORM_UNROLL)


def _residual_to(o_ref, x_ref, coef, final_g):
    y = x_ref[...] + coef * o_ref[...]
    o_ref[...] = y if final_g is None else _rmsnorm(y, final_g)


def _ada_rows(ada_ref, sub):
    return tuple(ada_ref[0, 3 * sub + k:3 * sub + k + 1, :] for k in range(3))


def _ada_kernel(c_ref, w_ref, b_ref, o_ref):
    c = c_ref[...]
    cond = c * jax.nn.sigmoid(c)
    o_ref[...] = _dot(cond.astype(BF16), w_ref[...].astype(BF16)) + b_ref[...]


def _ada(c_pad, w, b, l, tn=1024):
    nl, d, n = w.shape
    return pl.pallas_call(
        _ada_kernel,
        grid=(n // tn,),
        in_specs=[pl.BlockSpec((SUBLANES, d), lambda j: (0, 0)),
                  pl.BlockSpec((None, d, tn), lambda j: (l, 0, j)),
                  pl.BlockSpec((None, 1, tn), lambda j: (l, 0, j))],
        out_specs=pl.BlockSpec((SUBLANES, tn), lambda j: (0, j)),
        out_shape=jax.ShapeDtypeStruct((SUBLANES, n), F32),
        compiler_params=_params(("parallel",)),
        name="ada",
    )(c_pad, w, b.reshape(nl, 1, n))


def _ffn_kernel(x_ref, ada_ref, g_ref, wg_ref, wu_ref, wo_ref, *rest, sub, final, cast_next):
    rest = list(rest)
    fg_ref = rest.pop(0) if final else None
    if cast_next:
        nin_ref, nout_ref, o_ref, nin_b_ref, nout_b_ref, h_ref = rest
    else:
        o_ref, h_ref = rest
    f = pl.program_id(1)

    shift, scale, gate = _ada_rows(ada_ref, sub)

    @pl.when(f == 0)
    def _():
        _modulated_norm_to(x_ref, (h_ref, g_ref, shift, scale))
        o_ref[...] = jnp.zeros_like(o_ref)

    if cast_next:
        nin_b_ref[...] = nin_ref[...].astype(BF16)
        nout_b_ref[...] = nout_ref[...].astype(BF16)
    h = h_ref[...]
    a = _dot(h, wg_ref[...])
    b = _dot(h, wu_ref[...])
    act = (a * jax.nn.sigmoid(a) * b).astype(BF16)
    o_ref[...] += _dot(act, wo_ref[...])

    @pl.when(f == pl.num_programs(1) - 1)
    def _():
        _residual_to(o_ref, x_ref, 0.5 * gate, fg_ref[...] if final else None)


def _ffn(x, ada, g, w_in, w_out, sub, seq, final_g=None, cast_next=None, tf=512):
    t, d = x.shape
    ff = w_out.shape[0]
    nf = ff // tf
    final = final_g is not None
    tm = 512 if final else 1024
    per_batch = seq // tm
    ni = t // tm
    in_specs = [pl.BlockSpec((tm, d), lambda i, f: (i, 0)),
                pl.BlockSpec((1, 9, d), lambda i, f: (i // per_batch, 0, 0)),
                pl.BlockSpec((1, d), lambda i, f: (0, 0)),
                pl.BlockSpec((d, tf), lambda i, f: (0, f)),
                pl.BlockSpec((d, tf), lambda i, f: (0, nf + f)),
                pl.BlockSpec((tf, d), lambda i, f: (f, 0))]
    args = [x, ada, g.reshape(1, d), w_in, w_in, w_out]
    out_specs = [pl.BlockSpec((tm, d), lambda i, f: (i, 0))]
    out_shape = [jax.ShapeDtypeStruct((t, d), F32)]
    if final:
        in_specs.append(pl.BlockSpec((1, d), lambda i, f: (0, 0)))
        args.append(final_g.reshape(1, d))
    if cast_next:
        w_in32, w_out32, l2, idx2 = cast_next
        in_tile = (d // ni, 2 * ff // nf)
        out_tile = (ff // (ni * nf), d)
        in_specs += [pl.BlockSpec((None, None) + in_tile, lambda i, f: (l2, idx2, i, f)),
                     pl.BlockSpec((None, None) + out_tile, lambda i, f: (l2, idx2, i * nf + f, 0))]
        args += [w_in32, w_out32]
        out_specs += [pl.BlockSpec(in_tile, lambda i, f: (i, f)),
                      pl.BlockSpec(out_tile, lambda i, f: (i * nf + f, 0))]
        out_shape += [jax.ShapeDtypeStruct((d, 2 * ff), BF16), jax.ShapeDtypeStruct((ff, d), BF16)]
    outs = pl.pallas_call(
        functools.partial(_ffn_kernel, sub=sub, final=final, cast_next=bool(cast_next)),
        grid=(ni, nf),
        in_specs=in_specs,
        out_specs=out_specs,
        out_shape=out_shape,
        scratch_shapes=[pltpu.VMEM((tm, d), BF16)],
        compiler_params=_params(("parallel", "arbitrary")),
        name="ffn_final" if final else "ffn",
    )(*args)
    return outs if cast_next else outs[0]


def _conv_kernel(x_ref, ada_ref, g_ref, wb_ref, wc_ref, wx_ref, cw_ref, cb_ref, wo_ref,
                 o_ref, h_ref, tail_ref, *, per_batch):
    i = pl.program_id(0)
    c = pl.program_id(1)
    tm = x_ref.shape[0]

    shift, scale, gate = _ada_rows(ada_ref, 1)

    @pl.when(c == 0)
    def _():
        _modulated_norm_to(x_ref, (h_ref, g_ref, shift, scale))
        o_ref[...] = jnp.zeros_like(o_ref)

    @pl.when(i % per_batch == 0)
    def _():
        tail_ref[c] = jnp.zeros(tail_ref.shape[1:], F32)

    h = h_ref[...]
    bg = _dot(h, wb_ref[...])
    cg = _dot(h, wc_ref[...])
    xv = _dot(h, wx_ref[...])
    u = cg * xv
    tail = tail_ref[c]
    p1 = tail[SUBLANES - 1:SUBLANES, :]
    p2 = tail[SUBLANES - 2:SUBLANES - 1, :]
    row = lax.broadcasted_iota(jnp.int32, u.shape, 0)
    u1 = jnp.where(row == 0, p1, pltpu.roll(u, 1, 0))
    u2 = jnp.where(row == 0, p2, jnp.where(row == 1, p1, pltpu.roll(u, 2, 0)))
    conv = cw_ref[0:1, :] * u2 + cw_ref[1:2, :] * u1 + cw_ref[2:3, :] * u + cb_ref[...]
    tail_ref[c] = u[tm - SUBLANES:, :]
    o_ref[...] += _dot((bg * conv).astype(BF16), wo_ref[...])

    @pl.when(c == pl.num_programs(1) - 1)
    def _():
        _residual_to(o_ref, x_ref, gate, None)


def _conv(x, ada, g, w_in, conv_w, conv_b, w_out, l, seq, tm=512, tc=512):
    t, d = x.shape
    nc = d // tc
    per_batch = seq // tm
    return pl.pallas_call(
        functools.partial(_conv_kernel, per_batch=per_batch),
        grid=(t // tm, nc),
        in_specs=[pl.BlockSpec((tm, d), lambda i, c: (i, 0)),
                  pl.BlockSpec((1, 9, d), lambda i, c: (i // per_batch, 0, 0)),
                  pl.BlockSpec((1, d), lambda i, c: (0, 0)),
                  pl.BlockSpec((None, d, tc), lambda i, c: (l, 0, c)),
                  pl.BlockSpec((None, d, tc), lambda i, c: (l, 0, nc + c)),
                  pl.BlockSpec((None, d, tc), lambda i, c: (l, 0, 2 * nc + c)),
                  pl.BlockSpec((None, CONV_WIDTH, tc), lambda i, c: (l, 0, c)),
                  pl.BlockSpec((None, 1, tc), lambda i, c: (l, 0, c)),
                  pl.BlockSpec((None, tc, d), lambda i, c: (l, c, 0))],
        out_specs=pl.BlockSpec((tm, d), lambda i, c: (i, 0)),
        out_shape=jax.ShapeDtypeStruct((t, d), F32),
        scratch_shapes=[pltpu.VMEM((tm, d), BF16),
                        pltpu.VMEM((nc, SUBLANES, tc), F32)],
        compiler_params=_params(("arbitrary", "arbitrary")),
        name="conv",
    )(x, ada, g.reshape(1, d), w_in, w_in, w_in, conv_w, conv_b, w_out)


def _qkv_kernel(x_ref, ada_ref, adakv_ref, g_ref, gkv_ref, wq_ref, wk_ref, wv_ref, wf_ref, bf_ref,
                q_ref, k_ref, v_ref, fc_ref, hq_ref, hkv_ref, carry_ref, *, per_batch, q_scale):
    i = pl.program_id(0)
    c = pl.program_id(1)
    tm = x_ref.shape[0]

    @pl.when(c == 0)
    def _():
        shift, scale, _ = _ada_rows(ada_ref, 1)
        _modulated_norm_to(x_ref, (hq_ref, g_ref, shift, scale),
                           (hkv_ref, gkv_ref, adakv_ref[0, 0:1, :], adakv_ref[0, 1:2, :]))
        hkv = hkv_ref[...]

        @pl.when(i % per_batch == 0)
        def _():
            carry_ref[...] = jnp.zeros_like(carry_ref)

        zf = _dot(hkv, wf_ref[...]) + bf_ref[...]
        ls = jnp.minimum(zf, 0.0) - jnp.log1p(jnp.exp(-jnp.abs(zf)))
        hi = ls.astype(BF16)
        r1 = ls - hi.astype(F32)
        mid = r1.astype(BF16)
        lo = (r1 - mid.astype(F32)).astype(BF16)
        rr = lax.broadcasted_iota(jnp.int32, (tm, tm), 0)
        cc = lax.broadcasted_iota(jnp.int32, (tm, tm), 1)
        tri = (rr >= cc).astype(BF16)
        parts = _dot(tri, jnp.concatenate([hi, mid, lo], axis=1))
        cum = (parts[:, :LANES] + parts[:, LANES:2 * LANES]) + parts[:, 2 * LANES:] + carry_ref[0:1, :]
        carry_ref[...] = jnp.broadcast_to(cum[tm - 1:tm, :], carry_ref.shape)
        fc_ref[...] = cum

    hkv = hkv_ref[...]
    q_ref[...] = (_dot(hq_ref[...], wq_ref[...]) * q_scale).astype(BF16)
    k_ref[...] = _dot(hkv, wk_ref[...]).astype(BF16)
    v_ref[...] = _dot(hkv, wv_ref[...]).astype(BF16)


def _qkv(x, ada, ada_kv, g, gkv, w_q, w_kv, w_f, b_f, j, seq, tm=1024, tn=512):
    t, d = x.shape
    nn = d // tn
    per_batch = seq // tm
    q_scale = LOG2E / math.sqrt(HEAD_DIM)
    row_blk = pl.BlockSpec((tm, tn), lambda i, c: (i, c))
    return pl.pallas_call(
        functools.partial(_qkv_kernel, per_batch=per_batch, q_scale=q_scale),
        grid=(t // tm, nn),
        in_specs=[pl.BlockSpec((tm, d), lambda i, c: (i, 0)),
                  pl.BlockSpec((1, 9, d), lambda i, c: (i // per_batch, 0, 0)),
                  pl.BlockSpec((1, 2, d), lambda i, c: (i // per_batch, 0, 0)),
                  pl.BlockSpec((1, d), lambda i, c: (0, 0)),
                  pl.BlockSpec((1, d), lambda i, c: (0, 0)),
                  pl.BlockSpec((None, d, tn), lambda i, c: (j, 0, c)),
                  pl.BlockSpec((d, tn), lambda i, c: (0, c)),
                  pl.BlockSpec((d, tn), lambda i, c: (0, nn + c)),
                  pl.BlockSpec((d, LANES), lambda i, c: (0, 0)),
                  pl.BlockSpec((1, LANES), lambda i, c: (0, 0))],
        out_specs=[row_blk, row_blk, row_blk,
                   pl.BlockSpec((tm, LANES), lambda i, c: (i, 0))],
        out_shape=[jax.ShapeDtypeStruct((t, d), BF16),
                   jax.ShapeDtypeStruct((t, d), BF16),
                   jax.ShapeDtypeStruct((t, d), BF16),
                   jax.ShapeDtypeStruct((t, LANES), F32)],
        scratch_shapes=[pltpu.VMEM((tm, d), BF16),
                        pltpu.VMEM((tm, d), BF16),
                        pltpu.VMEM((SUBLANES, LANES), F32)],
        compiler_params=_params(("arbitrary", "arbitrary")),
        name="qkv",
    )(x, ada, ada_kv, g.reshape(1, d), gkv.reshape(1, d), w_q, w_kv, w_kv, w_f, b_f)


def _split3(f):
    hi = f.astype(BF16)
    r1 = f - hi.astype(F32)
    mid = r1.astype(BF16)
    lo = (r1 - mid.astype(F32)).astype(BF16)
    return hi.astype(F32), mid.astype(F32), lo.astype(F32)


def _attn_kernel(q_ref, k_ref, v_ref, fc_ref, o_ref, kaug_ref, vt_ref, st_ref, *, tq, tk, hp):
    hg = pl.program_id(1)
    i = pl.program_id(2)
    seq = k_ref.shape[1]

    def head_col(blk, head):
        lane = lax.broadcasted_iota(jnp.int32, blk.shape, 1)
        return jnp.sum(jnp.where(lane == head, blk, 0.0), axis=1, keepdims=True) * LOG2E

    @pl.when(i == 0)
    def _():
        lane = lax.broadcasted_iota(jnp.int32, (tk, LANES), 1)

        def build(r, _):
            start = pl.multiple_of(r * tk, tk)
            blk = fc_ref[0, pl.ds(start, tk), :]
            for hh in range(hp):
                hi, mid, lo = _split3(head_col(blk, hg * hp + hh))
                aug = jnp.where(lane < 3, 1.0,
                                jnp.where(lane == 3, -hi,
                                          jnp.where(lane == 4, -mid, jnp.where(lane == 5, -lo, 0.0))))
                kaug_ref[hh, pl.ds(start, tk), :HEAD_DIM] = k_ref[0, pl.ds(start, tk),
                                                                  hh * HEAD_DIM:(hh + 1) * HEAD_DIM]
                kaug_ref[hh, pl.ds(start, tk), HEAD_DIM:] = aug.astype(BF16)
                vblk = v_ref[0, pl.ds(start, tk), hh * HEAD_DIM:(hh + 1) * HEAD_DIM]
                vt_ref[hh, :, pl.ds(start, tk)] = vblk.astype(F32).T.astype(BF16)
            return 0
        lax.fori_loop(0, seq // tk, build, 0)

    fblk = fc_ref[0, pl.ds(pl.multiple_of(i * tq, tq), tq), :]
    lane = lax.broadcasted_iota(jnp.int32, (tq, LANES), 1)
    qts = []
    for hh in range(hp):
        hi, mid, lo = _split3(head_col(fblk, hg * hp + hh))
        aq = jnp.where(lane == 0, hi,
                       jnp.where(lane == 1, mid,
                                 jnp.where(lane == 2, lo, jnp.where(lane < 6, 1.0, 0.0))))
        qh = q_ref[0, :, hh * HEAD_DIM:(hh + 1) * HEAD_DIM].astype(F32)
        qts.append(jnp.concatenate([qh.T, aq.T], axis=0).astype(BF16))

    def scores(j, slot, lo=0):
        start = pl.multiple_of(j * tk, tk)
        for hh in range(hp):
            st_ref[slot, hh, :, lo:] = _dot(kaug_ref[hh, pl.ds(start, tk), :], qts[hh][:, lo:])

    def softmax_pv(j, slot, carry, diag):
        start = pl.multiple_of(j * tk, tk)
        lo = 0 if diag is None else diag * tk
        out = []
        for hh in range(hp):
            m0, l0, acc0 = carry[hh]
            m, l, acc = m0[:, lo:], l0[:, lo:], acc0[:, lo:]
            st = st_ref[slot, hh, :, lo:]
            if diag is not None:
                key = lax.broadcasted_iota(jnp.int32, st.shape, 0)
                qry = lax.broadcasted_iota(jnp.int32, st.shape, 1)
                st = jnp.where(key <= qry, st, -jnp.inf)
            m_new = jnp.maximum(m, jnp.max(st, axis=0, keepdims=True))
            alpha = jnp.exp2(m - m_new)
            pt = jnp.exp2(st - m_new)
            l = alpha * l + jnp.sum(pt, axis=0, keepdims=True)
            acc = alpha * acc + _dot(vt_ref[hh, :, pl.ds(start, tk)], pt.astype(BF16))
            if lo:
                m_new, l, acc = (jnp.concatenate([old[:, :lo], new], axis=1)
                                 for old, new in ((m0, m_new), (l0, l), (acc0, acc)))
            out.append((m_new, l, acc))
        return tuple(out)

    def pair(jj, carry):
        a = 2 * jj
        scores(a + 1, 1)
        carry = softmax_pv(a, 0, carry, None)
        scores(a + 2, 0)
        return softmax_pv(a + 1, 1, carry, None)

    init = tuple((jnp.full((1, tq), -jnp.inf, F32), jnp.zeros((1, tq), F32),
                  jnp.zeros((HEAD_DIM, tq), F32)) for _ in range(hp))
    scores(0, 0)
    carry = lax.fori_loop(0, i, pair, init)
    scores(2 * i + 1, 1, lo=tk)
    carry = softmax_pv(2 * i, 0, carry, 0)
    carry = softmax_pv(2 * i + 1, 1, carry, 1)
    for hh in range(hp):
        _, l, acc = carry[hh]
        o_ref[0, :, hh * HEAD_DIM:(hh + 1) * HEAD_DIM] = (acc / l).T.astype(BF16)


def _attn(q, k, v, fc, tq=1024, hp=2):
    b, s, d = q.shape
    w = hp * HEAD_DIM
    tk = tq // 2
    return pl.pallas_call(
        functools.partial(_attn_kernel, tq=tq, tk=tk, hp=hp),
        grid=(b, N_HEADS // hp, s // tq),
        in_specs=[pl.BlockSpec((1, tq, w), lambda bb, h, i: (bb, i, h)),
                  pl.BlockSpec((1, s, w), lambda bb, h, i: (bb, 0, h)),
                  pl.BlockSpec((1, s, w), lambda bb, h, i: (bb, 0, h)),
                  pl.BlockSpec((1, s, LANES), lambda bb, h, i: (bb, 0, 0))],
        out_specs=pl.BlockSpec((1, tq, w), lambda bb, h, i: (bb, i, h)),
        out_shape=jax.ShapeDtypeStruct((b, s, d), BF16),
        scratch_shapes=[pltpu.VMEM((hp, s, 2 * HEAD_DIM), BF16),
                        pltpu.VMEM((hp, HEAD_DIM, s), BF16),
                        pltpu.VMEM((2, hp, tk, tq), F32)],
        compiler_params=_params(("parallel", "parallel", "arbitrary")),
        name="attn",
    )(q, k, v, fc)


def _oproj_kernel(x_ref, a_ref, ada_ref, w_ref, o_ref):
    gate = ada_ref[0, 5:6, :]
    o_ref[...] = x_ref[...] + gate * _dot(a_ref[...], w_ref[...])


def _oproj(x, attn, ada, w_o, j, seq, tm=1024):
    t, d = x.shape
    per_batch = seq // tm
    return pl.pallas_call(
        _oproj_kernel,
        grid=(t // tm,),
        in_specs=[pl.BlockSpec((tm, d), lambda i: (i, 0)),
                  pl.BlockSpec((tm, d), lambda i: (i, 0)),
                  pl.BlockSpec((1, 9, d), lambda i: (i // per_batch, 0, 0)),
                  pl.BlockSpec((None, d, d), lambda i: (j, 0, 0), pipeline_mode=pl.Buffered(1))],
        out_specs=pl.BlockSpec((tm, d), lambda i: (i, 0)),
        out_shape=jax.ShapeDtypeStruct((t, d), F32),
        compiler_params=_params(("parallel",)),
        name="oproj",
    )(x, attn, ada, w_o)


def kernel(x, c, norm_g, w_ada, b_ada, w_ffn_in, w_ffn_out, w_conv_in, conv_w, conv_b, w_conv_out,
           kv_norm_g, w_ada_kv, b_ada_kv, w_kvf, b_fgate, w_q, w_o, final_g):
    b, s, d = x.shape
    depth = norm_g.shape[0]
    n_a = w_conv_in.shape[0]
    t = b * s
    assert w_q.shape[0] == 1 and depth == n_a + 1

    w_in_b = w_ffn_in[0, 0].astype(BF16)
    w_out_b = w_ffn_out[0, 0].astype(BF16)
    w_conv_in_b = w_conv_in.astype(BF16)
    w_conv_out_b = w_conv_out.astype(BF16)
    w_kv_b = w_kvf[:, :2 * d].astype(BF16)
    w_f_b = jnp.pad(w_kvf[:, 2 * d:], ((0, 0), (0, LANES - N_HEADS))).astype(BF16)
    b_f = jnp.pad(b_fgate, (0, LANES - N_HEADS)).reshape(1, LANES)
    w_q_b = w_q.astype(BF16)
    w_o_b = w_o.astype(BF16)
    conv_b3 = conv_b.reshape(n_a, 1, d)

    c_pad = jnp.pad(c, ((0, SUBLANES - b), (0, 0)))
    xf = x.reshape(t, d)
    for l in range(depth):
        ada = _ada(c_pad, w_ada, b_ada, l)[:b].reshape(b, 9, d)
        xf, w_in_b, w_out_b = _ffn(xf, ada, norm_g[l, 0], w_in_b, w_out_b, 0, s,
                                   cast_next=(w_ffn_in, w_ffn_out, l, 1))
        if l < n_a:
            xf = _conv(xf, ada, norm_g[l, 1], w_conv_in_b, conv_w, conv_b3, w_conv_out_b, l, s)
        else:
            j = l - n_a
            ada_kv = _ada(c_pad, w_ada_kv[None], b_ada_kv[None], 0)[:b].reshape(b, 2, d)
            q, k, v, fc = _qkv(xf, ada, ada_kv, norm_g[l, 1], kv_norm_g, w_q_b, w_kv_b, w_f_b, b_f, j, s)
            attn = _attn(q.reshape(b, s, d), k.reshape(b, s, d), v.reshape(b, s, d),
                         fc.reshape(b, s, LANES))
            xf = _oproj(xf, attn.reshape(t, d), ada, w_o_b, j, s)
        if l == depth - 1:
            xf = _ffn(xf, ada, norm_g[l, 2], w_in_b, w_out_b, 2, s, final_g=final_g)
        else:
            xf, w_in_b, w_out_b = _ffn(xf, ada, norm_g[l, 2], w_in_b, w_out_b, 2, s,
                                       cast_next=(w_ffn_in, w_ffn_out, l + 1, 0))
    return xf.reshape(b, s, d)
```

```python
import functools
import math

import jax
import jax.numpy as jnp
from jax import lax
from jax.experimental import pallas as pl
from jax.experimental.pallas import tpu as pltpu

EPS = 1e-6
N_HEADS = 16
HEAD_DIM = 128
CONV_WIDTH = 3
LOG2E = math.log2(math.e)
LANES = 128
SUBLANES = 8
MXU_DIM = 256
VMEM_LIMIT_BYTES = 59 * 1024 * 1024

F32 = jnp.float32
BF16 = jnp.bfloat16


def _params(semantics):
    return pltpu.CompilerParams(dimension_semantics=semantics,
                                vmem_limit_bytes=VMEM_LIMIT_BYTES)


def _dot(a, b):
    return jnp.dot(a, b, preferred_element_type=F32)


def _rmsnorm(x, g):
    return x * lax.rsqrt(jnp.mean(x * x, axis=-1, keepdims=True) + EPS) * g


NORM_ROWS = 16
NORM_UNROLL = 8
def _modulated_norm_to(x_ref, *targets):
    params = [(h_ref, g_ref[...], shift, 1.0 + scale) for h_ref, g_ref, shift, scale in targets]

    def body(r, _):
        rows = pl.ds(pl.multiple_of(r * NORM_ROWS, NORM_ROWS), NORM_ROWS)
        x = x_ref[rows, :]
        xhat = x * lax.rsqrt(jnp.mean(x * x, axis=-1, keepdims=True) + EPS)
        for h_ref, g, shift, scale1 in params:
            h_ref[rows, :] = (xhat * g * scale1 + shift).astype(BF16)
        return 0
    lax.fori_loop(0, x_ref.shape[0] // NORM_ROWS, body, 0, unroll=NORM_UNROLL)


def _residual_to(o_ref, x_ref, coef, final_g=None, rs_ref=None):
    if final_g is None:
        o_ref[...] = x_ref[...] + coef * o_ref[...]
        return
    n = x_ref.shape[0] // NORM_ROWS

    def rows_of(r):
        return pl.ds(pl.multiple_of(r * NORM_ROWS, NORM_ROWS), NORM_ROWS)

    def stats(r, _):
        y = x_ref[rows_of(r), :] + coef * o_ref[rows_of(r), :]
        rs = lax.rsqrt(jnp.mean(y * y, axis=-1, keepdims=True) + EPS)
        rs_ref[rows_of(r), :] = jnp.broadcast_to(rs, (NORM_ROWS, LANES))
        return 0
    lax.fori_loop(0, n, stats, 0, unroll=NORM_UNROLL)

    def scale(r, _):
        y = x_ref[rows_of(r), :] + coef * o_ref[rows_of(r), :]
        o_ref[rows_of(r), :] = y * rs_ref[rows_of(r), 0:1] * final_g
        return 0
    lax.fori_loop(0, n, scale, 0, unroll=NORM_UNROLL)


def _ada_rows(ada_ref, sub):
    return tuple(ada_ref[0, 3 * sub + k:3 * sub + k + 1, :] for k in range(3))


def _ada_kernel(c_ref, w_ref, b_ref, o_ref):
    c = c_ref[...]
    cond = c * jax.nn.sigmoid(c)
    o_ref[...] = _dot(cond.astype(BF16), w_ref[...].astype(BF16)) + b_ref[...]


def _ada(c_pad, w, b, l, tn=1024):
    nl, d, n = w.shape
    return pl.pallas_call(
        _ada_kernel,
        grid=(n // tn,),
        in_specs=[pl.BlockSpec((SUBLANES, d), lambda j: (0, 0)),
                  pl.BlockSpec((None, d, tn), lambda j: (l, 0, j)),
                  pl.BlockSpec((None, 1, tn), lambda j: (l, 0, j))],
        out_specs=pl.BlockSpec((SUBLANES, tn), lambda j: (0, j)),
        out_shape=jax.ShapeDtypeStruct((SUBLANES, n), F32),
        compiler_params=_params(("parallel",)),
        name="ada",
    )(c_pad, w, b.reshape(nl, 1, n))


def _ffn_kernel(x_ref, ada_ref, g_ref, wg_ref, wu_ref, wo_ref, *rest, sub, final, cast_next):
    rest = list(rest)
    fg_ref = rest.pop(0) if final else None
    if cast_next:
        nin_ref, nout_ref, o_ref, nin_b_ref, nout_b_ref, h_ref = rest
    else:
        o_ref, h_ref = rest[:2]
        rs_ref = rest[2] if final else None
    f = pl.program_id(1)

    shift, scale, gate = _ada_rows(ada_ref, sub)

    @pl.when(f == 0)
    def _():
        _modulated_norm_to(x_ref, (h_ref, g_ref, shift, scale))
        o_ref[...] = jnp.zeros_like(o_ref)

    if cast_next:
        nin_b_ref[...] = nin_ref[...].astype(BF16)
        nout_b_ref[...] = nout_ref[...].astype(BF16)
    h = h_ref[...]
    a = _dot(h, wg_ref[...])
    b = _dot(h, wu_ref[...])
    act = (a * jax.nn.sigmoid(a) * b).astype(BF16)
    o_ref[...] += _dot(act, wo_ref[...])

    @pl.when(f == pl.num_programs(1) - 1)
    def _():
        if final:
            _residual_to(o_ref, x_ref, 0.5 * gate, fg_ref[...], rs_ref)
        else:
            _residual_to(o_ref, x_ref, 0.5 * gate)


def _ffn(x, ada, g, w_in, w_out, sub, seq, final_g=None, cast_next=None, tf=512):
    t, d = x.shape
    ff = w_out.shape[0]
    nf = ff // tf
    final = final_g is not None
    tm = 1024
    per_batch = seq // tm
    ni = t // tm
    in_specs = [pl.BlockSpec((tm, d), lambda i, f: (i, 0)),
                pl.BlockSpec((1, 9, d), lambda i, f: (i // per_batch, 0, 0)),
                pl.BlockSpec((1, d), lambda i, f: (0, 0)),
                pl.BlockSpec((d, tf), lambda i, f: (0, f)),
                pl.BlockSpec((d, tf), lambda i, f: (0, nf + f)),
                pl.BlockSpec((tf, d), lambda i, f: (f, 0))]
    args = [x, ada, g.reshape(1, d), w_in, w_in, w_out]
    out_specs = [pl.BlockSpec((tm, d), lambda i, f: (i, 0))]
    out_shape = [jax.ShapeDtypeStruct((t, d), F32)]
    if final:
        in_specs.append(pl.BlockSpec((1, d), lambda i, f: (0, 0)))
        args.append(final_g.reshape(1, d))
    if cast_next:
        w_in32, w_out32, l2, idx2 = cast_next
        in_tile = (d // ni, 2 * ff // nf)
        out_tile = (ff // (ni * nf), d)
        in_specs += [pl.BlockSpec((None, None) + in_tile, lambda i, f: (l2, idx2, i, f)),
                     pl.BlockSpec((None, None) + out_tile, lambda i, f: (l2, idx2, i * nf + f, 0))]
        args += [w_in32, w_out32]
        out_specs += [pl.BlockSpec(in_tile, lambda i, f: (i, f)),
                      pl.BlockSpec(out_tile, lambda i, f: (i * nf + f, 0))]
        out_shape += [jax.ShapeDtypeStruct((d, 2 * ff), BF16), jax.ShapeDtypeStruct((ff, d), BF16)]
    outs = pl.pallas_call(
        functools.partial(_ffn_kernel, sub=sub, final=final, cast_next=bool(cast_next)),
        grid=(ni, nf),
        in_specs=in_specs,
        out_specs=out_specs,
        out_shape=out_shape,
        scratch_shapes=[pltpu.VMEM((tm, d), BF16)] + ([pltpu.VMEM((tm, LANES), F32)] if final else []),
        compiler_params=_params(("parallel", "arbitrary")),
        name="ffn_final" if final else "ffn",
    )(*args)
    return outs if cast_next else outs[0]


def _conv_kernel(x_ref, ada_ref, g_ref, wb_ref, wc_ref, wx_ref, cw_ref, cb_ref, wo_ref,
                 o_ref, h_ref, tail_ref, *, per_batch):
    i = pl.program_id(0)
    c = pl.program_id(1)
    tm = x_ref.shape[0]

    shift, scale, gate = _ada_rows(ada_ref, 1)

    @pl.when(c == 0)
    def _():
        _modulated_norm_to(x_ref, (h_ref, g_ref, shift, scale))
        o_ref[...] = jnp.zeros_like(o_ref)

    @pl.when(i % per_batch == 0)
    def _():
        tail_ref[c] = jnp.zeros(tail_ref.shape[1:], F32)

    h = h_ref[...]
    bg = _dot(h, wb_ref[...])
    cg = _dot(h, wc_ref[...])
    xv = _dot(h, wx_ref[...])
    u = cg * xv
    tail = tail_ref[c]
    p1 = tail[SUBLANES - 1:SUBLANES, :]
    p2 = tail[SUBLANES - 2:SUBLANES - 1, :]
    row = lax.broadcasted_iota(jnp.int32, u.shape, 0)
    u1 = jnp.where(row == 0, p1, pltpu.roll(u, 1, 0))
    u2 = jnp.where(row == 0, p2, jnp.where(row == 1, p1, pltpu.roll(u, 2, 0)))
    conv = cw_ref[0:1, :] * u2 + cw_ref[1:2, :] * u1 + cw_ref[2:3, :] * u + cb_ref[...]
    tail_ref[c] = u[tm - SUBLANES:, :]
    o_ref[...] += _dot((bg * conv).astype(BF16), wo_ref[...])

    @pl.when(c == pl.num_programs(1) - 1)
    def _():
        _residual_to(o_ref, x_ref, gate)


def _conv(x, ada, g, w_in, conv_w, conv_b, w_out, l, seq, tm=512, tc=512):
    t, d = x.shape
    nc = d // tc
    per_batch = seq // tm
    return pl.pallas_call(
        functools.partial(_conv_kernel, per_batch=per_batch),
        grid=(t // tm, nc),
        in_specs=[pl.BlockSpec((tm, d), lambda i, c: (i, 0)),
                  pl.BlockSpec((1, 9, d), lambda i, c: (i // per_batch, 0, 0)),
                  pl.BlockSpec((1, d), lambda i, c: (0, 0)),
                  pl.BlockSpec((None, d, tc), lambda i, c: (l, 0, c)),
                  pl.BlockSpec((None, d, tc), lambda i, c: (l, 0, nc + c)),
                  pl.BlockSpec((None, d, tc), lambda i, c: (l, 0, 2 * nc + c)),
                  pl.BlockSpec((None, CONV_WIDTH, tc), lambda i, c: (l, 0, c)),
                  pl.BlockSpec((None, 1, tc), lambda i, c: (l, 0, c)),
                  pl.BlockSpec((None, tc, d), lambda i, c: (l, c, 0))],
        out_specs=pl.BlockSpec((tm, d), lambda i, c: (i, 0)),
        out_shape=jax.ShapeDtypeStruct((t, d), F32),
        scratch_shapes=[pltpu.VMEM((tm, d), BF16),
                        pltpu.VMEM((nc, SUBLANES, tc), F32)],
        compiler_params=_params(("arbitrary", "arbitrary")),
        name="conv",
    )(x, ada, g.reshape(1, d), w_in, w_in, w_in, conv_w, conv_b, w_out)


def _qkv_kernel(x_ref, ada_ref, adakv_ref, g_ref, gkv_ref, wq_ref, wk_ref, wv_ref, wf_ref, bf_ref,
                q_ref, k_ref, v_ref, fc_ref, hq_ref, hkv_ref, carry_ref, *, per_batch, q_scale):
    i = pl.program_id(0)
    c = pl.program_id(1)
    tm = x_ref.shape[0]

    @pl.when(c == 0)
    def _():
        shift, scale, _ = _ada_rows(ada_ref, 1)
        _modulated_norm_to(x_ref, (hq_ref, g_ref, shift, scale),
                           (hkv_ref, gkv_ref, adakv_ref[0, 0:1, :], adakv_ref[0, 1:2, :]))
        hkv = hkv_ref[...]

        @pl.when(i % per_batch == 0)
        def _():
            carry_ref[...] = jnp.zeros_like(carry_ref)

        zf = _dot(hkv, wf_ref[...]) + bf_ref[...]
        ls = jnp.minimum(zf, 0.0) - jnp.log1p(jnp.exp(-jnp.abs(zf)))
        hi = ls.astype(BF16)
        r1 = ls - hi.astype(F32)
        mid = r1.astype(BF16)
        lo = (r1 - mid.astype(F32)).astype(BF16)
        rr = lax.broadcasted_iota(jnp.int32, (tm, tm), 0)
        cc = lax.broadcasted_iota(jnp.int32, (tm, tm), 1)
        tri = (rr >= cc).astype(BF16)
        parts = _dot(tri, jnp.concatenate([hi, mid, lo], axis=1))
        cum = (parts[:, :LANES] + parts[:, LANES:2 * LANES]) + parts[:, 2 * LANES:] + carry_ref[0:1, :]
        carry_ref[...] = jnp.broadcast_to(cum[tm - 1:tm, :], carry_ref.shape)
        fc_ref[...] = cum

    hkv = hkv_ref[...]
    q_ref[...] = (_dot(hq_ref[...], wq_ref[...]) * q_scale).astype(BF16)
    k_ref[...] = _dot(hkv, wk_ref[...]).astype(BF16)
    v_ref[...] = _dot(hkv, wv_ref[...]).astype(BF16)


def _qkv(x, ada, ada_kv, g, gkv, w_q, w_kv, w_f, b_f, j, seq, tm=1024, tn=512):
    t, d = x.shape
    nn = d // tn
    per_batch = seq // tm
    q_scale = LOG2E / math.sqrt(HEAD_DIM)
    row_blk = pl.BlockSpec((tm, tn), lambda i, c: (i, c))
    return pl.pallas_call(
        functools.partial(_qkv_kernel, per_batch=per_batch, q_scale=q_scale),
        grid=(t // tm, nn),
        in_specs=[pl.BlockSpec((tm, d), lambda i, c: (i, 0)),
                  pl.BlockSpec((1, 9, d), lambda i, c: (i // per_batch, 0, 0)),
                  pl.BlockSpec((1, 2, d), lambda i, c: (i // per_batch, 0, 0)),
                  pl.BlockSpec((1, d), lambda i, c: (0, 0)),
                  pl.BlockSpec((1, d), lambda i, c: (0, 0)),
                  pl.BlockSpec((None, d, tn), lambda i, c: (j, 0, c)),
                  pl.BlockSpec((d, tn), lambda i, c: (0, c)),
                  pl.BlockSpec((d, tn), lambda i, c: (0, nn + c)),
                  pl.BlockSpec((d, LANES), lambda i, c: (0, 0)),
                  pl.BlockSpec((1, LANES), lambda i, c: (0, 0))],
        out_specs=[row_blk, row_blk, row_blk,
                   pl.BlockSpec((tm, LANES), lambda i, c: (i, 0))],
        out_shape=[jax.ShapeDtypeStruct((t, d), BF16),
                   jax.ShapeDtypeStruct((t, d), BF16),
                   jax.ShapeDtypeStruct((t, d), BF16),
                   jax.ShapeDtypeStruct((t, LANES), F32)],
        scratch_shapes=[pltpu.VMEM((tm, d), BF16),
                        pltpu.VMEM((tm, d), BF16),
                        pltpu.VMEM((SUBLANES, LANES), F32)],
        compiler_params=_params(("arbitrary", "arbitrary")),
        name="qkv",
    )(x, ada, ada_kv, g.reshape(1, d), gkv.reshape(1, d), w_q, w_kv, w_kv, w_f, b_f)


def _split3(f):
    hi = f.astype(BF16)
    r1 = f - hi.astype(F32)
    mid = r1.astype(BF16)
    lo = (r1 - mid.astype(F32)).astype(BF16)
    return hi.astype(F32), mid.astype(F32), lo.astype(F32)


def _attn_kernel(q_ref, k_ref, v_ref, fc_ref, o_ref, kaug_ref, vt_ref, st_ref, *, tq, tk, hp):
    hg = pl.program_id(1)
    i = pl.program_id(2)
    seq = k_ref.shape[1]

    def head_col(blk, head):
        lane = lax.broadcasted_iota(jnp.int32, blk.shape, 1)
        return jnp.sum(jnp.where(lane == head, blk, 0.0), axis=1, keepdims=True) * LOG2E

    @pl.when(i == 0)
    def _():
        lane = lax.broadcasted_iota(jnp.int32, (tk, LANES), 1)

        def build(r, _):
            start = pl.multiple_of(r * tk, tk)
            blk = fc_ref[0, pl.ds(start, tk), :]
            for hh in range(hp):
                hi, mid, lo = _split3(head_col(blk, hg * hp + hh))
                aug = jnp.where(lane < 3, 1.0,
                                jnp.where(lane == 3, -hi,
                                          jnp.where(lane == 4, -mid, jnp.where(lane == 5, -lo, 0.0))))
                kaug_ref[hh, pl.ds(start, tk), :HEAD_DIM] = k_ref[0, pl.ds(start, tk),
                                                                  hh * HEAD_DIM:(hh + 1) * HEAD_DIM]
                kaug_ref[hh, pl.ds(start, tk), HEAD_DIM:] = aug.astype(BF16)
                vblk = v_ref[0, pl.ds(start, tk), hh * HEAD_DIM:(hh + 1) * HEAD_DIM]
                vt_ref[hh, :, pl.ds(start, tk)] = vblk.astype(F32).T.astype(BF16)
            return 0
        lax.fori_loop(0, seq // tk, build, 0)

    fblk = fc_ref[0, pl.ds(pl.multiple_of(i * tq, tq), tq), :]
    lane = lax.broadcasted_iota(jnp.int32, (tq, LANES), 1)
    qts = []
    for hh in range(hp):
        hi, mid, lo = _split3(head_col(fblk, hg * hp + hh))
        aq = jnp.where(lane == 0, hi,
                       jnp.where(lane == 1, mid,
                                 jnp.where(lane == 2, lo, jnp.where(lane < 6, 1.0, 0.0))))
        qh = q_ref[0, :, hh * HEAD_DIM:(hh + 1) * HEAD_DIM].astype(F32)
        qts.append(jnp.concatenate([qh.T, aq.T], axis=0).astype(BF16))

    def scores(j, slot, lo=0):
        start = pl.multiple_of(j * tk, tk)
        for hh in range(hp):
            st_ref[slot, hh, :, lo:] = _dot(kaug_ref[hh, pl.ds(start, tk), :], qts[hh][:, lo:])

    def softmax_pv(j, slot, carry, diag):
        start = pl.multiple_of(j * tk, tk)
        lo = 0 if diag is None else diag * tk
        out = []
        for hh in range(hp):
            m0, l0, acc0 = carry[hh]
            m, l, acc = m0[:, lo:], l0[:, lo:], acc0[:, lo:]
            st = st_ref[slot, hh, :, lo:]
            if diag is not None:
                key = lax.broadcasted_iota(jnp.int32, st.shape, 0)
                qry = lax.broadcasted_iota(jnp.int32, st.shape, 1)
                st = jnp.where(key <= qry, st, -jnp.inf)
            m_new = jnp.maximum(m, jnp.max(st, axis=0, keepdims=True))
            alpha = jnp.exp2(m - m_new)
            pt = jnp.exp2(st - m_new)
            l = alpha * l + jnp.sum(pt, axis=0, keepdims=True)
            acc = alpha * acc + _dot(vt_ref[hh, :, pl.ds(start, tk)], pt.astype(BF16))
            if lo:
                m_new, l, acc = (jnp.concatenate([old[:, :lo], new], axis=1)
                                 for old, new in ((m0, m_new), (l0, l), (acc0, acc)))
            out.append((m_new, l, acc))
        return tuple(out)

    def pair(jj, carry):
        a = 2 * jj
        scores(a + 1, 1)
        carry = softmax_pv(a, 0, carry, None)
        scores(a + 2, 0)
        return softmax_pv(a + 1, 1, carry, None)

    init = tuple((jnp.full((1, tq), -jnp.inf, F32), jnp.zeros((1, tq), F32),
                  jnp.zeros((HEAD_DIM, tq), F32)) for _ in range(hp))
    scores(0, 0)
    carry = lax.fori_loop(0, i, pair, init)
    scores(2 * i + 1, 1, lo=tk)
    carry = softmax_pv(2 * i, 0, carry, 0)
    carry = softmax_pv(2 * i + 1, 1, carry, 1)
    for hh in range(hp):
        _, l, acc = carry[hh]
        o_ref[0, :, hh * HEAD_DIM:(hh + 1) * HEAD_DIM] = (acc / l).T.astype(BF16)


def _attn(q, k, v, fc, tq=1024, hp=2):
    b, s, d = q.shape
    w = hp * HEAD_DIM
    tk = tq // 2
    return pl.pallas_call(
        functools.partial(_attn_kernel, tq=tq, tk=tk, hp=hp),
        grid=(b, N_HEADS // hp, s // tq),
        in_specs=[pl.BlockSpec((1, tq, w), lambda bb, h, i: (bb, i, h)),
                  pl.BlockSpec((1, s, w), lambda bb, h, i: (bb, 0, h)),
                  pl.BlockSpec((1, s, w), lambda bb, h, i: (bb, 0, h)),
                  pl.BlockSpec((1, s, LANES), lambda bb, h, i: (bb, 0, 0))],
        out_specs=pl.BlockSpec((1, tq, w), lambda bb, h, i: (bb, i, h)),
        out_shape=jax.ShapeDtypeStruct((b, s, d), BF16),
        scratch_shapes=[pltpu.VMEM((hp, s, 2 * HEAD_DIM), BF16),
                        pltpu.VMEM((hp, HEAD_DIM, s), BF16),
                        pltpu.VMEM((2, hp, tk, tq), F32)],
        compiler_params=_params(("parallel", "parallel", "arbitrary")),
        name="attn",
    )(q, k, v, fc)


def _oproj_kernel(x_ref, a_ref, ada_ref, w_ref, o_ref):
    gate = ada_ref[0, 5:6, :]
    o_ref[...] = x_ref[...] + gate * _dot(a_ref[...], w_ref[...])


def _oproj(x, attn, ada, w_o, j, seq, tm=1024):
    t, d = x.shape
    per_batch = seq // tm
    return pl.pallas_call(
        _oproj_kernel,
        grid=(t // tm,),
        in_specs=[pl.BlockSpec((tm, d), lambda i: (i, 0)),
                  pl.BlockSpec((tm, d), lambda i: (i, 0)),
                  pl.BlockSpec((1, 9, d), lambda i: (i // per_batch, 0, 0)),
                  pl.BlockSpec((None, d, d), lambda i: (j, 0, 0), pipeline_mode=pl.Buffered(1))],
        out_specs=pl.BlockSpec((tm, d), lambda i: (i, 0)),
        out_shape=jax.ShapeDtypeStruct((t, d), F32),
        compiler_params=_params(("parallel",)),
        name="oproj",
    )(x, attn, ada, w_o)


def kernel(x, c, norm_g, w_ada, b_ada, w_ffn_in, w_ffn_out, w_conv_in, conv_w, conv_b, w_conv_out,
           kv_norm_g, w_ada_kv, b_ada_kv, w_kvf, b_fgate, w_q, w_o, final_g):
    b, s, d = x.shape
    depth = norm_g.shape[0]
    n_a = w_conv_in.shape[0]
    t = b * s
    assert w_q.shape[0] == 1 and depth == n_a + 1

    w_in_b = w_ffn_in[0, 0].astype(BF16)
    w_out_b = w_ffn_out[0, 0].astype(BF16)
    w_conv_in_b = w_conv_in.astype(BF16)
    w_conv_out_b = w_conv_out.astype(BF16)
    w_kv_b = w_kvf.astype(BF16)
    w_f_b = jnp.pad(w_kvf[:, 2 * d:], ((0, 0), (0, LANES - N_HEADS))).astype(BF16)
    b_f = jnp.pad(b_fgate, (0, LANES - N_HEADS)).reshape(1, LANES)
    w_q_b = w_q.astype(BF16)
    w_o_b = w_o.astype(BF16)
    conv_b3 = conv_b.reshape(n_a, 1, d)

    c_pad = jnp.pad(c, ((0, SUBLANES - b), (0, 0)))
    xf = x.reshape(t, d)
    for l in range(depth):
        ada = _ada(c_pad, w_ada, b_ada, l)[:b].reshape(b, 9, d)
        xf, w_in_b, w_out_b = _ffn(xf, ada, norm_g[l, 0], w_in_b, w_out_b, 0, s,
                                   cast_next=(w_ffn_in, w_ffn_out, l, 1))
        if l < n_a:
            xf = _conv(xf, ada, norm_g[l, 1], w_conv_in_b, conv_w, conv_b3, w_conv_out_b, l, s)
        else:
            j = l - n_a
            ada_kv = _ada(c_pad, w_ada_kv[None], b_ada_kv[None], 0)[:b].reshape(b, 2, d)
            q, k, v, fc = _qkv(xf, ada, ada_kv, norm_g[l, 1], kv_norm_g, w_q_b, w_kv_b, w_f_b, b_f, j, s)
            attn = _attn(q.reshape(b, s, d), k.reshape(b, s, d), v.reshape(b, s, d),
                         fc.reshape(b, s, LANES))
            xf = _oproj(xf, attn.reshape(t, d), ada, w_o_b, j, s)
        if l == depth - 1:
            xf = _ffn(xf, ada, norm_g[l, 2], w_in_b, w_out_b, 2, s, final_g=final_g)
        else:
            xf, w_in_b, w_out_b = _ffn(xf, ada, norm_g[l, 2], w_in_b, w_out_b, 2, s,
                                       cast_next=(w_ffn_in, w_ffn_out, l + 1, 0))
    return xf.reshape(b, s, d)
```

```python
import functools
import math

import jax
import jax.numpy as jnp
from jax import lax
from jax.experimental import pallas as pl
from jax.experimental.pallas import tpu as pltpu

EPS = 1e-6
N_HEADS = 16
HEAD_DIM = 128
CONV_WIDTH = 3
LOG2E = math.log2(math.e)
LANES = 128
SUBLANES = 8
MXU_DIM = 256
VMEM_LIMIT_BYTES = 59 * 1024 * 1024

F32 = jnp.float32
BF16 = jnp.bfloat16


def _params(semantics):
    return pltpu.CompilerParams(dimension_semantics=semantics,
                                vmem_limit_bytes=VMEM_LIMIT_BYTES)


def _dot(a, b):
    return jnp.dot(a, b, preferred_element_type=F32)


def _rmsnorm(x, g):
    return x * lax.rsqrt(jnp.mean(x * x, axis=-1, keepdims=True) + EPS) * g


NORM_ROWS = 16
NORM_UNROLL = 8
def _modulated_norm_to(x_ref, *targets):
    params = [(h_ref, g_ref[...], shift, 1.0 + scale) for h_ref, g_ref, shift, scale in targets]

    def body(r, _):
        rows = pl.ds(pl.multiple_of(r * NORM_ROWS, NORM_ROWS), NORM_ROWS)
        x = x_ref[rows, :]
        xhat = x * lax.rsqrt(jnp.mean(x * x, axis=-1, keepdims=True) + EPS)
        for h_ref, g, shift, scale1 in params:
            h_ref[rows, :] = (xhat * g * scale1 + shift).astype(BF16)
        return 0
    lax.fori_loop(0, x_ref.shape[0] // NORM_ROWS, body, 0, unroll=NORM_UNROLL)


def _residual_to(o_ref, x_ref, coef, final_g=None, rs_ref=None):
    if final_g is None:
        o_ref[...] = x_ref[...] + coef * o_ref[...]
        return
    n = x_ref.shape[0] // NORM_ROWS

    def rows_of(r):
        return pl.ds(pl.multiple_of(r * NORM_ROWS, NORM_ROWS), NORM_ROWS)

    def stats(r, _):
        y = x_ref[rows_of(r), :] + coef * o_ref[rows_of(r), :]
        rs = lax.rsqrt(jnp.mean(y * y, axis=-1, keepdims=True) + EPS)
        rs_ref[rows_of(r), :] = jnp.broadcast_to(rs, (NORM_ROWS, LANES))
        return 0
    lax.fori_loop(0, n, stats, 0, unroll=NORM_UNROLL)

    def scale(r, _):
        y = x_ref[rows_of(r), :] + coef * o_ref[rows_of(r), :]
        o_ref[rows_of(r), :] = y * rs_ref[rows_of(r), 0:1] * final_g
        return 0
    lax.fori_loop(0, n, scale, 0, unroll=NORM_UNROLL)


def _ada_rows(ada_ref, sub):
    return tuple(ada_ref[0, 3 * sub + k:3 * sub + k + 1, :] for k in range(3))


def _ada_kernel(c_ref, w_ref, b_ref, o_ref):
    c = c_ref[...]
    cond = c * jax.nn.sigmoid(c)
    o_ref[...] = _dot(cond.astype(BF16), w_ref[...].astype(BF16)) + b_ref[...]


def _ada(c_pad, w, b, l, tn=1024):
    nl, d, n = w.shape
    return pl.pallas_call(
        _ada_kernel,
        grid=(n // tn,),
        in_specs=[pl.BlockSpec((SUBLANES, d), lambda j: (0, 0)),
                  pl.BlockSpec((None, d, tn), lambda j: (l, 0, j)),
                  pl.BlockSpec((None, 1, tn), lambda j: (l, 0, j))],
        out_specs=pl.BlockSpec((SUBLANES, tn), lambda j: (0, j)),
        out_shape=jax.ShapeDtypeStruct((SUBLANES, n), F32),
        compiler_params=_params(("parallel",)),
        name="ada",
    )(c_pad, w, b.reshape(nl, 1, n))


def _row_cast_specs(w, lead, n_steps, step_of, row_tile=16):
    rows, cols = w.shape[-2:]
    n_tiles = rows // row_tile
    assert rows % row_tile == 0 and n_tiles <= n_steps

    def tile(*g):
        return jnp.minimum(step_of(*g), n_tiles - 1)
    in_spec = pl.BlockSpec((None,) * len(lead) + (row_tile, cols), lambda *g: tuple(lead) + (tile(*g), 0))
    out_spec = pl.BlockSpec((row_tile, cols), lambda *g: (tile(*g), 0))
    return in_spec, out_spec, jax.ShapeDtypeStruct((rows, cols), BF16)


def _cast_tiles(in_refs, out_refs):
    for src, dst in zip(in_refs, out_refs):
        dst[...] = src[...].astype(BF16)


def _ffn_kernel(x_ref, ada_ref, g_ref, wg_ref, wu_ref, wo_ref, *rest, sub, final, n_cast):
    rest = list(rest)
    fg_ref = rest.pop(0) if final else None
    cast_in = [rest.pop(0) for _ in range(n_cast)]
    o_ref = rest.pop(0)
    cast_out = [rest.pop(0) for _ in range(n_cast)]
    h_ref = rest.pop(0)
    rs_ref = rest.pop(0) if final else None
    f = pl.program_id(1)

    shift, scale, gate = _ada_rows(ada_ref, sub)

    @pl.when(f == 0)
    def _():
        _modulated_norm_to(x_ref, (h_ref, g_ref, shift, scale))
        o_ref[...] = jnp.zeros_like(o_ref)

    _cast_tiles(cast_in, cast_out)
    h = h_ref[...]
    a = _dot(h, wg_ref[...])
    b = _dot(h, wu_ref[...])
    act = (a * jax.nn.sigmoid(a) * b).astype(BF16)
    o_ref[...] += _dot(act, wo_ref[...])

    @pl.when(f == pl.num_programs(1) - 1)
    def _():
        if final:
            _residual_to(o_ref, x_ref, 0.5 * gate, fg_ref[...], rs_ref)
        else:
            _residual_to(o_ref, x_ref, 0.5 * gate)


def _ffn(x, ada, g, w_in, w_out, sub, seq, final_g=None, cast_next=None, casts=(), tf=512):
    t, d = x.shape
    ff = w_out.shape[0]
    nf = ff // tf
    final = final_g is not None
    tm = 1024
    per_batch = seq // tm
    ni = t // tm
    in_specs = [pl.BlockSpec((tm, d), lambda i, f: (i, 0)),
                pl.BlockSpec((1, 9, d), lambda i, f: (i // per_batch, 0, 0)),
                pl.BlockSpec((1, d), lambda i, f: (0, 0)),
                pl.BlockSpec((d, tf), lambda i, f: (0, f)),
                pl.BlockSpec((d, tf), lambda i, f: (0, nf + f)),
                pl.BlockSpec((tf, d), lambda i, f: (f, 0))]
    args = [x, ada, g.reshape(1, d), w_in, w_in, w_out]
    out_specs = [pl.BlockSpec((tm, d), lambda i, f: (i, 0))]
    out_shape = [jax.ShapeDtypeStruct((t, d), F32)]
    if final:
        in_specs.append(pl.BlockSpec((1, d), lambda i, f: (0, 0)))
        args.append(final_g.reshape(1, d))
    n_cast = 0
    if cast_next:
        w_in32, w_out32, l2, idx2 = cast_next
        in_tile = (d // ni, 2 * ff // nf)
        out_tile = (ff // (ni * nf), d)
        in_specs += [pl.BlockSpec((None, None) + in_tile, lambda i, f: (l2, idx2, i, f)),
                     pl.BlockSpec((None, None) + out_tile, lambda i, f: (l2, idx2, i * nf + f, 0))]
        args += [w_in32, w_out32]
        out_specs += [pl.BlockSpec(in_tile, lambda i, f: (i, f)),
                      pl.BlockSpec(out_tile, lambda i, f: (i * nf + f, 0))]
        out_shape += [jax.ShapeDtypeStruct((d, 2 * ff), BF16), jax.ShapeDtypeStruct((ff, d), BF16)]
        n_cast += 2
    for w, lead in casts:
        i_spec, o_spec, o_shape = _row_cast_specs(w, lead, ni * nf, lambda i, f: i * nf + f)
        in_specs.append(i_spec)
        args.append(w)
        out_specs.append(o_spec)
        out_shape.append(o_shape)
        n_cast += 1
    outs = pl.pallas_call(
        functools.partial(_ffn_kernel, sub=sub, final=final, n_cast=n_cast),
        grid=(ni, nf),
        in_specs=in_specs,
        out_specs=out_specs,
        out_shape=out_shape,
        scratch_shapes=[pltpu.VMEM((tm, d), BF16)] + ([pltpu.VMEM((tm, LANES), F32)] if final else []),
        compiler_params=_params(("parallel", "arbitrary")),
        name="ffn_final" if final else "ffn",
    )(*args)
    return outs if n_cast else outs[0]


def _conv_kernel(x_ref, ada_ref, g_ref, wb_ref, wc_ref, wx_ref, cw_ref, cb_ref, wo_ref,
                 *rest, per_batch, n_cast):
    cast_in, (o_ref, *cast_out), (h_ref, tail_ref) = (rest[:n_cast], rest[n_cast:2 * n_cast + 1],
                                                      rest[2 * n_cast + 1:])
    i = pl.program_id(0)
    c = pl.program_id(1)
    tm = x_ref.shape[0]

    shift, scale, gate = _ada_rows(ada_ref, 1)

    @pl.when(c == 0)
    def _():
        _modulated_norm_to(x_ref, (h_ref, g_ref, shift, scale))
        o_ref[...] = jnp.zeros_like(o_ref)

    @pl.when(i % per_batch == 0)
    def _():
        tail_ref[c] = jnp.zeros(tail_ref.shape[1:], F32)

    _cast_tiles(cast_in, cast_out)
    h = h_ref[...]
    bg = _dot(h, wb_ref[...])
    cg = _dot(h, wc_ref[...])
    xv = _dot(h, wx_ref[...])
    u = cg * xv
    tail = tail_ref[c]
    p1 = tail[SUBLANES - 1:SUBLANES, :]
    p2 = tail[SUBLANES - 2:SUBLANES - 1, :]
    row = lax.broadcasted_iota(jnp.int32, u.shape, 0)
    u1 = jnp.where(row == 0, p1, pltpu.roll(u, 1, 0))
    u2 = jnp.where(row == 0, p2, jnp.where(row == 1, p1, pltpu.roll(u, 2, 0)))
    conv = cw_ref[0:1, :] * u2 + cw_ref[1:2, :] * u1 + cw_ref[2:3, :] * u + cb_ref[...]
    tail_ref[c] = u[tm - SUBLANES:, :]
    o_ref[...] += _dot((bg * conv).astype(BF16), wo_ref[...])

    @pl.when(c == pl.num_programs(1) - 1)
    def _():
        _residual_to(o_ref, x_ref, gate)


def _conv(x, ada, g, w_in, conv_w, conv_b, w_out, l, seq, casts=(), tm=512, tc=512):
    t, d = x.shape
    nc = d // tc
    ni = t // tm
    per_batch = seq // tm
    in_specs = [pl.BlockSpec((tm, d), lambda i, c: (i, 0)),
                pl.BlockSpec((1, 9, d), lambda i, c: (i // per_batch, 0, 0)),
                pl.BlockSpec((1, d), lambda i, c: (0, 0)),
                pl.BlockSpec((d, tc), lambda i, c: (0, c)),
                pl.BlockSpec((d, tc), lambda i, c: (0, nc + c)),
                pl.BlockSpec((d, tc), lambda i, c: (0, 2 * nc + c)),
                pl.BlockSpec((None, CONV_WIDTH, tc), lambda i, c: (l, 0, c)),
                pl.BlockSpec((None, 1, tc), lambda i, c: (l, 0, c)),
                pl.BlockSpec((tc, d), lambda i, c: (c, 0))]
    args = [x, ada, g.reshape(1, d), w_in, w_in, w_in, conv_w, conv_b, w_out]
    out_specs = [pl.BlockSpec((tm, d), lambda i, c: (i, 0))]
    out_shape = [jax.ShapeDtypeStruct((t, d), F32)]
    for w, lead in casts:
        i_spec, o_spec, o_shape = _row_cast_specs(w, lead, ni * nc, lambda i, c: i * nc + c)
        in_specs.append(i_spec)
        args.append(w)
        out_specs.append(o_spec)
        out_shape.append(o_shape)
    outs = pl.pallas_call(
        functools.partial(_conv_kernel, per_batch=per_batch, n_cast=len(casts)),
        grid=(ni, nc),
        in_specs=in_specs,
        out_specs=out_specs,
        out_shape=out_shape,
        scratch_shapes=[pltpu.VMEM((tm, d), BF16),
                        pltpu.VMEM((nc, SUBLANES, tc), F32)],
        compiler_params=_params(("arbitrary", "arbitrary")),
        name="conv",
    )(*args)
    return outs if casts else outs[0]


def _qkv_kernel(x_ref, ada_ref, adakv_ref, g_ref, gkv_ref, wq_ref, wk_ref, wv_ref, wf_ref, bf_ref,
                q_ref, k_ref, v_ref, fc_ref, hq_ref, hkv_ref, carry_ref, *, per_batch, q_scale):
    i = pl.program_id(0)
    c = pl.program_id(1)
    tm = x_ref.shape[0]

    @pl.when(c == 0)
    def _():
        shift, scale, _ = _ada_rows(ada_ref, 1)
        _modulated_norm_to(x_ref, (hq_ref, g_ref, shift, scale),
                           (hkv_ref, gkv_ref, adakv_ref[0, 0:1, :], adakv_ref[0, 1:2, :]))
        hkv = hkv_ref[...]

        @pl.when(i % per_batch == 0)
        def _():
            carry_ref[...] = jnp.zeros_like(carry_ref)

        zf = _dot(hkv, wf_ref[...]) + bf_ref[...]
        ls = jnp.minimum(zf, 0.0) - jnp.log1p(jnp.exp(-jnp.abs(zf)))
        hi = ls.astype(BF16)
        r1 = ls - hi.astype(F32)
        mid = r1.astype(BF16)
        lo = (r1 - mid.astype(F32)).astype(BF16)
        rr = lax.broadcasted_iota(jnp.int32, (tm, tm), 0)
        cc = lax.broadcasted_iota(jnp.int32, (tm, tm), 1)
        tri = (rr >= cc).astype(BF16)
        parts = _dot(tri, jnp.concatenate([hi, mid, lo], axis=1))
        cum = (parts[:, :LANES] + parts[:, LANES:2 * LANES]) + parts[:, 2 * LANES:] + carry_ref[0:1, :]
        carry_ref[...] = jnp.broadcast_to(cum[tm - 1:tm, :], carry_ref.shape)
        fc_ref[...] = cum

    hkv = hkv_ref[...]
    q_ref[...] = (_dot(hq_ref[...], wq_ref[...]) * q_scale).astype(BF16)
    k_ref[...] = _dot(hkv, wk_ref[...]).astype(BF16)
    v_ref[...] = _dot(hkv, wv_ref[...]).astype(BF16)


def _qkv(x, ada, ada_kv, g, gkv, w_q, w_kv, w_f, b_f, j, seq, tm=1024, tn=512):
    t, d = x.shape
    nn = d // tn
    per_batch = seq // tm
    q_scale = LOG2E / math.sqrt(HEAD_DIM)
    row_blk = pl.BlockSpec((tm, tn), lambda i, c: (i, c))
    return pl.pallas_call(
        functools.partial(_qkv_kernel, per_batch=per_batch, q_scale=q_scale),
        grid=(t // tm, nn),
        in_specs=[pl.BlockSpec((tm, d), lambda i, c: (i, 0)),
                  pl.BlockSpec((1, 9, d), lambda i, c: (i // per_batch, 0, 0)),
                  pl.BlockSpec((1, 2, d), lambda i, c: (i // per_batch, 0, 0)),
                  pl.BlockSpec((1, d), lambda i, c: (0, 0)),
                  pl.BlockSpec((1, d), lambda i, c: (0, 0)),
                  pl.BlockSpec((None, d, tn), lambda i, c: (j, 0, c)),
                  pl.BlockSpec((d, tn), lambda i, c: (0, c)),
                  pl.BlockSpec((d, tn), lambda i, c: (0, nn + c)),
                  pl.BlockSpec((d, LANES), lambda i, c: (0, 0)),
                  pl.BlockSpec((1, LANES), lambda i, c: (0, 0))],
        out_specs=[row_blk, row_blk, row_blk,
                   pl.BlockSpec((tm, LANES), lambda i, c: (i, 0))],
        out_shape=[jax.ShapeDtypeStruct((t, d), BF16),
                   jax.ShapeDtypeStruct((t, d), BF16),
                   jax.ShapeDtypeStruct((t, d), BF16),
                   jax.ShapeDtypeStruct((t, LANES), F32)],
        scratch_shapes=[pltpu.VMEM((tm, d), BF16),
                        pltpu.VMEM((tm, d), BF16),
                        pltpu.VMEM((SUBLANES, LANES), F32)],
        compiler_params=_params(("arbitrary", "arbitrary")),
        name="qkv",
    )(x, ada, ada_kv, g.reshape(1, d), gkv.reshape(1, d), w_q, w_kv, w_kv, w_f, b_f)


def _split3(f):
    hi = f.astype(BF16)
    r1 = f - hi.astype(F32)
    mid = r1.astype(BF16)
    lo = (r1 - mid.astype(F32)).astype(BF16)
    return hi.astype(F32), mid.astype(F32), lo.astype(F32)


def _attn_kernel(q_ref, k_ref, v_ref, fc_ref, o_ref, kaug_ref, vt_ref, st_ref, *, tq, tk, hp):
    hg = pl.program_id(1)
    i = pl.program_id(2)
    seq = k_ref.shape[1]

    def head_col(blk, head):
        lane = lax.broadcasted_iota(jnp.int32, blk.shape, 1)
        return jnp.sum(jnp.where(lane == head, blk, 0.0), axis=1, keepdims=True) * LOG2E

    @pl.when(i == 0)
    def _():
        lane = lax.broadcasted_iota(jnp.int32, (tk, LANES), 1)

        def build(r, _):
            start = pl.multiple_of(r * tk, tk)
            blk = fc_ref[0, pl.ds(start, tk), :]
            for hh in range(hp):
                hi, mid, lo = _split3(head_col(blk, hg * hp + hh))
                aug = jnp.where(lane < 3, 1.0,
                                jnp.where(lane == 3, -hi,
                                          jnp.where(lane == 4, -mid, jnp.where(lane == 5, -lo, 0.0))))
                kaug_ref[hh, pl.ds(start, tk), :HEAD_DIM] = k_ref[0, pl.ds(start, tk),
                                                                  hh * HEAD_DIM:(hh + 1) * HEAD_DIM]
                kaug_ref[hh, pl.ds(start, tk), HEAD_DIM:] = aug.astype(BF16)
                vblk = v_ref[0, pl.ds(start, tk), hh * HEAD_DIM:(hh + 1) * HEAD_DIM]
                vt_ref[hh, :, pl.ds(start, tk)] = vblk.astype(F32).T.astype(BF16)
            return 0
        lax.fori_loop(0, seq // tk, build, 0)

    fblk = fc_ref[0, pl.ds(pl.multiple_of(i * tq, tq), tq), :]
    lane = lax.broadcasted_iota(jnp.int32, (tq, LANES), 1)
    qts = []
    for hh in range(hp):
        hi, mid, lo = _split3(head_col(fblk, hg * hp + hh))
        aq = jnp.where(lane == 0, hi,
                       jnp.where(lane == 1, mid,
                                 jnp.where(lane == 2, lo, jnp.where(lane < 6, 1.0, 0.0))))
        qh = q_ref[0, :, hh * HEAD_DIM:(hh + 1) * HEAD_DIM].astype(F32)
        qts.append(jnp.concatenate([qh.T, aq.T], axis=0).astype(BF16))

    def scores(j, slot, lo=0):
        start = pl.multiple_of(j * tk, tk)
        for hh in range(hp):
            st_ref[slot, hh, :, lo:] = _dot(kaug_ref[hh, pl.ds(start, tk), :], qts[hh][:, lo:])

    def softmax_pv(j, slot, carry, diag):
        start = pl.multiple_of(j * tk, tk)
        lo = 0 if diag is None else diag * tk
        out = []
        for hh in range(hp):
            m0, l0, acc0 = carry[hh]
            m, l, acc = m0[:, lo:], l0[:, lo:], acc0[:, lo:]
            st = st_ref[slot, hh, :, lo:]
            if diag is not None:
                key = lax.broadcasted_iota(jnp.int32, st.shape, 0)
                qry = lax.broadcasted_iota(jnp.int32, st.shape, 1)
                st = jnp.where(key <= qry, st, -jnp.inf)
            m_new = jnp.maximum(m, jnp.max(st, axis=0, keepdims=True))
            alpha = jnp.exp2(m - m_new)
            pt = jnp.exp2(st - m_new)
            l = alpha * l + jnp.sum(pt, axis=0, keepdims=True)
            acc = alpha * acc + _dot(vt_ref[hh, :, pl.ds(start, tk)], pt.astype(BF16))
            if lo:
                m_new, l, acc = (jnp.concatenate([old[:, :lo], new], axis=1)
                                 for old, new in ((m0, m_new), (l0, l), (acc0, acc)))
            out.append((m_new, l, acc))
        return tuple(out)

    def pair(jj, carry):
        a = 2 * jj
        scores(a + 1, 1)
        carry = softmax_pv(a, 0, carry, None)
        scores(a + 2, 0)
        return softmax_pv(a + 1, 1, carry, None)

    init = tuple((jnp.full((1, tq), -jnp.inf, F32), jnp.zeros((1, tq), F32),
                  jnp.zeros((HEAD_DIM, tq), F32)) for _ in range(hp))
    scores(0, 0)
    carry = lax.fori_loop(0, i, pair, init)
    scores(2 * i + 1, 1, lo=tk)
    carry = softmax_pv(2 * i, 0, carry, 0)
    carry = softmax_pv(2 * i + 1, 1, carry, 1)
    for hh in range(hp):
        _, l, acc = carry[hh]
        o_ref[0, :, hh * HEAD_DIM:(hh + 1) * HEAD_DIM] = (acc / l).T.astype(BF16)


def _attn(q, k, v, fc, tq=1024, hp=2):
    b, s, d = q.shape
    w = hp * HEAD_DIM
    tk = tq // 2
    return pl.pallas_call(
        functools.partial(_attn_kernel, tq=tq, tk=tk, hp=hp),
        grid=(b, N_HEADS // hp, s // tq),
        in_specs=[pl.BlockSpec((1, tq, w), lambda bb, h, i: (bb, i, h)),
                  pl.BlockSpec((1, s, w), lambda bb, h, i: (bb, 0, h)),
                  pl.BlockSpec((1, s, w), lambda bb, h, i: (bb, 0, h)),
                  pl.BlockSpec((1, s, LANES), lambda bb, h, i: (bb, 0, 0))],
        out_specs=pl.BlockSpec((1, tq, w), lambda bb, h, i: (bb, i, h)),
        out_shape=jax.ShapeDtypeStruct((b, s, d), BF16),
        scratch_shapes=[pltpu.VMEM((hp, s, 2 * HEAD_DIM), BF16),
                        pltpu.VMEM((hp, HEAD_DIM, s), BF16),
                        pltpu.VMEM((2, hp, tk, tq), F32)],
        compiler_params=_params(("parallel", "parallel", "arbitrary")),
        name="attn",
    )(q, k, v, fc)


def _oproj_kernel(x_ref, a_ref, ada_ref, w_ref, o_ref):
    gate = ada_ref[0, 5:6, :]
    o_ref[...] = x_ref[...] + gate * _dot(a_ref[...], w_ref[...])


def _oproj(x, attn, ada, w_o, j, seq, tm=1024):
    t, d = x.shape
    per_batch = seq // tm
    return pl.pallas_call(
        _oproj_kernel,
        grid=(t // tm,),
        in_specs=[pl.BlockSpec((tm, d), lambda i: (i, 0)),
                  pl.BlockSpec((tm, d), lambda i: (i, 0)),
                  pl.BlockSpec((1, 9, d), lambda i: (i // per_batch, 0, 0)),
                  pl.BlockSpec((None, d, d), lambda i: (j, 0, 0), pipeline_mode=pl.Buffered(1))],
        out_specs=pl.BlockSpec((tm, d), lambda i: (i, 0)),
        out_shape=jax.ShapeDtypeStruct((t, d), F32),
        compiler_params=_params(("parallel",)),
        name="oproj",
    )(x, attn, ada, w_o)


def kernel(x, c, norm_g, w_ada, b_ada, w_ffn_in, w_ffn_out, w_conv_in, conv_w, conv_b, w_conv_out,
           kv_norm_g, w_ada_kv, b_ada_kv, w_kvf, b_fgate, w_q, w_o, final_g):
    b, s, d = x.shape
    depth = norm_g.shape[0]
    n_a = w_conv_in.shape[0]
    t = b * s
    assert w_q.shape[0] == 1 and depth == n_a + 1

    w_in_b = w_ffn_in[0, 0].astype(BF16)
    w_out_b = w_ffn_out[0, 0].astype(BF16)
    w_f_b = jnp.pad(w_kvf[:, 2 * d:], ((0, 0), (0, LANES - N_HEADS))).astype(BF16)
    b_f = jnp.pad(b_fgate, (0, LANES - N_HEADS)).reshape(1, LANES)
    conv_b3 = conv_b.reshape(n_a, 1, d)
    attn_casts = [(w_q, (0,)), (w_o, (0,)), (w_kvf, ())]

    c_pad = jnp.pad(c, ((0, SUBLANES - b), (0, 0)))
    xf = x.reshape(t, d)
    for l in range(depth):
        ada = _ada(c_pad, w_ada, b_ada, l)[:b].reshape(b, 9, d)
        if l < n_a:
            mixer_casts = [(w_conv_in, (l,)), (w_conv_out, (l,))]
        else:
            mixer_casts = attn_casts if n_a == 0 else []
        xf, w_in_b, w_out_b, *mixer_w = _ffn(xf, ada, norm_g[l, 0], w_in_b, w_out_b, 0, s,
                                             cast_next=(w_ffn_in, w_ffn_out, l, 1), casts=mixer_casts)
        if l < n_a:
            w_conv_in_b, w_conv_out_b = mixer_w
            if l == n_a - 1:
                xf, *attn_w = _conv(xf, ada, norm_g[l, 1], w_conv_in_b, conv_w, conv_b3, w_conv_out_b, l, s,
                                    casts=attn_casts)
            else:
                xf = _conv(xf, ada, norm_g[l, 1], w_conv_in_b, conv_w, conv_b3, w_conv_out_b, l, s)
        else:
            if n_a == 0:
                attn_w = mixer_w
            w_q_b, w_o_b, w_kv_b = attn_w
            ada_kv = _ada(c_pad, w_ada_kv[None], b_ada_kv[None], 0)[:b].reshape(b, 2, d)
            q, k, v, fc = _qkv(xf, ada, ada_kv, norm_g[l, 1], kv_norm_g, w_q_b[None], w_kv_b, w_f_b, b_f, 0, s)
            attn = _attn(q.reshape(b, s, d), k.reshape(b, s, d), v.reshape(b, s, d),
                         fc.reshape(b, s, LANES))
            xf = _oproj(xf, attn.reshape(t, d), ada, w_o_b[None], 0, s)
        if l == depth - 1:
            xf = _ffn(xf, ada, norm_g[l, 2], w_in_b, w_out_b, 2, s, final_g=final_g)
        else:
            xf, w_in_b, w_out_b = _ffn(xf, ada, norm_g[l, 2], w_in_b, w_out_b, 2, s,
                                       cast_next=(w_ffn_in, w_ffn_out, l + 1, 0))
    return xf.reshape(b, s, d)
```

```python
import functools
import math

import jax
import jax.numpy as jnp
from jax import lax
from jax.experimental import pallas as pl
from jax.experimental.pallas import tpu as pltpu

EPS = 1e-6
N_HEADS = 16
HEAD_DIM = 128
CONV_WIDTH = 3
LOG2E = math.log2(math.e)
LANES = 128
SUBLANES = 8
MXU_DIM = 256
VMEM_LIMIT_BYTES = 59 * 1024 * 1024

F32 = jnp.float32
BF16 = jnp.bfloat16


def _params(semantics):
    return pltpu.CompilerParams(dimension_semantics=semantics,
                                vmem_limit_bytes=VMEM_LIMIT_BYTES)


def _dot(a, b):
    return jnp.dot(a, b, preferred_element_type=F32)


def _rmsnorm(x, g):
    return x * lax.rsqrt(jnp.mean(x * x, axis=-1, keepdims=True) + EPS) * g


NORM_ROWS = 16
NORM_UNROLL = 8
def _modulated_norm_to(x_ref, *targets):
    params = [(h_ref, g_ref[...], shift, 1.0 + scale) for h_ref, g_ref, shift, scale in targets]

    def body(r, _):
        rows = pl.ds(pl.multiple_of(r * NORM_ROWS, NORM_ROWS), NORM_ROWS)
        x = x_ref[rows, :]
        xhat = x * lax.rsqrt(jnp.mean(x * x, axis=-1, keepdims=True) + EPS)
        for h_ref, g, shift, scale1 in params:
            h_ref[rows, :] = (xhat * g * scale1 + shift).astype(BF16)
        return 0
    lax.fori_loop(0, x_ref.shape[0] // NORM_ROWS, body, 0, unroll=NORM_UNROLL)


def _residual_to(o_ref, x_ref, coef, final_g=None, rs_ref=None):
    if final_g is None:
        o_ref[...] = x_ref[...] + coef * o_ref[...]
        return
    n = x_ref.shape[0] // NORM_ROWS

    def rows_of(r):
        return pl.ds(pl.multiple_of(r * NORM_ROWS, NORM_ROWS), NORM_ROWS)

    def stats(r, _):
        y = x_ref[rows_of(r), :] + coef * o_ref[rows_of(r), :]
        rs = lax.rsqrt(jnp.mean(y * y, axis=-1, keepdims=True) + EPS)
        rs_ref[rows_of(r), :] = jnp.broadcast_to(rs, (NORM_ROWS, LANES))
        return 0
    lax.fori_loop(0, n, stats, 0, unroll=NORM_UNROLL)

    def scale(r, _):
        y = x_ref[rows_of(r), :] + coef * o_ref[rows_of(r), :]
        o_ref[rows_of(r), :] = y * rs_ref[rows_of(r), 0:1] * final_g
        return 0
    lax.fori_loop(0, n, scale, 0, unroll=NORM_UNROLL)


def _ada_rows(ada_ref, sub):
    return tuple(ada_ref[0, 3 * sub + k:3 * sub + k + 1, :] for k in range(3))


def _ada_kernel(c_ref, w_ref, b_ref, o_ref):
    c = c_ref[...]
    cond = c * jax.nn.sigmoid(c)
    o_ref[...] = _dot(cond.astype(BF16), w_ref[...].astype(BF16)) + b_ref[...]


def _ada(c_pad, w, b, l, tn=1024):
    nl, d, n = w.shape
    return pl.pallas_call(
        _ada_kernel,
        grid=(n // tn,),
        in_specs=[pl.BlockSpec((SUBLANES, d), lambda j: (0, 0)),
                  pl.BlockSpec((None, d, tn), lambda j: (l, 0, j)),
                  pl.BlockSpec((None, 1, tn), lambda j: (l, 0, j))],
        out_specs=pl.BlockSpec((SUBLANES, tn), lambda j: (0, j)),
        out_shape=jax.ShapeDtypeStruct((SUBLANES, n), F32),
        compiler_params=_params(("parallel",)),
        name="ada",
    )(c_pad, w, b.reshape(nl, 1, n))


def _row_cast_specs(w, lead, n_steps, step_of, row_tile=16):
    rows, cols = w.shape[-2:]
    cols -= cols % LANES
    n_tiles = rows // row_tile
    assert rows % row_tile == 0 and n_tiles <= n_steps

    def tile(*g):
        return jnp.minimum(step_of(*g), n_tiles - 1)
    in_spec = pl.BlockSpec((None,) * len(lead) + (row_tile, cols), lambda *g: tuple(lead) + (tile(*g), 0))
    out_spec = pl.BlockSpec((row_tile, cols), lambda *g: (tile(*g), 0))
    return in_spec, out_spec, jax.ShapeDtypeStruct((rows, cols), BF16)


def _cast_tiles(in_refs, out_refs):
    for src, dst in zip(in_refs, out_refs):
        dst[...] = src[...].astype(BF16)


def _ffn_kernel(x_ref, ada_ref, g_ref, wg_ref, wu_ref, wo_ref, *rest, sub, final, n_cast):
    rest = list(rest)
    fg_ref = rest.pop(0) if final else None
    cast_in = [rest.pop(0) for _ in range(n_cast)]
    o_ref = rest.pop(0)
    cast_out = [rest.pop(0) for _ in range(n_cast)]
    h_ref = rest.pop(0)
    rs_ref = rest.pop(0) if final else None
    f = pl.program_id(1)

    shift, scale, gate = _ada_rows(ada_ref, sub)

    @pl.when(f == 0)
    def _():
        _modulated_norm_to(x_ref, (h_ref, g_ref, shift, scale))
        o_ref[...] = jnp.zeros_like(o_ref)

    _cast_tiles(cast_in, cast_out)
    h = h_ref[...]
    a = _dot(h, wg_ref[...])
    b = _dot(h, wu_ref[...])
    act = (a * jax.nn.sigmoid(a) * b).astype(BF16)
    o_ref[...] += _dot(act, wo_ref[...])

    @pl.when(f == pl.num_programs(1) - 1)
    def _():
        if final:
            _residual_to(o_ref, x_ref, 0.5 * gate, fg_ref[...], rs_ref)
        else:
            _residual_to(o_ref, x_ref, 0.5 * gate)


def _ffn(x, ada, g, w_in, w_out, sub, seq, final_g=None, cast_next=None, casts=(), tf=512):
    t, d = x.shape
    ff = w_out.shape[0]
    nf = ff // tf
    final = final_g is not None
    tm = 1024
    per_batch = seq // tm
    ni = t // tm
    in_specs = [pl.BlockSpec((tm, d), lambda i, f: (i, 0)),
                pl.BlockSpec((1, 9, d), lambda i, f: (i // per_batch, 0, 0)),
                pl.BlockSpec((1, d), lambda i, f: (0, 0)),
                pl.BlockSpec((d, tf), lambda i, f: (0, f)),
                pl.BlockSpec((d, tf), lambda i, f: (0, nf + f)),
                pl.BlockSpec((tf, d), lambda i, f: (f, 0))]
    args = [x, ada, g.reshape(1, d), w_in, w_in, w_out]
    out_specs = [pl.BlockSpec((tm, d), lambda i, f: (i, 0))]
    out_shape = [jax.ShapeDtypeStruct((t, d), F32)]
    if final:
        in_specs.append(pl.BlockSpec((1, d), lambda i, f: (0, 0)))
        args.append(final_g.reshape(1, d))
    n_cast = 0
    if cast_next:
        w_in32, w_out32, l2, idx2 = cast_next
        in_tile = (d // ni, 2 * ff // nf)
        out_tile = (ff // (ni * nf), d)
        in_specs += [pl.BlockSpec((None, None) + in_tile, lambda i, f: (l2, idx2, i, f)),
                     pl.BlockSpec((None, None) + out_tile, lambda i, f: (l2, idx2, i * nf + f, 0))]
        args += [w_in32, w_out32]
        out_specs += [pl.BlockSpec(in_tile, lambda i, f: (i, f)),
                      pl.BlockSpec(out_tile, lambda i, f: (i * nf + f, 0))]
        out_shape += [jax.ShapeDtypeStruct((d, 2 * ff), BF16), jax.ShapeDtypeStruct((ff, d), BF16)]
        n_cast += 2
    for w, lead in casts:
        i_spec, o_spec, o_shape = _row_cast_specs(w, lead, ni * nf, lambda i, f: i * nf + f)
        in_specs.append(i_spec)
        args.append(w)
        out_specs.append(o_spec)
        out_shape.append(o_shape)
        n_cast += 1
    outs = pl.pallas_call(
        functools.partial(_ffn_kernel, sub=sub, final=final, n_cast=n_cast),
        grid=(ni, nf),
        in_specs=in_specs,
        out_specs=out_specs,
        out_shape=out_shape,
        scratch_shapes=[pltpu.VMEM((tm, d), BF16)] + ([pltpu.VMEM((tm, LANES), F32)] if final else []),
        compiler_params=_params(("parallel", "arbitrary")),
        name="ffn_final" if final else "ffn",
    )(*args)
    return outs if n_cast else outs[0]


def _conv_kernel(x_ref, ada_ref, g_ref, wb_ref, wc_ref, wx_ref, cw_ref, cb_ref, wo_ref,
                 *rest, per_batch, n_cast):
    cast_in, (o_ref, *cast_out), (h_ref, tail_ref) = (rest[:n_cast], rest[n_cast:2 * n_cast + 1],
                                                      rest[2 * n_cast + 1:])
    i = pl.program_id(0)
    c = pl.program_id(1)
    tm = x_ref.shape[0]

    shift, scale, gate = _ada_rows(ada_ref, 1)

    @pl.when(c == 0)
    def _():
        _modulated_norm_to(x_ref, (h_ref, g_ref, shift, scale))
        o_ref[...] = jnp.zeros_like(o_ref)

    @pl.when(i % per_batch == 0)
    def _():
        tail_ref[c] = jnp.zeros(tail_ref.shape[1:], F32)

    _cast_tiles(cast_in, cast_out)
    h = h_ref[...]
    bg = _dot(h, wb_ref[...])
    cg = _dot(h, wc_ref[...])
    xv = _dot(h, wx_ref[...])
    u = cg * xv
    tail = tail_ref[c]
    p1 = tail[SUBLANES - 1:SUBLANES, :]
    p2 = tail[SUBLANES - 2:SUBLANES - 1, :]
    row = lax.broadcasted_iota(jnp.int32, u.shape, 0)
    u1 = jnp.where(row == 0, p1, pltpu.roll(u, 1, 0))
    u2 = jnp.where(row == 0, p2, jnp.where(row == 1, p1, pltpu.roll(u, 2, 0)))
    conv = cw_ref[0:1, :] * u2 + cw_ref[1:2, :] * u1 + cw_ref[2:3, :] * u + cb_ref[...]
    tail_ref[c] = u[tm - SUBLANES:, :]
    o_ref[...] += _dot((bg * conv).astype(BF16), wo_ref[...])

    @pl.when(c == pl.num_programs(1) - 1)
    def _():
        _residual_to(o_ref, x_ref, gate)


def _conv(x, ada, g, w_in, conv_w, conv_b, w_out, l, seq, casts=(), tm=512, tc=512):
    t, d = x.shape
    nc = d // tc
    ni = t // tm
    per_batch = seq // tm
    in_specs = [pl.BlockSpec((tm, d), lambda i, c: (i, 0)),
                pl.BlockSpec((1, 9, d), lambda i, c: (i // per_batch, 0, 0)),
                pl.BlockSpec((1, d), lambda i, c: (0, 0)),
                pl.BlockSpec((d, tc), lambda i, c: (0, c)),
                pl.BlockSpec((d, tc), lambda i, c: (0, nc + c)),
                pl.BlockSpec((d, tc), lambda i, c: (0, 2 * nc + c)),
                pl.BlockSpec((None, CONV_WIDTH, tc), lambda i, c: (l, 0, c)),
                pl.BlockSpec((None, 1, tc), lambda i, c: (l, 0, c)),
                pl.BlockSpec((tc, d), lambda i, c: (c, 0))]
    args = [x, ada, g.reshape(1, d), w_in, w_in, w_in, conv_w, conv_b, w_out]
    out_specs = [pl.BlockSpec((tm, d), lambda i, c: (i, 0))]
    out_shape = [jax.ShapeDtypeStruct((t, d), F32)]
    for w, lead in casts:
        i_spec, o_spec, o_shape = _row_cast_specs(w, lead, ni * nc, lambda i, c: i * nc + c)
        in_specs.append(i_spec)
        args.append(w)
        out_specs.append(o_spec)
        out_shape.append(o_shape)
    outs = pl.pallas_call(
        functools.partial(_conv_kernel, per_batch=per_batch, n_cast=len(casts)),
        grid=(ni, nc),
        in_specs=in_specs,
        out_specs=out_specs,
        out_shape=out_shape,
        scratch_shapes=[pltpu.VMEM((tm, d), BF16),
                        pltpu.VMEM((nc, SUBLANES, tc), F32)],
        compiler_params=_params(("arbitrary", "arbitrary")),
        name="conv",
    )(*args)
    return outs if casts else outs[0]


def _qkv_kernel(x_ref, ada_ref, adakv_ref, g_ref, gkv_ref, wq_ref, wk_ref, wv_ref, wf_ref, bf_ref,
                q_ref, k_ref, v_ref, fc_ref, hq_ref, hkv_ref, carry_ref, *, per_batch, q_scale):
    i = pl.program_id(0)
    c = pl.program_id(1)
    tm = x_ref.shape[0]

    @pl.when(c == 0)
    def _():
        shift, scale, _ = _ada_rows(ada_ref, 1)
        _modulated_norm_to(x_ref, (hq_ref, g_ref, shift, scale),
                           (hkv_ref, gkv_ref, adakv_ref[0, 0:1, :], adakv_ref[0, 1:2, :]))
        hkv = hkv_ref[...]

        @pl.when(i % per_batch == 0)
        def _():
            carry_ref[...] = jnp.zeros_like(carry_ref)

        zf = _dot(hkv, wf_ref[...]) + bf_ref[...]
        ls = jnp.minimum(zf, 0.0) - jnp.log1p(jnp.exp(-jnp.abs(zf)))
        hi = ls.astype(BF16)
        r1 = ls - hi.astype(F32)
        mid = r1.astype(BF16)
        lo = (r1 - mid.astype(F32)).astype(BF16)
        rr = lax.broadcasted_iota(jnp.int32, (tm, tm), 0)
        cc = lax.broadcasted_iota(jnp.int32, (tm, tm), 1)
        tri = (rr >= cc).astype(BF16)
        parts = _dot(tri, jnp.concatenate([hi, mid, lo], axis=1))
        cum = (parts[:, :LANES] + parts[:, LANES:2 * LANES]) + parts[:, 2 * LANES:] + carry_ref[0:1, :]
        carry_ref[...] = jnp.broadcast_to(cum[tm - 1:tm, :], carry_ref.shape)
        fc_ref[...] = cum

    hkv = hkv_ref[...]
    q_ref[...] = (_dot(hq_ref[...], wq_ref[...]) * q_scale).astype(BF16)
    k_ref[...] = _dot(hkv, wk_ref[...]).astype(BF16)
    v_ref[...] = _dot(hkv, wv_ref[...]).astype(BF16)


def _qkv(x, ada, ada_kv, g, gkv, w_q, w_kv, w_f, b_f, j, seq, tm=1024, tn=512):
    t, d = x.shape
    nn = d // tn
    per_batch = seq // tm
    q_scale = LOG2E / math.sqrt(HEAD_DIM)
    row_blk = pl.BlockSpec((tm, tn), lambda i, c: (i, c))
    return pl.pallas_call(
        functools.partial(_qkv_kernel, per_batch=per_batch, q_scale=q_scale),
        grid=(t // tm, nn),
        in_specs=[pl.BlockSpec((tm, d), lambda i, c: (i, 0)),
                  pl.BlockSpec((1, 9, d), lambda i, c: (i // per_batch, 0, 0)),
                  pl.BlockSpec((1, 2, d), lambda i, c: (i // per_batch, 0, 0)),
                  pl.BlockSpec((1, d), lambda i, c: (0, 0)),
                  pl.BlockSpec((1, d), lambda i, c: (0, 0)),
                  pl.BlockSpec((None, d, tn), lambda i, c: (j, 0, c)),
                  pl.BlockSpec((d, tn), lambda i, c: (0, c)),
                  pl.BlockSpec((d, tn), lambda i, c: (0, nn + c)),
                  pl.BlockSpec((d, LANES), lambda i, c: (0, 0)),
                  pl.BlockSpec((1, LANES), lambda i, c: (0, 0))],
        out_specs=[row_blk, row_blk, row_blk,
                   pl.BlockSpec((tm, LANES), lambda i, c: (i, 0))],
        out_shape=[jax.ShapeDtypeStruct((t, d), BF16),
                   jax.ShapeDtypeStruct((t, d), BF16),
                   jax.ShapeDtypeStruct((t, d), BF16),
                   jax.ShapeDtypeStruct((t, LANES), F32)],
        scratch_shapes=[pltpu.VMEM((tm, d), BF16),
                        pltpu.VMEM((tm, d), BF16),
                        pltpu.VMEM((SUBLANES, LANES), F32)],
        compiler_params=_params(("arbitrary", "arbitrary")),
        name="qkv",
    )(x, ada, ada_kv, g.reshape(1, d), gkv.reshape(1, d), w_q, w_kv, w_kv, w_f, b_f)


def _split3(f):
    hi = f.astype(BF16)
    r1 = f - hi.astype(F32)
    mid = r1.astype(BF16)
    lo = (r1 - mid.astype(F32)).astype(BF16)
    return hi.astype(F32), mid.astype(F32), lo.astype(F32)


def _attn_kernel(q_ref, k_ref, v_ref, fc_ref, o_ref, kaug_ref, vt_ref, st_ref, *, tq, tk, hp):
    hg = pl.program_id(1)
    i = pl.program_id(2)
    seq = k_ref.shape[1]

    def head_col(blk, head):
        lane = lax.broadcasted_iota(jnp.int32, blk.shape, 1)
        return jnp.sum(jnp.where(lane == head, blk, 0.0), axis=1, keepdims=True) * LOG2E

    @pl.when(i == 0)
    def _():
        lane = lax.broadcasted_iota(jnp.int32, (tk, LANES), 1)

        def build(r, _):
            start = pl.multiple_of(r * tk, tk)
            blk = fc_ref[0, pl.ds(start, tk), :]
            for hh in range(hp):
                hi, mid, lo = _split3(head_col(blk, hg * hp + hh))
                aug = jnp.where(lane < 3, 1.0,
                                jnp.where(lane == 3, -hi,
                                          jnp.where(lane == 4, -mid, jnp.where(lane == 5, -lo, 0.0))))
                kaug_ref[hh, pl.ds(start, tk), :HEAD_DIM] = k_ref[0, pl.ds(start, tk),
                                                                  hh * HEAD_DIM:(hh + 1) * HEAD_DIM]
                kaug_ref[hh, pl.ds(start, tk), HEAD_DIM:] = aug.astype(BF16)
                vblk = v_ref[0, pl.ds(start, tk), hh * HEAD_DIM:(hh + 1) * HEAD_DIM]
                vt_ref[hh, :, pl.ds(start, tk)] = vblk.astype(F32).T.astype(BF16)
            return 0
        lax.fori_loop(0, seq // tk, build, 0)

    fblk = fc_ref[0, pl.ds(pl.multiple_of(i * tq, tq), tq), :]
    lane = lax.broadcasted_iota(jnp.int32, (tq, LANES), 1)
    qts = []
    for hh in range(hp):
        hi, mid, lo = _split3(head_col(fblk, hg * hp + hh))
        aq = jnp.where(lane == 0, hi,
                       jnp.where(lane == 1, mid,
                                 jnp.where(lane == 2, lo, jnp.where(lane < 6, 1.0, 0.0))))
        qh = q_ref[0, :, hh * HEAD_DIM:(hh + 1) * HEAD_DIM].astype(F32)
        qts.append(jnp.concatenate([qh.T, aq.T], axis=0).astype(BF16))

    def scores(j, slot, lo=0):
        start = pl.multiple_of(j * tk, tk)
        for hh in range(hp):
            st_ref[slot, hh, :, lo:] = _dot(kaug_ref[hh, pl.ds(start, tk), :], qts[hh][:, lo:])

    def softmax_pv(j, slot, carry, diag):
        start = pl.multiple_of(j * tk, tk)
        lo = 0 if diag is None else diag * tk
        out = []
        for hh in range(hp):
            m0, l0, acc0 = carry[hh]
            m, l, acc = m0[:, lo:], l0[:, lo:], acc0[:, lo:]
            st = st_ref[slot, hh, :, lo:]
            if diag is not None:
                key = lax.broadcasted_iota(jnp.int32, st.shape, 0)
                qry = lax.broadcasted_iota(jnp.int32, st.shape, 1)
                st = jnp.where(key <= qry, st, -jnp.inf)
            m_new = jnp.maximum(m, jnp.max(st, axis=0, keepdims=True))
            alpha = jnp.exp2(m - m_new)
            pt = jnp.exp2(st - m_new)
            l = alpha * l + jnp.sum(pt, axis=0, keepdims=True)
            acc = alpha * acc + _dot(vt_ref[hh, :, pl.ds(start, tk)], pt.astype(BF16))
            if lo:
                m_new, l, acc = (jnp.concatenate([old[:, :lo], new], axis=1)
                                 for old, new in ((m0, m_new), (l0, l), (acc0, acc)))
            out.append((m_new, l, acc))
        return tuple(out)

    def pair(jj, carry):
        a = 2 * jj
        scores(a + 1, 1)
        carry = softmax_pv(a, 0, carry, None)
        scores(a + 2, 0)
        return softmax_pv(a + 1, 1, carry, None)

    init = tuple((jnp.full((1, tq), -jnp.inf, F32), jnp.zeros((1, tq), F32),
                  jnp.zeros((HEAD_DIM, tq), F32)) for _ in range(hp))
    scores(0, 0)
    carry = lax.fori_loop(0, i, pair, init)
    scores(2 * i + 1, 1, lo=tk)
    carry = softmax_pv(2 * i, 0, carry, 0)
    carry = softmax_pv(2 * i + 1, 1, carry, 1)
    for hh in range(hp):
        _, l, acc = carry[hh]
        o_ref[0, :, hh * HEAD_DIM:(hh + 1) * HEAD_DIM] = (acc / l).T.astype(BF16)


def _attn(q, k, v, fc, tq=1024, hp=2):
    b, s, d = q.shape
    w = hp * HEAD_DIM
    tk = tq // 2
    return pl.pallas_call(
        functools.partial(_attn_kernel, tq=tq, tk=tk, hp=hp),
        grid=(b, N_HEADS // hp, s // tq),
        in_specs=[pl.BlockSpec((1, tq, w), lambda bb, h, i: (bb, i, h)),
                  pl.BlockSpec((1, s, w), lambda bb, h, i: (bb, 0, h)),
                  pl.BlockSpec((1, s, w), lambda bb, h, i: (bb, 0, h)),
                  pl.BlockSpec((1, s, LANES), lambda bb, h, i: (bb, 0, 0))],
        out_specs=pl.BlockSpec((1, tq, w), lambda bb, h, i: (bb, i, h)),
        out_shape=jax.ShapeDtypeStruct((b, s, d), BF16),
        scratch_shapes=[pltpu.VMEM((hp, s, 2 * HEAD_DIM), BF16),
                        pltpu.VMEM((hp, HEAD_DIM, s), BF16),
                        pltpu.VMEM((2, hp, tk, tq), F32)],
        compiler_params=_params(("parallel", "parallel", "arbitrary")),
        name="attn",
    )(q, k, v, fc)


def _oproj_kernel(x_ref, a_ref, ada_ref, w_ref, o_ref):
    gate = ada_ref[0, 5:6, :]
    o_ref[...] = x_ref[...] + gate * _dot(a_ref[...], w_ref[...])


def _oproj(x, attn, ada, w_o, j, seq, tm=1024):
    t, d = x.shape
    per_batch = seq // tm
    return pl.pallas_call(
        _oproj_kernel,
        grid=(t // tm,),
        in_specs=[pl.BlockSpec((tm, d), lambda i: (i, 0)),
                  pl.BlockSpec((tm, d), lambda i: (i, 0)),
                  pl.BlockSpec((1, 9, d), lambda i: (i // per_batch, 0, 0)),
                  pl.BlockSpec((None, d, d), lambda i: (j, 0, 0), pipeline_mode=pl.Buffered(1))],
        out_specs=pl.BlockSpec((tm, d), lambda i: (i, 0)),
        out_shape=jax.ShapeDtypeStruct((t, d), F32),
        compiler_params=_params(("parallel",)),
        name="oproj",
    )(x, attn, ada, w_o)


def kernel(x, c, norm_g, w_ada, b_ada, w_ffn_in, w_ffn_out, w_conv_in, conv_w, conv_b, w_conv_out,
           kv_norm_g, w_ada_kv, b_ada_kv, w_kvf, b_fgate, w_q, w_o, final_g):
    b, s, d = x.shape
    depth = norm_g.shape[0]
    n_a = w_conv_in.shape[0]
    t = b * s
    assert w_q.shape[0] == 1 and depth == n_a + 1

    w_in_b = w_ffn_in[0, 0].astype(BF16)
    w_out_b = w_ffn_out[0, 0].astype(BF16)
    w_f_b = jnp.pad(w_kvf[:, 2 * d:], ((0, 0), (0, LANES - N_HEADS))).astype(BF16)
    b_f = jnp.pad(b_fgate, (0, LANES - N_HEADS)).reshape(1, LANES)
    conv_b3 = conv_b.reshape(n_a, 1, d)
    attn_casts = [(w_q, (0,)), (w_o, (0,)), (w_kvf, ())]

    c_pad = jnp.pad(c, ((0, SUBLANES - b), (0, 0)))
    xf = x.reshape(t, d)
    for l in range(depth):
        ada = _ada(c_pad, w_ada, b_ada, l)[:b].reshape(b, 9, d)
        if l < n_a:
            mixer_casts = [(w_conv_in, (l,)), (w_conv_out, (l,))]
        else:
            mixer_casts = attn_casts if n_a == 0 else []
        xf, w_in_b, w_out_b, *mixer_w = _ffn(xf, ada, norm_g[l, 0], w_in_b, w_out_b, 0, s,
                                             cast_next=(w_ffn_in, w_ffn_out, l, 1), casts=mixer_casts)
        if l < n_a:
            w_conv_in_b, w_conv_out_b = mixer_w
            if l == n_a - 1:
                xf, *attn_w = _conv(xf, ada, norm_g[l, 1], w_conv_in_b, conv_w, conv_b3, w_conv_out_b, l, s,
                                    casts=attn_casts)
            else:
                xf = _conv(xf, ada, norm_g[l, 1], w_conv_in_b, conv_w, conv_b3, w_conv_out_b, l, s)
        else:
            if n_a == 0:
                attn_w = mixer_w
            w_q_b, w_o_b, w_kv_b = attn_w
            ada_kv = _ada(c_pad, w_ada_kv[None], b_ada_kv[None], 0)[:b].reshape(b, 2, d)
            q, k, v, fc = _qkv(xf, ada, ada_kv, norm_g[l, 1], kv_norm_g, w_q_b[None], w_kv_b, w_f_b, b_f, 0, s)
            attn = _attn(q.reshape(b, s, d), k.reshape(b, s, d), v.reshape(b, s, d),
                         fc.reshape(b, s, LANES))
            xf = _oproj(xf, attn.reshape(t, d), ada, w_o_b[None], 0, s)
        if l == depth - 1:
            xf = _ffn(xf, ada, norm_g[l, 2], w_in_b, w_out_b, 2, s, final_g=final_g)
        else:
            xf, w_in_b, w_out_b = _ffn(xf, ada, norm_g[l, 2], w_in_b, w_out_b, 2, s,
                                       cast_next=(w_ffn_in, w_ffn_out, l + 1, 0))
    return xf.reshape(b, s, d)
```

```python
import functools
import math

import jax
import jax.numpy as jnp
from jax import lax
from jax.experimental import pallas as pl
from jax.experimental.pallas import tpu as pltpu

EPS = 1e-6
N_HEADS = 16
HEAD_DIM = 128
CONV_WIDTH = 3
LOG2E = math.log2(math.e)
LANES = 128
SUBLANES = 8
MXU_DIM = 256
VMEM_LIMIT_BYTES = 59 * 1024 * 1024

F32 = jnp.float32
BF16 = jnp.bfloat16


def _params(semantics):
    return pltpu.CompilerParams(dimension_semantics=semantics,
                                vmem_limit_bytes=VMEM_LIMIT_BYTES)


def _dot(a, b):
    return jnp.dot(a, b, preferred_element_type=F32)


def _rmsnorm(x, g):
    return x * lax.rsqrt(jnp.mean(x * x, axis=-1, keepdims=True) + EPS) * g


NORM_ROWS = 16
NORM_UNROLL = 8
def _modulated_norm_to(x_ref, *targets):
    params = [(h_ref, g_ref[...], shift, 1.0 + scale) for h_ref, g_ref, shift, scale in targets]

    def body(r, _):
        rows = pl.ds(pl.multiple_of(r * NORM_ROWS, NORM_ROWS), NORM_ROWS)
        x = x_ref[rows, :]
        xhat = x * lax.rsqrt(jnp.mean(x * x, axis=-1, keepdims=True) + EPS)
        for h_ref, g, shift, scale1 in params:
            h_ref[rows, :] = (xhat * g * scale1 + shift).astype(BF16)
        return 0
    lax.fori_loop(0, x_ref.shape[0] // NORM_ROWS, body, 0, unroll=NORM_UNROLL)


def _residual_to(o_ref, x_ref, coef, final_g=None, rs_ref=None):
    if final_g is None:
        o_ref[...] = x_ref[...] + coef * o_ref[...]
        return
    n = x_ref.shape[0] // NORM_ROWS

    def rows_of(r):
        return pl.ds(pl.multiple_of(r * NORM_ROWS, NORM_ROWS), NORM_ROWS)

    def stats(r, _):
        y = x_ref[rows_of(r), :] + coef * o_ref[rows_of(r), :]
        rs = lax.rsqrt(jnp.mean(y * y, axis=-1, keepdims=True) + EPS)
        rs_ref[rows_of(r), :] = jnp.broadcast_to(rs, (NORM_ROWS, LANES))
        return 0
    lax.fori_loop(0, n, stats, 0, unroll=NORM_UNROLL)

    def scale(r, _):
        y = x_ref[rows_of(r), :] + coef * o_ref[rows_of(r), :]
        o_ref[rows_of(r), :] = y * rs_ref[rows_of(r), 0:1] * final_g
        return 0
    lax.fori_loop(0, n, scale, 0, unroll=NORM_UNROLL)


def _ada_rows(ada_ref, sub):
    return tuple(ada_ref[0, 3 * sub + k:3 * sub + k + 1, :] for k in range(3))


def _ada_kernel(c_ref, w_ref, b_ref, o_ref):
    c = c_ref[...]
    cond = c * jax.nn.sigmoid(c)
    o_ref[...] = _dot(cond.astype(BF16), w_ref[...].astype(BF16)) + b_ref[...]


def _ada(c_pad, w, b, l, tn=1024):
    nl, d, n = w.shape
    return pl.pallas_call(
        _ada_kernel,
        grid=(n // tn,),
        in_specs=[pl.BlockSpec((SUBLANES, d), lambda j: (0, 0)),
                  pl.BlockSpec((None, d, tn), lambda j: (l, 0, j)),
                  pl.BlockSpec((None, 1, tn), lambda j: (l, 0, j))],
        out_specs=pl.BlockSpec((SUBLANES, tn), lambda j: (0, j)),
        out_shape=jax.ShapeDtypeStruct((SUBLANES, n), F32),
        compiler_params=_params(("parallel",)),
        name="ada",
    )(c_pad, w, b.reshape(nl, 1, n))


def _row_cast_specs(w, lead, n_steps, step_of, row_tile=16):
    rows, cols = w.shape[-2:]
    cols -= cols % LANES
    n_tiles = rows // row_tile
    assert rows % row_tile == 0 and n_tiles <= n_steps

    def tile(*g):
        return jnp.minimum(step_of(*g), n_tiles - 1)
    in_spec = pl.BlockSpec((None,) * len(lead) + (row_tile, cols), lambda *g: tuple(lead) + (tile(*g), 0))
    out_spec = pl.BlockSpec((row_tile, cols), lambda *g: (tile(*g), 0))
    return in_spec, out_spec, jax.ShapeDtypeStruct((rows, cols), BF16)


def _cast_tiles(in_refs, out_refs):
    for src, dst in zip(in_refs, out_refs):
        dst[...] = src[...].astype(BF16)


def _ffn_kernel(x_ref, ada_ref, g_ref, wg_ref, wu_ref, wo_ref, *rest, sub, final, n_cast):
    rest = list(rest)
    fg_ref = rest.pop(0) if final else None
    cast_in = [rest.pop(0) for _ in range(n_cast)]
    o_ref = rest.pop(0)
    cast_out = [rest.pop(0) for _ in range(n_cast)]
    h_ref = rest.pop(0)
    rs_ref = rest.pop(0) if final else None
    f = pl.program_id(1)

    shift, scale, gate = _ada_rows(ada_ref, sub)

    @pl.when(f == 0)
    def _():
        _modulated_norm_to(x_ref, (h_ref, g_ref, shift, scale))
        o_ref[...] = jnp.zeros_like(o_ref)

    _cast_tiles(cast_in, cast_out)
    h = h_ref[...]
    a = _dot(h, wg_ref[...])
    b = _dot(h, wu_ref[...])
    act = (a * jax.nn.sigmoid(a) * b).astype(BF16)
    o_ref[...] += _dot(act, wo_ref[...])

    @pl.when(f == pl.num_programs(1) - 1)
    def _():
        if final:
            _residual_to(o_ref, x_ref, 0.5 * gate, fg_ref[...], rs_ref)
        else:
            _residual_to(o_ref, x_ref, 0.5 * gate)


def _ffn(x, ada, g, w_in, w_out, sub, seq, final_g=None, cast_next=None, casts=(), tf=512):
    t, d = x.shape
    ff = w_out.shape[0]
    nf = ff // tf
    final = final_g is not None
    tm = 1024
    per_batch = seq // tm
    ni = t // tm
    in_specs = [pl.BlockSpec((tm, d), lambda i, f: (i, 0)),
                pl.BlockSpec((1, 9, d), lambda i, f: (i // per_batch, 0, 0)),
                pl.BlockSpec((1, d), lambda i, f: (0, 0)),
                pl.BlockSpec((d, tf), lambda i, f: (0, f)),
                pl.BlockSpec((d, tf), lambda i, f: (0, nf + f)),
                pl.BlockSpec((tf, d), lambda i, f: (f, 0))]
    args = [x, ada, g.reshape(1, d), w_in, w_in, w_out]
    out_specs = [pl.BlockSpec((tm, d), lambda i, f: (i, 0))]
    out_shape = [jax.ShapeDtypeStruct((t, d), F32)]
    if final:
        in_specs.append(pl.BlockSpec((1, d), lambda i, f: (0, 0)))
        args.append(final_g.reshape(1, d))
    n_cast = 0
    if cast_next:
        w_in32, w_out32, l2, idx2 = cast_next
        in_tile = (d // ni, 2 * ff // nf)
        out_tile = (ff // (ni * nf), d)
        in_specs += [pl.BlockSpec((None, None) + in_tile, lambda i, f: (l2, idx2, i, f)),
                     pl.BlockSpec((None, None) + out_tile, lambda i, f: (l2, idx2, i * nf + f, 0))]
        args += [w_in32, w_out32]
        out_specs += [pl.BlockSpec(in_tile, lambda i, f: (i, f)),
                      pl.BlockSpec(out_tile, lambda i, f: (i * nf + f, 0))]
        out_shape += [jax.ShapeDtypeStruct((d, 2 * ff), BF16), jax.ShapeDtypeStruct((ff, d), BF16)]
        n_cast += 2
    for w, lead in casts:
        i_spec, o_spec, o_shape = _row_cast_specs(w, lead, ni * nf, lambda i, f: i * nf + f)
        in_specs.append(i_spec)
        args.append(w)
        out_specs.append(o_spec)
        out_shape.append(o_shape)
        n_cast += 1
    outs = pl.pallas_call(
        functools.partial(_ffn_kernel, sub=sub, final=final, n_cast=n_cast),
        grid=(ni, nf),
        in_specs=in_specs,
        out_specs=out_specs,
        out_shape=out_shape,
        scratch_shapes=[pltpu.VMEM((tm, d), BF16)] + ([pltpu.VMEM((tm, LANES), F32)] if final else []),
        compiler_params=_params(("parallel", "arbitrary")),
        name="ffn_final" if final else "ffn",
    )(*args)
    return outs if n_cast else outs[0]


def _conv_kernel(x_ref, ada_ref, g_ref, wb_ref, wc_ref, wx_ref, cw_ref, cb_ref, wo_ref,
                 *rest, per_batch, n_cast):
    cast_in, (o_ref, *cast_out), (h_ref, tail_ref) = (rest[:n_cast], rest[n_cast:2 * n_cast + 1],
                                                      rest[2 * n_cast + 1:])
    i = pl.program_id(0)
    c = pl.program_id(1)
    tm = x_ref.shape[0]

    shift, scale, gate = _ada_rows(ada_ref, 1)

    @pl.when(c == 0)
    def _():
        _modulated_norm_to(x_ref, (h_ref, g_ref, shift, scale))
        o_ref[...] = jnp.zeros_like(o_ref)

    @pl.when(i % per_batch == 0)
    def _():
        tail_ref[c] = jnp.zeros(tail_ref.shape[1:], F32)

    _cast_tiles(cast_in, cast_out)
    half = tm // 2
    subs = [pl.ds(0, half), pl.ds(half, half)]
    proj = [tuple(_dot(h_ref[rows, :], w[...]) for w in (wb_ref, wc_ref, wx_ref)) for rows in subs]
    tail = tail_ref[c]
    row = lax.broadcasted_iota(jnp.int32, (half, tail.shape[1]), 0)
    for rows, (bg, cg, xv) in zip(subs, proj):
        u = cg * xv
        p1 = tail[SUBLANES - 1:SUBLANES, :]
        p2 = tail[SUBLANES - 2:SUBLANES - 1, :]
        u1 = jnp.where(row == 0, p1, pltpu.roll(u, 1, 0))
        u2 = jnp.where(row == 0, p2, jnp.where(row == 1, p1, pltpu.roll(u, 2, 0)))
        conv = cw_ref[0:1, :] * u2 + cw_ref[1:2, :] * u1 + cw_ref[2:3, :] * u + cb_ref[...]
        tail = u[half - SUBLANES:, :]
        o_ref[rows, :] += _dot((bg * conv).astype(BF16), wo_ref[...])
    tail_ref[c] = tail

    @pl.when(c == pl.num_programs(1) - 1)
    def _():
        _residual_to(o_ref, x_ref, gate)


def _conv(x, ada, g, w_in, conv_w, conv_b, w_out, l, seq, casts=(), tm=512, tc=512):
    t, d = x.shape
    nc = d // tc
    ni = t // tm
    per_batch = seq // tm
    in_specs = [pl.BlockSpec((tm, d), lambda i, c: (i, 0)),
                pl.BlockSpec((1, 9, d), lambda i, c: (i // per_batch, 0, 0)),
                pl.BlockSpec((1, d), lambda i, c: (0, 0)),
                pl.BlockSpec((d, tc), lambda i, c: (0, c)),
                pl.BlockSpec((d, tc), lambda i, c: (0, nc + c)),
                pl.BlockSpec((d, tc), lambda i, c: (0, 2 * nc + c)),
                pl.BlockSpec((None, CONV_WIDTH, tc), lambda i, c: (l, 0, c)),
                pl.BlockSpec((None, 1, tc), lambda i, c: (l, 0, c)),
                pl.BlockSpec((tc, d), lambda i, c: (c, 0))]
    args = [x, ada, g.reshape(1, d), w_in, w_in, w_in, conv_w, conv_b, w_out]
    out_specs = [pl.BlockSpec((tm, d), lambda i, c: (i, 0))]
    out_shape = [jax.ShapeDtypeStruct((t, d), F32)]
    for w, lead in casts:
        i_spec, o_spec, o_shape = _row_cast_specs(w, lead, ni * nc, lambda i, c: i * nc + c)
        in_specs.append(i_spec)
        args.append(w)
        out_specs.append(o_spec)
        out_shape.append(o_shape)
    outs = pl.pallas_call(
        functools.partial(_conv_kernel, per_batch=per_batch, n_cast=len(casts)),
        grid=(ni, nc),
        in_specs=in_specs,
        out_specs=out_specs,
        out_shape=out_shape,
        scratch_shapes=[pltpu.VMEM((tm, d), BF16),
                        pltpu.VMEM((nc, SUBLANES, tc), F32)],
        compiler_params=_params(("arbitrary", "arbitrary")),
        name="conv",
    )(*args)
    return outs if casts else outs[0]


def _qkv_kernel(x_ref, ada_ref, adakv_ref, g_ref, gkv_ref, wq_ref, wk_ref, wv_ref, wf_ref, bf_ref,
                q_ref, k_ref, v_ref, fc_ref, hq_ref, hkv_ref, carry_ref, *, per_batch, q_scale):
    i = pl.program_id(0)
    c = pl.program_id(1)
    tm = x_ref.shape[0]

    @pl.when(c == 0)
    def _():
        shift, scale, _ = _ada_rows(ada_ref, 1)
        _modulated_norm_to(x_ref, (hq_ref, g_ref, shift, scale),
                           (hkv_ref, gkv_ref, adakv_ref[0, 0:1, :], adakv_ref[0, 1:2, :]))
        hkv = hkv_ref[...]

        @pl.when(i % per_batch == 0)
        def _():
            carry_ref[...] = jnp.zeros_like(carry_ref)

        zf = _dot(hkv, wf_ref[...]) + bf_ref[...]
        ls = jnp.minimum(zf, 0.0) - jnp.log1p(jnp.exp(-jnp.abs(zf)))
        hi = ls.astype(BF16)
        r1 = ls - hi.astype(F32)
        mid = r1.astype(BF16)
        lo = (r1 - mid.astype(F32)).astype(BF16)
        rr = lax.broadcasted_iota(jnp.int32, (tm, tm), 0)
        cc = lax.broadcasted_iota(jnp.int32, (tm, tm), 1)
        tri = (rr >= cc).astype(BF16)
        parts = _dot(tri, jnp.concatenate([hi, mid, lo], axis=1))
        cum = (parts[:, :LANES] + parts[:, LANES:2 * LANES]) + parts[:, 2 * LANES:] + carry_ref[0:1, :]
        carry_ref[...] = jnp.broadcast_to(cum[tm - 1:tm, :], carry_ref.shape)
        fc_ref[...] = cum

    hkv = hkv_ref[...]
    q_ref[...] = (_dot(hq_ref[...], wq_ref[...]) * q_scale).astype(BF16)
    k_ref[...] = _dot(hkv, wk_ref[...]).astype(BF16)
    v_ref[...] = _dot(hkv, wv_ref[...]).astype(BF16)


def _qkv(x, ada, ada_kv, g, gkv, w_q, w_kv, w_f, b_f, j, seq, tm=1024, tn=512):
    t, d = x.shape
    nn = d // tn
    per_batch = seq // tm
    q_scale = LOG2E / math.sqrt(HEAD_DIM)
    row_blk = pl.BlockSpec((tm, tn), lambda i, c: (i, c))
    return pl.pallas_call(
        functools.partial(_qkv_kernel, per_batch=per_batch, q_scale=q_scale),
        grid=(t // tm, nn),
        in_specs=[pl.BlockSpec((tm, d), lambda i, c: (i, 0)),
                  pl.BlockSpec((1, 9, d), lambda i, c: (i // per_batch, 0, 0)),
                  pl.BlockSpec((1, 2, d), lambda i, c: (i // per_batch, 0, 0)),
                  pl.BlockSpec((1, d), lambda i, c: (0, 0)),
                  pl.BlockSpec((1, d), lambda i, c: (0, 0)),
                  pl.BlockSpec((None, d, tn), lambda i, c: (j, 0, c)),
                  pl.BlockSpec((d, tn), lambda i, c: (0, c)),
                  pl.BlockSpec((d, tn), lambda i, c: (0, nn + c)),
                  pl.BlockSpec((d, LANES), lambda i, c: (0, 0)),
                  pl.BlockSpec((1, LANES), lambda i, c: (0, 0))],
        out_specs=[row_blk, row_blk, row_blk,
                   pl.BlockSpec((tm, LANES), lambda i, c: (i, 0))],
        out_shape=[jax.ShapeDtypeStruct((t, d), BF16),
                   jax.ShapeDtypeStruct((t, d), BF16),
                   jax.ShapeDtypeStruct((t, d), BF16),
                   jax.ShapeDtypeStruct((t, LANES), F32)],
        scratch_shapes=[pltpu.VMEM((tm, d), BF16),
                        pltpu.VMEM((tm, d), BF16),
                        pltpu.VMEM((SUBLANES, LANES), F32)],
        compiler_params=_params(("arbitrary", "arbitrary")),
        name="qkv",
    )(x, ada, ada_kv, g.reshape(1, d), gkv.reshape(1, d), w_q, w_kv, w_kv, w_f, b_f)


def _split3(f):
    hi = f.astype(BF16)
    r1 = f - hi.astype(F32)
    mid = r1.astype(BF16)
    lo = (r1 - mid.astype(F32)).astype(BF16)
    return hi.astype(F32), mid.astype(F32), lo.astype(F32)


def _attn_kernel(q_ref, k_ref, v_ref, fc_ref, o_ref, kaug_ref, vt_ref, st_ref, *, tq, tk, hp):
    hg = pl.program_id(1)
    i = pl.program_id(2)
    seq = k_ref.shape[1]

    def head_col(blk, head):
        lane = lax.broadcasted_iota(jnp.int32, blk.shape, 1)
        return jnp.sum(jnp.where(lane == head, blk, 0.0), axis=1, keepdims=True) * LOG2E

    @pl.when(i == 0)
    def _():
        lane = lax.broadcasted_iota(jnp.int32, (tk, LANES), 1)

        def build(r, _):
            start = pl.multiple_of(r * tk, tk)
            blk = fc_ref[0, pl.ds(start, tk), :]
            for hh in range(hp):
                hi, mid, lo = _split3(head_col(blk, hg * hp + hh))
                aug = jnp.where(lane < 3, 1.0,
                                jnp.where(lane == 3, -hi,
                                          jnp.where(lane == 4, -mid, jnp.where(lane == 5, -lo, 0.0))))
                kaug_ref[hh, pl.ds(start, tk), :HEAD_DIM] = k_ref[0, pl.ds(start, tk),
                                                                  hh * HEAD_DIM:(hh + 1) * HEAD_DIM]
                kaug_ref[hh, pl.ds(start, tk), HEAD_DIM:] = aug.astype(BF16)
                vblk = v_ref[0, pl.ds(start, tk), hh * HEAD_DIM:(hh + 1) * HEAD_DIM]
                vt_ref[hh, :, pl.ds(start, tk)] = vblk.astype(F32).T.astype(BF16)
            return 0
        lax.fori_loop(0, seq // tk, build, 0)

    fblk = fc_ref[0, pl.ds(pl.multiple_of(i * tq, tq), tq), :]
    lane = lax.broadcasted_iota(jnp.int32, (tq, LANES), 1)
    qts = []
    for hh in range(hp):
        hi, mid, lo = _split3(head_col(fblk, hg * hp + hh))
        aq = jnp.where(lane == 0, hi,
                       jnp.where(lane == 1, mid,
                                 jnp.where(lane == 2, lo, jnp.where(lane < 6, 1.0, 0.0))))
        qh = q_ref[0, :, hh * HEAD_DIM:(hh + 1) * HEAD_DIM].astype(F32)
        qts.append(jnp.concatenate([qh.T, aq.T], axis=0).astype(BF16))

    def scores(j, slot, lo=0):
        start = pl.multiple_of(j * tk, tk)
        for hh in range(hp):
            st_ref[slot, hh, :, lo:] = _dot(kaug_ref[hh, pl.ds(start, tk), :], qts[hh][:, lo:])

    def softmax_pv(j, slot, carry, diag):
        start = pl.multiple_of(j * tk, tk)
        lo = 0 if diag is None else diag * tk
        out = []
        for hh in range(hp):
            m0, l0, acc0 = carry[hh]
            m, l, acc = m0[:, lo:], l0[:, lo:], acc0[:, lo:]
            st = st_ref[slot, hh, :, lo:]
            if diag is not None:
                key = lax.broadcasted_iota(jnp.int32, st.shape, 0)
                qry = lax.broadcasted_iota(jnp.int32, st.shape, 1)
                st = jnp.where(key <= qry, st, -jnp.inf)
            m_new = jnp.maximum(m, jnp.max(st, axis=0, keepdims=True))
            alpha = jnp.exp2(m - m_new)
            pt = jnp.exp2(st - m_new)
            l = alpha * l + jnp.sum(pt, axis=0, keepdims=True)
            acc = alpha * acc + _dot(vt_ref[hh, :, pl.ds(start, tk)], pt.astype(BF16))
            if lo:
                m_new, l, acc = (jnp.concatenate([old[:, :lo], new], axis=1)
                                 for old, new in ((m0, m_new), (l0, l), (acc0, acc)))
            out.append((m_new, l, acc))
        return tuple(out)

    def pair(jj, carry):
        a = 2 * jj
        scores(a + 1, 1)
        carry = softmax_pv(a, 0, carry, None)
        scores(a + 2, 0)
        return softmax_pv(a + 1, 1, carry, None)

    init = tuple((jnp.full((1, tq), -jnp.inf, F32), jnp.zeros((1, tq), F32),
                  jnp.zeros((HEAD_DIM, tq), F32)) for _ in range(hp))
    scores(0, 0)
    carry = lax.fori_loop(0, i, pair, init)
    scores(2 * i + 1, 1, lo=tk)
    carry = softmax_pv(2 * i, 0, carry, 0)
    carry = softmax_pv(2 * i + 1, 1, carry, 1)
    for hh in range(hp):
        _, l, acc = carry[hh]
        o_ref[0, :, hh * HEAD_DIM:(hh + 1) * HEAD_DIM] = (acc / l).T.astype(BF16)


def _attn(q, k, v, fc, tq=1024, hp=2):
    b, s, d = q.shape
    w = hp * HEAD_DIM
    tk = tq // 2
    return pl.pallas_call(
        functools.partial(_attn_kernel, tq=tq, tk=tk, hp=hp),
        grid=(b, N_HEADS // hp, s // tq),
        in_specs=[pl.BlockSpec((1, tq, w), lambda bb, h, i: (bb, i, h)),
                  pl.BlockSpec((1, s, w), lambda bb, h, i: (bb, 0, h)),
                  pl.BlockSpec((1, s, w), lambda bb, h, i: (bb, 0, h)),
                  pl.BlockSpec((1, s, LANES), lambda bb, h, i: (bb, 0, 0))],
        out_specs=pl.BlockSpec((1, tq, w), lambda bb, h, i: (bb, i, h)),
        out_shape=jax.ShapeDtypeStruct((b, s, d), BF16),
        scratch_shapes=[pltpu.VMEM((hp, s, 2 * HEAD_DIM), BF16),
                        pltpu.VMEM((hp, HEAD_DIM, s), BF16),
                        pltpu.VMEM((2, hp, tk, tq), F32)],
        compiler_params=_params(("parallel", "parallel", "arbitrary")),
        name="attn",
    )(q, k, v, fc)


def _oproj_kernel(x_ref, a_ref, ada_ref, w_ref, o_ref):
    gate = ada_ref[0, 5:6, :]
    o_ref[...] = x_ref[...] + gate * _dot(a_ref[...], w_ref[...])


def _oproj(x, attn, ada, w_o, j, seq, tm=1024):
    t, d = x.shape
    per_batch = seq // tm
    return pl.pallas_call(
        _oproj_kernel,
        grid=(t // tm,),
        in_specs=[pl.BlockSpec((tm, d), lambda i: (i, 0)),
                  pl.BlockSpec((tm, d), lambda i: (i, 0)),
                  pl.BlockSpec((1, 9, d), lambda i: (i // per_batch, 0, 0)),
                  pl.BlockSpec((None, d, d), lambda i: (j, 0, 0), pipeline_mode=pl.Buffered(1))],
        out_specs=pl.BlockSpec((tm, d), lambda i: (i, 0)),
        out_shape=jax.ShapeDtypeStruct((t, d), F32),
        compiler_params=_params(("parallel",)),
        name="oproj",
    )(x, attn, ada, w_o)


def kernel(x, c, norm_g, w_ada, b_ada, w_ffn_in, w_ffn_out, w_conv_in, conv_w, conv_b, w_conv_out,
           kv_norm_g, w_ada_kv, b_ada_kv, w_kvf, b_fgate, w_q, w_o, final_g):
    b, s, d = x.shape
    depth = norm_g.shape[0]
    n_a = w_conv_in.shape[0]
    t = b * s
    assert w_q.shape[0] == 1 and depth == n_a + 1

    w_in_b = w_ffn_in[0, 0].astype(BF16)
    w_out_b = w_ffn_out[0, 0].astype(BF16)
    w_f_b = jnp.pad(w_kvf[:, 2 * d:], ((0, 0), (0, LANES - N_HEADS))).astype(BF16)
    b_f = jnp.pad(b_fgate, (0, LANES - N_HEADS)).reshape(1, LANES)
    conv_b3 = conv_b.reshape(n_a, 1, d)
    attn_casts = [(w_q, (0,)), (w_o, (0,)), (w_kvf, ())]

    c_pad = jnp.pad(c, ((0, SUBLANES - b), (0, 0)))
    xf = x.reshape(t, d)
    for l in range(depth):
        ada = _ada(c_pad, w_ada, b_ada, l)[:b].reshape(b, 9, d)
        if l < n_a:
            mixer_casts = [(w_conv_in, (l,)), (w_conv_out, (l,))]
        else:
            mixer_casts = attn_casts if n_a == 0 else []
        xf, w_in_b, w_out_b, *mixer_w = _ffn(xf, ada, norm_g[l, 0], w_in_b, w_out_b, 0, s,
                                             cast_next=(w_ffn_in, w_ffn_out, l, 1), casts=mixer_casts)
        if l < n_a:
            w_conv_in_b, w_conv_out_b = mixer_w
            if l == n_a - 1:
                xf, *attn_w = _conv(xf, ada, norm_g[l, 1], w_conv_in_b, conv_w, conv_b3, w_conv_out_b, l, s,
                                    casts=attn_casts)
            else:
                xf = _conv(xf, ada, norm_g[l, 1], w_conv_in_b, conv_w, conv_b3, w_conv_out_b, l, s)
        else:
            if n_a == 0:
                attn_w = mixer_w
            w_q_b, w_o_b, w_kv_b = attn_w
            ada_kv = _ada(c_pad, w_ada_kv[None], b_ada_kv[None], 0)[:b].reshape(b, 2, d)
            q, k, v, fc = _qkv(xf, ada, ada_kv, norm_g[l, 1], kv_norm_g, w_q_b[None], w_kv_b, w_f_b, b_f, 0, s)
            attn = _attn(q.reshape(b, s, d), k.reshape(b, s, d), v.reshape(b, s, d),
                         fc.reshape(b, s, LANES))
            xf = _oproj(xf, attn.reshape(t, d), ada, w_o_b[None], 0, s)
        if l == depth - 1:
            xf = _ffn(xf, ada, norm_g[l, 2], w_in_b, w_out_b, 2, s, final_g=final_g)
        else:
            xf, w_in_b, w_out_b = _ffn(xf, ada, norm_g[l, 2], w_in_b, w_out_b, 2, s,
                                       cast_next=(w_ffn_in, w_ffn_out, l + 1, 0))
    return xf.reshape(b, s, d)
```

```python
import functools
import math

import jax
import jax.numpy as jnp
from jax import lax
from jax.experimental import pallas as pl
from jax.experimental.pallas import tpu as pltpu

EPS = 1e-6
N_HEADS = 16
HEAD_DIM = 128
CONV_WIDTH = 3
LOG2E = math.log2(math.e)
LANES = 128
SUBLANES = 8
MXU_DIM = 256
VMEM_LIMIT_BYTES = 59 * 1024 * 1024

F32 = jnp.float32
BF16 = jnp.bfloat16


def _params(semantics):
    return pltpu.CompilerParams(dimension_semantics=semantics,
                                vmem_limit_bytes=VMEM_LIMIT_BYTES)


def _dot(a, b):
    return jnp.dot(a, b, preferred_element_type=F32)


def _rmsnorm(x, g):
    return x * lax.rsqrt(jnp.mean(x * x, axis=-1, keepdims=True) + EPS) * g


NORM_ROWS = 16
NORM_UNROLL = 8
def _modulated_norm_to(x_ref, *targets):
    params = [(h_ref, g_ref[...], shift, 1.0 + scale) for h_ref, g_ref, shift, scale in targets]

    def body(r, _):
        rows = pl.ds(pl.multiple_of(r * NORM_ROWS, NORM_ROWS), NORM_ROWS)
        x = x_ref[rows, :]
        xhat = x * lax.rsqrt(jnp.mean(x * x, axis=-1, keepdims=True) + EPS)
        for h_ref, g, shift, scale1 in params:
            h_ref[rows, :] = (xhat * g * scale1 + shift).astype(BF16)
        return 0
    lax.fori_loop(0, x_ref.shape[0] // NORM_ROWS, body, 0, unroll=NORM_UNROLL)


def _residual_to(o_ref, x_ref, coef, final_g=None, rs_ref=None):
    if final_g is None:
        o_ref[...] = x_ref[...] + coef * o_ref[...]
        return
    n = x_ref.shape[0] // NORM_ROWS

    def rows_of(r):
        return pl.ds(pl.multiple_of(r * NORM_ROWS, NORM_ROWS), NORM_ROWS)

    def stats(r, _):
        y = x_ref[rows_of(r), :] + coef * o_ref[rows_of(r), :]
        rs = lax.rsqrt(jnp.mean(y * y, axis=-1, keepdims=True) + EPS)
        rs_ref[rows_of(r), :] = jnp.broadcast_to(rs, (NORM_ROWS, LANES))
        return 0
    lax.fori_loop(0, n, stats, 0, unroll=NORM_UNROLL)

    def scale(r, _):
        y = x_ref[rows_of(r), :] + coef * o_ref[rows_of(r), :]
        o_ref[rows_of(r), :] = y * rs_ref[rows_of(r), 0:1] * final_g
        return 0
    lax.fori_loop(0, n, scale, 0, unroll=NORM_UNROLL)


def _ada_rows(ada_ref, sub):
    return tuple(ada_ref[0, 3 * sub + k:3 * sub + k + 1, :] for k in range(3))


def _ada_kernel(c_ref, w_ref, b_ref, o_ref):
    c = c_ref[...]
    cond = c * jax.nn.sigmoid(c)
    o_ref[...] = _dot(cond.astype(BF16), w_ref[...].astype(BF16)) + b_ref[...]


def _ada(c_pad, w, b, l, tn=1024):
    nl, d, n = w.shape
    return pl.pallas_call(
        _ada_kernel,
        grid=(n // tn,),
        in_specs=[pl.BlockSpec((SUBLANES, d), lambda j: (0, 0)),
                  pl.BlockSpec((None, d, tn), lambda j: (l, 0, j)),
                  pl.BlockSpec((None, 1, tn), lambda j: (l, 0, j))],
        out_specs=pl.BlockSpec((SUBLANES, tn), lambda j: (0, j)),
        out_shape=jax.ShapeDtypeStruct((SUBLANES, n), F32),
        compiler_params=_params(("parallel",)),
        name="ada",
    )(c_pad, w, b.reshape(nl, 1, n))


def _row_cast_specs(w, lead, n_steps, step_of, row_tile=16):
    rows, cols = w.shape[-2:]
    cols -= cols % LANES
    n_tiles = rows // row_tile
    assert rows % row_tile == 0 and n_tiles <= n_steps

    def tile(*g):
        return jnp.minimum(step_of(*g), n_tiles - 1)
    in_spec = pl.BlockSpec((None,) * len(lead) + (row_tile, cols), lambda *g: tuple(lead) + (tile(*g), 0))
    out_spec = pl.BlockSpec((row_tile, cols), lambda *g: (tile(*g), 0))
    return in_spec, out_spec, jax.ShapeDtypeStruct((rows, cols), BF16)


def _cast_tiles(in_refs, out_refs):
    for src, dst in zip(in_refs, out_refs):
        dst[...] = src[...].astype(BF16)


def _ffn_kernel(x_ref, ada_ref, g_ref, wg_ref, wu_ref, wo_ref, *rest, sub, final, n_cast):
    rest = list(rest)
    fg_ref = rest.pop(0) if final else None
    cast_in = [rest.pop(0) for _ in range(n_cast)]
    o_ref = rest.pop(0)
    cast_out = [rest.pop(0) for _ in range(n_cast)]
    h_ref = rest.pop(0)
    rs_ref = rest.pop(0) if final else None
    f = pl.program_id(1)

    shift, scale, gate = _ada_rows(ada_ref, sub)

    @pl.when(f == 0)
    def _():
        _modulated_norm_to(x_ref, (h_ref, g_ref, shift, scale))
        o_ref[...] = jnp.zeros_like(o_ref)

    _cast_tiles(cast_in, cast_out)
    h = h_ref[...]
    a = _dot(h, wg_ref[...])
    b = _dot(h, wu_ref[...])
    act = (a * jax.nn.sigmoid(a) * b).astype(BF16)
    o_ref[...] += _dot(act, wo_ref[...])

    @pl.when(f == pl.num_programs(1) - 1)
    def _():
        if final:
            _residual_to(o_ref, x_ref, 0.5 * gate, fg_ref[...], rs_ref)
        else:
            _residual_to(o_ref, x_ref, 0.5 * gate)


def _ffn(x, ada, g, w_in, w_out, sub, seq, final_g=None, cast_next=None, casts=(), tf=512):
    t, d = x.shape
    ff = w_out.shape[0]
    nf = ff // tf
    final = final_g is not None
    tm = 1024
    per_batch = seq // tm
    ni = t // tm
    in_specs = [pl.BlockSpec((tm, d), lambda i, f: (i, 0)),
                pl.BlockSpec((1, 9, d), lambda i, f: (i // per_batch, 0, 0)),
                pl.BlockSpec((1, d), lambda i, f: (0, 0)),
                pl.BlockSpec((d, tf), lambda i, f: (0, f)),
                pl.BlockSpec((d, tf), lambda i, f: (0, nf + f)),
                pl.BlockSpec((tf, d), lambda i, f: (f, 0))]
    args = [x, ada, g.reshape(1, d), w_in, w_in, w_out]
    out_specs = [pl.BlockSpec((tm, d), lambda i, f: (i, 0))]
    out_shape = [jax.ShapeDtypeStruct((t, d), F32)]
    if final:
        in_specs.append(pl.BlockSpec((1, d), lambda i, f: (0, 0)))
        args.append(final_g.reshape(1, d))
    n_cast = 0
    if cast_next:
        w_in32, w_out32, l2, idx2 = cast_next
        in_tile = (d // ni, 2 * ff // nf)
        out_tile = (ff // (ni * nf), d)
        in_specs += [pl.BlockSpec((None, None) + in_tile, lambda i, f: (l2, idx2, i, f)),
                     pl.BlockSpec((None, None) + out_tile, lambda i, f: (l2, idx2, i * nf + f, 0))]
        args += [w_in32, w_out32]
        out_specs += [pl.BlockSpec(in_tile, lambda i, f: (i, f)),
                      pl.BlockSpec(out_tile, lambda i, f: (i * nf + f, 0))]
        out_shape += [jax.ShapeDtypeStruct((d, 2 * ff), BF16), jax.ShapeDtypeStruct((ff, d), BF16)]
        n_cast += 2
    for w, lead in casts:
        i_spec, o_spec, o_shape = _row_cast_specs(w, lead, ni * nf, lambda i, f: i * nf + f)
        in_specs.append(i_spec)
        args.append(w)
        out_specs.append(o_spec)
        out_shape.append(o_shape)
        n_cast += 1
    outs = pl.pallas_call(
        functools.partial(_ffn_kernel, sub=sub, final=final, n_cast=n_cast),
        grid=(ni, nf),
        in_specs=in_specs,
        out_specs=out_specs,
        out_shape=out_shape,
        scratch_shapes=[pltpu.VMEM((tm, d), BF16)] + ([pltpu.VMEM((tm, LANES), F32)] if final else []),
        compiler_params=_params(("parallel", "arbitrary")),
        name="ffn_final" if final else "ffn",
    )(*args)
    return outs if n_cast else outs[0]


def _conv_kernel(x_ref, ada_ref, g_ref, wb_ref, wc_ref, wx_ref, cw_ref, cb_ref, wo_ref,
                 *rest, per_batch, n_cast):
    cast_in, (o_ref, *cast_out), (h_ref, tail_ref) = (rest[:n_cast], rest[n_cast:2 * n_cast + 1],
                                                      rest[2 * n_cast + 1:])
    i = pl.program_id(0)
    c = pl.program_id(1)
    tm = x_ref.shape[0]

    shift, scale, gate = _ada_rows(ada_ref, 1)

    @pl.when(c == 0)
    def _():
        _modulated_norm_to(x_ref, (h_ref, g_ref, shift, scale))
        o_ref[...] = jnp.zeros_like(o_ref)

    @pl.when(i % per_batch == 0)
    def _():
        tail_ref[c] = jnp.zeros(tail_ref.shape[1:], F32)

    _cast_tiles(cast_in, cast_out)
    h = h_ref[...]
    bg = _dot(h, wb_ref[...])
    cg = _dot(h, wc_ref[...])
    xv = _dot(h, wx_ref[...])
    u = cg * xv
    tail = tail_ref[c]
    p1 = tail[SUBLANES - 1:SUBLANES, :]
    p2 = tail[SUBLANES - 2:SUBLANES - 1, :]
    row = lax.broadcasted_iota(jnp.int32, u.shape, 0)
    u1 = jnp.where(row == 0, p1, pltpu.roll(u, 1, 0))
    u2 = jnp.where(row == 0, p2, jnp.where(row == 1, p1, pltpu.roll(u, 2, 0)))
    conv = cw_ref[0:1, :] * u2 + cw_ref[1:2, :] * u1 + cw_ref[2:3, :] * u + cb_ref[...]
    tail_ref[c] = u[tm - SUBLANES:, :]
    o_ref[...] += _dot((bg * conv).astype(BF16), wo_ref[...])

    @pl.when(c == pl.num_programs(1) - 1)
    def _():
        _residual_to(o_ref, x_ref, gate)


def _conv(x, ada, g, w_in, conv_w, conv_b, w_out, l, seq, casts=(), tm=512, tc=512):
    t, d = x.shape
    nc = d // tc
    ni = t // tm
    per_batch = seq // tm
    in_specs = [pl.BlockSpec((tm, d), lambda i, c: (i, 0)),
                pl.BlockSpec((1, 9, d), lambda i, c: (i // per_batch, 0, 0)),
                pl.BlockSpec((1, d), lambda i, c: (0, 0)),
                pl.BlockSpec((d, tc), lambda i, c: (0, c)),
                pl.BlockSpec((d, tc), lambda i, c: (0, nc + c)),
                pl.BlockSpec((d, tc), lambda i, c: (0, 2 * nc + c)),
                pl.BlockSpec((None, CONV_WIDTH, tc), lambda i, c: (l, 0, c)),
                pl.BlockSpec((None, 1, tc), lambda i, c: (l, 0, c)),
                pl.BlockSpec((tc, d), lambda i, c: (c, 0))]
    args = [x, ada, g.reshape(1, d), w_in, w_in, w_in, conv_w, conv_b, w_out]
    out_specs = [pl.BlockSpec((tm, d), lambda i, c: (i, 0))]
    out_shape = [jax.ShapeDtypeStruct((t, d), F32)]
    for w, lead in casts:
        i_spec, o_spec, o_shape = _row_cast_specs(w, lead, ni * nc, lambda i, c: i * nc + c)
        in_specs.append(i_spec)
        args.append(w)
        out_specs.append(o_spec)
        out_shape.append(o_shape)
    outs = pl.pallas_call(
        functools.partial(_conv_kernel, per_batch=per_batch, n_cast=len(casts)),
        grid=(ni, nc),
        in_specs=in_specs,
        out_specs=out_specs,
        out_shape=out_shape,
        scratch_shapes=[pltpu.VMEM((tm, d), BF16),
                        pltpu.VMEM((nc, SUBLANES, tc), F32)],
        compiler_params=_params(("arbitrary", "arbitrary")),
        name="conv",
    )(*args)
    return outs if casts else outs[0]


def _qkv_kernel(x_ref, ada_ref, adakv_ref, g_ref, gkv_ref, wq_ref, wk_ref, wv_ref, wf_ref, bf_ref,
                q_ref, k_ref, v_ref, fc_ref, hq_ref, hkv_ref, carry_ref, *, per_batch, q_scale):
    i = pl.program_id(0)
    c = pl.program_id(1)
    tm = x_ref.shape[0]

    @pl.when(c == 0)
    def _():
        shift, scale, _ = _ada_rows(ada_ref, 1)
        _modulated_norm_to(x_ref, (hq_ref, g_ref, shift, scale),
                           (hkv_ref, gkv_ref, adakv_ref[0, 0:1, :], adakv_ref[0, 1:2, :]))
        hkv = hkv_ref[...]

        @pl.when(i % per_batch == 0)
        def _():
            carry_ref[...] = jnp.zeros_like(carry_ref)

        zf = _dot(hkv, wf_ref[...]) + bf_ref[...]
        ls = jnp.minimum(zf, 0.0) - jnp.log1p(jnp.exp(-jnp.abs(zf)))
        hi = ls.astype(BF16)
        r1 = ls - hi.astype(F32)
        mid = r1.astype(BF16)
        lo = (r1 - mid.astype(F32)).astype(BF16)
        rr = lax.broadcasted_iota(jnp.int32, (tm, tm), 0)
        cc = lax.broadcasted_iota(jnp.int32, (tm, tm), 1)
        tri = (rr >= cc).astype(BF16)
        parts = _dot(tri, jnp.concatenate([hi, mid, lo], axis=1))
        cum = (parts[:, :LANES] + parts[:, LANES:2 * LANES]) + parts[:, 2 * LANES:] + carry_ref[0:1, :]
        carry_ref[...] = jnp.broadcast_to(cum[tm - 1:tm, :], carry_ref.shape)
        fc_ref[...] = cum

    hkv = hkv_ref[...]
    q_ref[...] = (_dot(hq_ref[...], wq_ref[...]) * q_scale).astype(BF16)
    kc = _dot(hkv, wk_ref[...])
    cum = fc_ref[...]
    lane = lax.broadcasted_iota(jnp.int32, (tm, LANES), 1)
    for hh in range(kc.shape[1] // HEAD_DIM):
        head = c * (kc.shape[1] // HEAD_DIM) + hh
        col = jnp.sum(jnp.where(lane == head, cum, 0.0), axis=1, keepdims=True) * LOG2E
        hi, mid, lo = _split3(col)
        aug = jnp.where(lane < 3, 1.0,
                        jnp.where(lane == 3, -hi, jnp.where(lane == 4, -mid, jnp.where(lane == 5, -lo, 0.0))))
        k_ref[:, 2 * hh * HEAD_DIM:(2 * hh + 1) * HEAD_DIM] = kc[:, hh * HEAD_DIM:(hh + 1) * HEAD_DIM].astype(BF16)
        k_ref[:, (2 * hh + 1) * HEAD_DIM:(2 * hh + 2) * HEAD_DIM] = aug.astype(BF16)
    v_ref[0] = _dot(hkv, wv_ref[...]).T.astype(BF16)


def _qkv(x, ada, ada_kv, g, gkv, w_q, w_kv, w_f, b_f, j, seq, tm=1024, tn=512):
    t, d = x.shape
    nn = d // tn
    per_batch = seq // tm
    q_scale = LOG2E / math.sqrt(HEAD_DIM)
    row_blk = pl.BlockSpec((tm, tn), lambda i, c: (i, c))
    return pl.pallas_call(
        functools.partial(_qkv_kernel, per_batch=per_batch, q_scale=q_scale),
        grid=(t // tm, nn),
        in_specs=[pl.BlockSpec((tm, d), lambda i, c: (i, 0)),
                  pl.BlockSpec((1, 9, d), lambda i, c: (i // per_batch, 0, 0)),
                  pl.BlockSpec((1, 2, d), lambda i, c: (i // per_batch, 0, 0)),
                  pl.BlockSpec((1, d), lambda i, c: (0, 0)),
                  pl.BlockSpec((1, d), lambda i, c: (0, 0)),
                  pl.BlockSpec((None, d, tn), lambda i, c: (j, 0, c)),
                  pl.BlockSpec((d, tn), lambda i, c: (0, c)),
                  pl.BlockSpec((d, tn), lambda i, c: (0, nn + c)),
                  pl.BlockSpec((d, LANES), lambda i, c: (0, 0)),
                  pl.BlockSpec((1, LANES), lambda i, c: (0, 0))],
        out_specs=[row_blk,
                   pl.BlockSpec((tm, 2 * tn), lambda i, c: (i, c)),
                   pl.BlockSpec((1, tn, tm), lambda i, c: (i // per_batch, c, i % per_batch)),
                   pl.BlockSpec((tm, LANES), lambda i, c: (i, 0))],
        out_shape=[jax.ShapeDtypeStruct((t, d), BF16),
                   jax.ShapeDtypeStruct((t, 2 * d), BF16),
                   jax.ShapeDtypeStruct((t // seq, d, seq), BF16),
                   jax.ShapeDtypeStruct((t, LANES), F32)],
        scratch_shapes=[pltpu.VMEM((tm, d), BF16),
                        pltpu.VMEM((tm, d), BF16),
                        pltpu.VMEM((SUBLANES, LANES), F32)],
        compiler_params=_params(("arbitrary", "arbitrary")),
        name="qkv",
    )(x, ada, ada_kv, g.reshape(1, d), gkv.reshape(1, d), w_q, w_kv, w_kv, w_f, b_f)


def _split3(f):
    hi = f.astype(BF16)
    r1 = f - hi.astype(F32)
    mid = r1.astype(BF16)
    lo = (r1 - mid.astype(F32)).astype(BF16)
    return hi.astype(F32), mid.astype(F32), lo.astype(F32)


def _attn_kernel(q_ref, k_ref, vt_ref, fc_ref, o_ref, st_ref, *, tq, tk, hp):
    hg = pl.program_id(1)
    i = pl.program_id(2)

    fblk = fc_ref[0]
    lane = lax.broadcasted_iota(jnp.int32, (tq, LANES), 1)
    qts = []
    for hh in range(hp):
        fq = jnp.sum(jnp.where(lane == hg * hp + hh, fblk, 0.0), axis=1, keepdims=True) * LOG2E
        hi, mid, lo = _split3(fq)
        aq = jnp.where(lane == 0, hi,
                       jnp.where(lane == 1, mid,
                                 jnp.where(lane == 2, lo, jnp.where(lane < 6, 1.0, 0.0))))
        qh = q_ref[0, :, hh * HEAD_DIM:(hh + 1) * HEAD_DIM].astype(F32)
        qts.append(jnp.concatenate([qh.T, aq.T], axis=0).astype(BF16))

    def scores(j, slot, lo=0):
        start = pl.multiple_of(j * tk, tk)
        for hh in range(hp):
            kj = k_ref[0, pl.ds(start, tk), 2 * hh * HEAD_DIM:2 * (hh + 1) * HEAD_DIM]
            st_ref[slot, hh, :, lo:] = _dot(kj, qts[hh][:, lo:])

    def softmax_pv(j, slot, carry, diag):
        start = pl.multiple_of(j * tk, tk)
        lo = 0 if diag is None else diag * tk
        out = []
        for hh in range(hp):
            m0, l0, acc0 = carry[hh]
            m, l, acc = m0[:, lo:], l0[:, lo:], acc0[:, lo:]
            st = st_ref[slot, hh, :, lo:]
            if diag is not None:
                key = lax.broadcasted_iota(jnp.int32, st.shape, 0)
                qry = lax.broadcasted_iota(jnp.int32, st.shape, 1)
                st = jnp.where(key <= qry, st, -jnp.inf)
            m_new = jnp.maximum(m, jnp.max(st, axis=0, keepdims=True))
            alpha = jnp.exp2(m - m_new)
            pt = jnp.exp2(st - m_new)
            l = alpha * l + jnp.sum(pt, axis=0, keepdims=True)
            vj = vt_ref[0, hh * HEAD_DIM:(hh + 1) * HEAD_DIM, pl.ds(start, tk)]
            acc = alpha * acc + _dot(vj, pt.astype(BF16))
            if lo:
                m_new, l, acc = (jnp.concatenate([old[:, :lo], new], axis=1)
                                 for old, new in ((m0, m_new), (l0, l), (acc0, acc)))
            out.append((m_new, l, acc))
        return tuple(out)

    def pair(jj, carry):
        a = 2 * jj
        scores(a + 1, 1)
        carry = softmax_pv(a, 0, carry, None)
        scores(a + 2, 0)
        return softmax_pv(a + 1, 1, carry, None)

    init = tuple((jnp.full((1, tq), -jnp.inf, F32), jnp.zeros((1, tq), F32),
                  jnp.zeros((HEAD_DIM, tq), F32)) for _ in range(hp))
    scores(0, 0)
    carry = lax.fori_loop(0, i, pair, init)
    scores(2 * i + 1, 1, lo=tk)
    carry = softmax_pv(2 * i, 0, carry, 0)
    carry = softmax_pv(2 * i + 1, 1, carry, 1)
    for hh in range(hp):
        _, l, acc = carry[hh]
        o_ref[0, :, hh * HEAD_DIM:(hh + 1) * HEAD_DIM] = (acc / l).T.astype(BF16)


def _attn(q, kp, vt, fc, tq=1024, hp=2):
    b, s, d = q.shape
    w = hp * HEAD_DIM
    tk = tq // 2
    return pl.pallas_call(
        functools.partial(_attn_kernel, tq=tq, tk=tk, hp=hp),
        grid=(b, N_HEADS // hp, s // tq),
        in_specs=[pl.BlockSpec((1, tq, w), lambda bb, h, i: (bb, i, h)),
                  pl.BlockSpec((1, s, 2 * w), lambda bb, h, i: (bb, 0, h)),
                  pl.BlockSpec((1, w, s), lambda bb, h, i: (bb, h, 0)),
                  pl.BlockSpec((1, tq, LANES), lambda bb, h, i: (bb, i, 0))],
        out_specs=pl.BlockSpec((1, tq, w), lambda bb, h, i: (bb, i, h)),
        out_shape=jax.ShapeDtypeStruct((b, s, d), BF16),
        scratch_shapes=[pltpu.VMEM((2, hp, tk, tq), F32)],
        compiler_params=_params(("parallel", "parallel", "parallel")),
        name="attn",
    )(q, kp, vt, fc)


def _oproj_kernel(x_ref, a_ref, ada_ref, w_ref, o_ref):
    gate = ada_ref[0, 5:6, :]
    o_ref[...] = x_ref[...] + gate * _dot(a_ref[...], w_ref[...])


def _oproj(x, attn, ada, w_o, j, seq, tm=1024):
    t, d = x.shape
    per_batch = seq // tm
    return pl.pallas_call(
        _oproj_kernel,
        grid=(t // tm,),
        in_specs=[pl.BlockSpec((tm, d), lambda i: (i, 0)),
                  pl.BlockSpec((tm, d), lambda i: (i, 0)),
                  pl.BlockSpec((1, 9, d), lambda i: (i // per_batch, 0, 0)),
                  pl.BlockSpec((None, d, d), lambda i: (j, 0, 0), pipeline_mode=pl.Buffered(1))],
        out_specs=pl.BlockSpec((tm, d), lambda i: (i, 0)),
        out_shape=jax.ShapeDtypeStruct((t, d), F32),
        compiler_params=_params(("parallel",)),
        name="oproj",
    )(x, attn, ada, w_o)


def kernel(x, c, norm_g, w_ada, b_ada, w_ffn_in, w_ffn_out, w_conv_in, conv_w, conv_b, w_conv_out,
           kv_norm_g, w_ada_kv, b_ada_kv, w_kvf, b_fgate, w_q, w_o, final_g):
    b, s, d = x.shape
    depth = norm_g.shape[0]
    n_a = w_conv_in.shape[0]
    t = b * s
    assert w_q.shape[0] == 1 and depth == n_a + 1

    w_in_b = w_ffn_in[0, 0].astype(BF16)
    w_out_b = w_ffn_out[0, 0].astype(BF16)
    w_f_b = jnp.pad(w_kvf[:, 2 * d:], ((0, 0), (0, LANES - N_HEADS))).astype(BF16)
    b_f = jnp.pad(b_fgate, (0, LANES - N_HEADS)).reshape(1, LANES)
    conv_b3 = conv_b.reshape(n_a, 1, d)
    attn_casts = [(w_q, (0,)), (w_o, (0,)), (w_kvf, ())]

    c_pad = jnp.pad(c, ((0, SUBLANES - b), (0, 0)))
    xf = x.reshape(t, d)
    for l in range(depth):
        ada = _ada(c_pad, w_ada, b_ada, l)[:b].reshape(b, 9, d)
        if l < n_a:
            mixer_casts = [(w_conv_in, (l,)), (w_conv_out, (l,))]
        else:
            mixer_casts = attn_casts if n_a == 0 else []
        xf, w_in_b, w_out_b, *mixer_w = _ffn(xf, ada, norm_g[l, 0], w_in_b, w_out_b, 0, s,
                                             cast_next=(w_ffn_in, w_ffn_out, l, 1), casts=mixer_casts)
        if l < n_a:
            w_conv_in_b, w_conv_out_b = mixer_w
            if l == n_a - 1:
                xf, *attn_w = _conv(xf, ada, norm_g[l, 1], w_conv_in_b, conv_w, conv_b3, w_conv_out_b, l, s,
                                    casts=attn_casts)
            else:
                xf = _conv(xf, ada, norm_g[l, 1], w_conv_in_b, conv_w, conv_b3, w_conv_out_b, l, s)
        else:
            if n_a == 0:
                attn_w = mixer_w
            w_q_b, w_o_b, w_kv_b = attn_w
            ada_kv = _ada(c_pad, w_ada_kv[None], b_ada_kv[None], 0)[:b].reshape(b, 2, d)
            q, kp, vt, fc = _qkv(xf, ada, ada_kv, norm_g[l, 1], kv_norm_g, w_q_b[None], w_kv_b, w_f_b, b_f, 0, s)
            attn = _attn(q.reshape(b, s, d), kp.reshape(b, s, 2 * d), vt, fc.reshape(b, s, LANES))
            xf = _oproj(xf, attn.reshape(t, d), ada, w_o_b[None], 0, s)
        if l == depth - 1:
            xf = _ffn(xf, ada, norm_g[l, 2], w_in_b, w_out_b, 2, s, final_g=final_g)
        else:
            xf, w_in_b, w_out_b = _ffn(xf, ada, norm_g[l, 2], w_in_b, w_out_b, 2, s,
                                       cast_next=(w_ffn_in, w_ffn_out, l + 1, 0))
    return xf.reshape(b, s, d)
```

```python
import functools
import math

import jax
import jax.numpy as jnp
from jax import lax
from jax.experimental import pallas as pl
from jax.experimental.pallas import tpu as pltpu

EPS = 1e-6
N_HEADS = 16
HEAD_DIM = 128
CONV_WIDTH = 3
LOG2E = math.log2(math.e)
LANES = 128
SUBLANES = 8
MXU_DIM = 256
VMEM_LIMIT_BYTES = 59 * 1024 * 1024

F32 = jnp.float32
BF16 = jnp.bfloat16


def _params(semantics):
    return pltpu.CompilerParams(dimension_semantics=semantics,
                                vmem_limit_bytes=VMEM_LIMIT_BYTES)


def _dot(a, b):
    return jnp.dot(a, b, preferred_element_type=F32)


def _rmsnorm(x, g):
    return x * lax.rsqrt(jnp.mean(x * x, axis=-1, keepdims=True) + EPS) * g


NORM_ROWS = 16
NORM_UNROLL = 8
def _modulated_norm_to(x_ref, *targets):
    params = [(h_ref, g_ref[...] * (1.0 + scale), shift) for h_ref, g_ref, shift, scale in targets]

    def body(r, _):
        rows = pl.ds(pl.multiple_of(r * NORM_ROWS, NORM_ROWS), NORM_ROWS)
        x = x_ref[rows, :]
        xhat = x * lax.rsqrt(jnp.mean(x * x, axis=-1, keepdims=True) + EPS)
        for h_ref, gain, shift in params:
            h_ref[rows, :] = (xhat * gain + shift).astype(BF16)
        return 0
    lax.fori_loop(0, x_ref.shape[0] // NORM_ROWS, body, 0, unroll=NORM_UNROLL)


def _residual_to(o_ref, x_ref, coef, final_g=None, rs_ref=None):
    if final_g is None:
        o_ref[...] = x_ref[...] + coef * o_ref[...]
        return
    n = x_ref.shape[0] // NORM_ROWS

    def rows_of(r):
        return pl.ds(pl.multiple_of(r * NORM_ROWS, NORM_ROWS), NORM_ROWS)

    def stats(r, _):
        y = x_ref[rows_of(r), :] + coef * o_ref[rows_of(r), :]
        rs = lax.rsqrt(jnp.mean(y * y, axis=-1, keepdims=True) + EPS)
        rs_ref[rows_of(r), :] = jnp.broadcast_to(rs, (NORM_ROWS, LANES))
        return 0
    lax.fori_loop(0, n, stats, 0, unroll=NORM_UNROLL)

    def scale(r, _):
        y = x_ref[rows_of(r), :] + coef * o_ref[rows_of(r), :]
        o_ref[rows_of(r), :] = y * rs_ref[rows_of(r), 0:1] * final_g
        return 0
    lax.fori_loop(0, n, scale, 0, unroll=NORM_UNROLL)


def _ada_rows(ada_ref, sub):
    return tuple(ada_ref[0, 3 * sub + k:3 * sub + k + 1, :] for k in range(3))


def _ada_kernel(c_ref, w_ref, b_ref, o_ref):
    c = c_ref[...]
    cond = c * jax.nn.sigmoid(c)
    o_ref[...] = _dot(cond.astype(BF16), w_ref[...].astype(BF16)) + b_ref[...]


def _ada(c_pad, w, b, l, tn=1024):
    nl, d, n = w.shape
    return pl.pallas_call(
        _ada_kernel,
        grid=(n // tn,),
        in_specs=[pl.BlockSpec((SUBLANES, d), lambda j: (0, 0)),
                  pl.BlockSpec((None, d, tn), lambda j: (l, 0, j)),
                  pl.BlockSpec((None, 1, tn), lambda j: (l, 0, j))],
        out_specs=pl.BlockSpec((SUBLANES, tn), lambda j: (0, j)),
        out_shape=jax.ShapeDtypeStruct((SUBLANES, n), F32),
        compiler_params=_params(("parallel",)),
        name="ada",
    )(c_pad, w, b.reshape(nl, 1, n))


def _row_cast_specs(w, lead, n_steps, step_of, row_tile=16):
    rows, cols = w.shape[-2:]
    cols -= cols % LANES
    n_tiles = rows // row_tile
    assert rows % row_tile == 0 and n_tiles <= n_steps

    def tile(*g):
        return jnp.minimum(step_of(*g), n_tiles - 1)
    in_spec = pl.BlockSpec((None,) * len(lead) + (row_tile, cols), lambda *g: tuple(lead) + (tile(*g), 0))
    out_spec = pl.BlockSpec((row_tile, cols), lambda *g: (tile(*g), 0))
    return in_spec, out_spec, jax.ShapeDtypeStruct((rows, cols), BF16)


def _cast_tiles(in_refs, out_refs):
    for src, dst in zip(in_refs, out_refs):
        dst[...] = src[...].astype(BF16)


def _ffn_kernel(x_ref, ada_ref, g_ref, wg_ref, wu_ref, wo_ref, *rest, sub, final, n_cast):
    rest = list(rest)
    fg_ref = rest.pop(0) if final else None
    cast_in = [rest.pop(0) for _ in range(n_cast)]
    o_ref = rest.pop(0)
    cast_out = [rest.pop(0) for _ in range(n_cast)]
    h_ref = rest.pop(0)
    rs_ref = rest.pop(0) if final else None
    f = pl.program_id(1)

    shift, scale, gate = _ada_rows(ada_ref, sub)

    @pl.when(f == 0)
    def _():
        _modulated_norm_to(x_ref, (h_ref, g_ref, shift, scale))
        o_ref[...] = jnp.zeros_like(o_ref)

    _cast_tiles(cast_in, cast_out)
    h = h_ref[...]
    a = _dot(h, wg_ref[...])
    b = _dot(h, wu_ref[...])
    act = (a * jax.nn.sigmoid(a) * b).astype(BF16)
    o_ref[...] += _dot(act, wo_ref[...])

    @pl.when(f == pl.num_programs(1) - 1)
    def _():
        if final:
            _residual_to(o_ref, x_ref, 0.5 * gate, fg_ref[...], rs_ref)
        else:
            _residual_to(o_ref, x_ref, 0.5 * gate)


def _ffn(x, ada, g, w_in, w_out, sub, seq, final_g=None, cast_next=None, casts=(), tf=512):
    t, d = x.shape
    ff = w_out.shape[0]
    nf = ff // tf
    final = final_g is not None
    tm = 1024
    per_batch = seq // tm
    ni = t // tm
    in_specs = [pl.BlockSpec((tm, d), lambda i, f: (i, 0)),
                pl.BlockSpec((1, 9, d), lambda i, f: (i // per_batch, 0, 0)),
                pl.BlockSpec((1, d), lambda i, f: (0, 0)),
                pl.BlockSpec((d, tf), lambda i, f: (0, f)),
                pl.BlockSpec((d, tf), lambda i, f: (0, nf + f)),
                pl.BlockSpec((tf, d), lambda i, f: (f, 0))]
    args = [x, ada, g.reshape(1, d), w_in, w_in, w_out]
    out_specs = [pl.BlockSpec((tm, d), lambda i, f: (i, 0))]
    out_shape = [jax.ShapeDtypeStruct((t, d), F32)]
    if final:
        in_specs.append(pl.BlockSpec((1, d), lambda i, f: (0, 0)))
        args.append(final_g.reshape(1, d))
    n_cast = 0
    if cast_next:
        w_in32, w_out32, l2, idx2 = cast_next
        in_tile = (d // ni, 2 * ff // nf)
        out_tile = (ff // (ni * nf), d)
        in_specs += [pl.BlockSpec((None, None) + in_tile, lambda i, f: (l2, idx2, i, f)),
                     pl.BlockSpec((None, None) + out_tile, lambda i, f: (l2, idx2, i * nf + f, 0))]
        args += [w_in32, w_out32]
        out_specs += [pl.BlockSpec(in_tile, lambda i, f: (i, f)),
                      pl.BlockSpec(out_tile, lambda i, f: (i * nf + f, 0))]
        out_shape += [jax.ShapeDtypeStruct((d, 2 * ff), BF16), jax.ShapeDtypeStruct((ff, d), BF16)]
        n_cast += 2
    for w, lead in casts:
        i_spec, o_spec, o_shape = _row_cast_specs(w, lead, ni * nf, lambda i, f: i * nf + f)
        in_specs.append(i_spec)
        args.append(w)
        out_specs.append(o_spec)
        out_shape.append(o_shape)
        n_cast += 1
    outs = pl.pallas_call(
        functools.partial(_ffn_kernel, sub=sub, final=final, n_cast=n_cast),
        grid=(ni, nf),
        in_specs=in_specs,
        out_specs=out_specs,
        out_shape=out_shape,
        scratch_shapes=[pltpu.VMEM((tm, d), BF16)] + ([pltpu.VMEM((tm, LANES), F32)] if final else []),
        compiler_params=_params(("parallel", "arbitrary")),
        name="ffn_final" if final else "ffn",
    )(*args)
    return outs if n_cast else outs[0]


def _conv_kernel(x_ref, ada_ref, g_ref, wb_ref, wc_ref, wx_ref, cw_ref, cb_ref, wo_ref,
                 *rest, per_batch, n_cast, nc):
    cast_in, (o_ref, *cast_out), (h_ref, tail_ref, v_ref) = (rest[:n_cast], rest[n_cast:2 * n_cast + 1],
                                                             rest[2 * n_cast + 1:])
    i = pl.program_id(0)
    c = pl.program_id(1)
    tm = x_ref.shape[0]

    shift, scale, gate = _ada_rows(ada_ref, 1)

    @pl.when(c == 0)
    def _():
        _modulated_norm_to(x_ref, (h_ref, g_ref, shift, scale))
        o_ref[...] = jnp.zeros_like(o_ref)

        @pl.when(i == 0)
        def _():
            tail_ref[...] = jnp.zeros_like(tail_ref)

    def project():
        h = h_ref[...]
        return _dot(h, wb_ref[...]), _dot(h, wc_ref[...]), _dot(h, wx_ref[...])

    def gated_conv(chunk, bg, cg, xv, slot):
        u = cg * xv
        tail = jnp.where(i % per_batch == 0, 0.0, tail_ref[chunk])
        p1 = tail[SUBLANES - 1:SUBLANES, :]
        p2 = tail[SUBLANES - 2:SUBLANES - 1, :]
        row = lax.broadcasted_iota(jnp.int32, u.shape, 0)
        u1 = jnp.where(row == 0, p1, pltpu.roll(u, 1, 0))
        u2 = jnp.where(row == 0, p2, jnp.where(row == 1, p1, pltpu.roll(u, 2, 0)))
        conv = cw_ref[0:1, :] * u2 + cw_ref[1:2, :] * u1 + cw_ref[2:3, :] * u + cb_ref[...]
        tail_ref[chunk] = u[tm - SUBLANES:, :]
        v_ref[slot] = (bg * conv).astype(BF16)

    for step in range(nc + 1):
        @pl.when(c == step)
        def _(step=step):
            _cast_tiles(cast_in, cast_out)
            proj = project() if step < nc else None
            if step > 0:
                o_ref[...] += _dot(v_ref[(step - 1) % 2], wo_ref[...])
            if step < nc:
                gated_conv(step, *proj, step % 2)
            else:
                _residual_to(o_ref, x_ref, gate)


def _conv(x, ada, g, w_in, conv_w, conv_b, w_out, l, seq, casts=(), tm=512, tc=512):
    t, d = x.shape
    nc = d // tc
    ni = t // tm
    per_batch = seq // tm

    def proj_chunk(c):
        return jnp.minimum(c, nc - 1)

    def out_chunk(c):
        return jnp.maximum(c - 1, 0)
    in_specs = [pl.BlockSpec((tm, d), lambda i, c: (i, 0)),
                pl.BlockSpec((1, 9, d), lambda i, c: (i // per_batch, 0, 0)),
                pl.BlockSpec((1, d), lambda i, c: (0, 0)),
                pl.BlockSpec((d, tc), lambda i, c: (0, proj_chunk(c))),
                pl.BlockSpec((d, tc), lambda i, c: (0, nc + proj_chunk(c))),
                pl.BlockSpec((d, tc), lambda i, c: (0, 2 * nc + proj_chunk(c))),
                pl.BlockSpec((None, CONV_WIDTH, tc), lambda i, c: (l, 0, proj_chunk(c))),
                pl.BlockSpec((None, 1, tc), lambda i, c: (l, 0, proj_chunk(c))),
                pl.BlockSpec((tc, d), lambda i, c: (out_chunk(c), 0))]
    args = [x, ada, g.reshape(1, d), w_in, w_in, w_in, conv_w, conv_b, w_out]
    out_specs = [pl.BlockSpec((tm, d), lambda i, c: (i, 0))]
    out_shape = [jax.ShapeDtypeStruct((t, d), F32)]
    for w, lead in casts:
        i_spec, o_spec, o_shape = _row_cast_specs(w, lead, ni * (nc + 1), lambda i, c: i * (nc + 1) + c)
        in_specs.append(i_spec)
        args.append(w)
        out_specs.append(o_spec)
        out_shape.append(o_shape)
    outs = pl.pallas_call(
        functools.partial(_conv_kernel, per_batch=per_batch, n_cast=len(casts), nc=nc),
        grid=(ni, nc + 1),
        in_specs=in_specs,
        out_specs=out_specs,
        out_shape=out_shape,
        scratch_shapes=[pltpu.VMEM((tm, d), BF16),
                        pltpu.VMEM((nc, SUBLANES, tc), F32),
                        pltpu.VMEM((2, tm, tc), BF16)],
        compiler_params=_params(("arbitrary", "arbitrary")),
        name="conv",
    )(*args)
    return outs if casts else outs[0]


def _qkv_kernel(x_ref, ada_ref, adakv_ref, g_ref, gkv_ref, wq_ref, wk_ref, wv_ref, wf_ref, bf_ref,
                q_ref, k_ref, v_ref, fc_ref, hq_ref, hkv_ref, carry_ref, *, per_batch, q_scale):
    i = pl.program_id(0)
    c = pl.program_id(1)
    tm = x_ref.shape[0]

    @pl.when(c == 0)
    def _():
        shift, scale, _ = _ada_rows(ada_ref, 1)
        _modulated_norm_to(x_ref, (hq_ref, g_ref, shift, scale),
                           (hkv_ref, gkv_ref, adakv_ref[0, 0:1, :], adakv_ref[0, 1:2, :]))
        hkv = hkv_ref[...]

        @pl.when(i % per_batch == 0)
        def _():
            carry_ref[...] = jnp.zeros_like(carry_ref)

        zf = _dot(hkv, wf_ref[...]) + bf_ref[...]
        ls = jnp.minimum(zf, 0.0) - jnp.log1p(jnp.exp(-jnp.abs(zf)))
        hi = ls.astype(BF16)
        r1 = ls - hi.astype(F32)
        mid = r1.astype(BF16)
        lo = (r1 - mid.astype(F32)).astype(BF16)
        rr = lax.broadcasted_iota(jnp.int32, (tm, tm), 0)
        cc = lax.broadcasted_iota(jnp.int32, (tm, tm), 1)
        tri = (rr >= cc).astype(BF16)
        parts = _dot(tri, jnp.concatenate([hi, mid, lo], axis=1))
        cum = (parts[:, :LANES] + parts[:, LANES:2 * LANES]) + parts[:, 2 * LANES:] + carry_ref[0:1, :]
        carry_ref[...] = jnp.broadcast_to(cum[tm - 1:tm, :], carry_ref.shape)
        fc_ref[...] = cum

    hkv = hkv_ref[...]
    q_ref[...] = (_dot(hq_ref[...], wq_ref[...]) * q_scale).astype(BF16)
    kc = _dot(hkv, wk_ref[...])
    cum = fc_ref[...]
    lane = lax.broadcasted_iota(jnp.int32, (tm, LANES), 1)
    for hh in range(kc.shape[1] // HEAD_DIM):
        head = c * (kc.shape[1] // HEAD_DIM) + hh
        col = jnp.sum(jnp.where(lane == head, cum, 0.0), axis=1, keepdims=True) * LOG2E
        hi, mid, lo = _split3(col)
        aug = jnp.where(lane < 3, 1.0,
                        jnp.where(lane == 3, -hi, jnp.where(lane == 4, -mid, jnp.where(lane == 5, -lo, 0.0))))
        k_ref[:, 2 * hh * HEAD_DIM:(2 * hh + 1) * HEAD_DIM] = kc[:, hh * HEAD_DIM:(hh + 1) * HEAD_DIM].astype(BF16)
        k_ref[:, (2 * hh + 1) * HEAD_DIM:(2 * hh + 2) * HEAD_DIM] = aug.astype(BF16)
    v_ref[0] = _dot(hkv, wv_ref[...]).T.astype(BF16)


def _qkv(x, ada, ada_kv, g, gkv, w_q, w_kv, w_f, b_f, j, seq, tm=1024, tn=512):
    t, d = x.shape
    nn = d // tn
    per_batch = seq // tm
    q_scale = LOG2E / math.sqrt(HEAD_DIM)
    row_blk = pl.BlockSpec((tm, tn), lambda i, c: (i, c))
    return pl.pallas_call(
        functools.partial(_qkv_kernel, per_batch=per_batch, q_scale=q_scale),
        grid=(t // tm, nn),
        in_specs=[pl.BlockSpec((tm, d), lambda i, c: (i, 0)),
                  pl.BlockSpec((1, 9, d), lambda i, c: (i // per_batch, 0, 0)),
                  pl.BlockSpec((1, 2, d), lambda i, c: (i // per_batch, 0, 0)),
                  pl.BlockSpec((1, d), lambda i, c: (0, 0)),
                  pl.BlockSpec((1, d), lambda i, c: (0, 0)),
                  pl.BlockSpec((None, d, tn), lambda i, c: (j, 0, c)),
                  pl.BlockSpec((d, tn), lambda i, c: (0, c)),
                  pl.BlockSpec((d, tn), lambda i, c: (0, nn + c)),
                  pl.BlockSpec((d, LANES), lambda i, c: (0, 0)),
                  pl.BlockSpec((1, LANES), lambda i, c: (0, 0))],
        out_specs=[row_blk,
                   pl.BlockSpec((tm, 2 * tn), lambda i, c: (i, c)),
                   pl.BlockSpec((1, tn, tm), lambda i, c: (i // per_batch, c, i % per_batch)),
                   pl.BlockSpec((tm, LANES), lambda i, c: (i, 0))],
        out_shape=[jax.ShapeDtypeStruct((t, d), BF16),
                   jax.ShapeDtypeStruct((t, 2 * d), BF16),
                   jax.ShapeDtypeStruct((t // seq, d, seq), BF16),
                   jax.ShapeDtypeStruct((t, LANES), F32)],
        scratch_shapes=[pltpu.VMEM((tm, d), BF16),
                        pltpu.VMEM((tm, d), BF16),
                        pltpu.VMEM((SUBLANES, LANES), F32)],
        compiler_params=_params(("arbitrary", "arbitrary")),
        name="qkv",
    )(x, ada, ada_kv, g.reshape(1, d), gkv.reshape(1, d), w_q, w_kv, w_kv, w_f, b_f)


def _split3(f):
    hi = f.astype(BF16)
    r1 = f - hi.astype(F32)
    mid = r1.astype(BF16)
    lo = (r1 - mid.astype(F32)).astype(BF16)
    return hi.astype(F32), mid.astype(F32), lo.astype(F32)


def _attn_kernel(q_ref, k_ref, vt_ref, fc_ref, o_ref, st_ref, *, tq, tk, hp):
    hg = pl.program_id(1)
    i = pl.program_id(2)

    fblk = fc_ref[0]
    lane = lax.broadcasted_iota(jnp.int32, (tq, LANES), 1)
    qts = []
    for hh in range(hp):
        fq = jnp.sum(jnp.where(lane == hg * hp + hh, fblk, 0.0), axis=1, keepdims=True) * LOG2E
        hi, mid, lo = _split3(fq)
        aq = jnp.where(lane == 0, hi,
                       jnp.where(lane == 1, mid,
                                 jnp.where(lane == 2, lo, jnp.where(lane < 6, 1.0, 0.0))))
        qh = q_ref[0, :, hh * HEAD_DIM:(hh + 1) * HEAD_DIM].astype(F32)
        qts.append(jnp.concatenate([qh.T, aq.T], axis=0).astype(BF16))

    def scores(j, slot, lo=0):
        start = pl.multiple_of(j * tk, tk)
        for hh in range(hp):
            kj = k_ref[0, pl.ds(start, tk), 2 * hh * HEAD_DIM:2 * (hh + 1) * HEAD_DIM]
            st_ref[slot, hh, :, lo:] = _dot(kj, qts[hh][:, lo:])

    def softmax_pv(j, slot, carry, diag):
        start = pl.multiple_of(j * tk, tk)
        lo = 0 if diag is None else diag * tk
        out = []
        for hh in range(hp):
            m0, l0, acc0 = carry[hh]
            m, l, acc = m0[:, lo:], l0[:, lo:], acc0[:, lo:]
            st = st_ref[slot, hh, :, lo:]
            if diag is not None:
                key = lax.broadcasted_iota(jnp.int32, st.shape, 0)
                qry = lax.broadcasted_iota(jnp.int32, st.shape, 1)
                st = jnp.where(key <= qry, st, -jnp.inf)
            m_new = jnp.maximum(m, jnp.max(st, axis=0, keepdims=True))
            alpha = jnp.exp2(m - m_new)
            pt = jnp.exp2(st - m_new)
            l = alpha * l + jnp.sum(pt, axis=0, keepdims=True)
            vj = vt_ref[0, hh * HEAD_DIM:(hh + 1) * HEAD_DIM, pl.ds(start, tk)]
            acc = alpha * acc + _dot(vj, pt.astype(BF16))
            if lo:
                m_new, l, acc = (jnp.concatenate([old[:, :lo], new], axis=1)
                                 for old, new in ((m0, m_new), (l0, l), (acc0, acc)))
            out.append((m_new, l, acc))
        return tuple(out)

    def pair(jj, carry):
        a = 2 * jj
        scores(a + 1, 1)
        carry = softmax_pv(a, 0, carry, None)
        scores(a + 2, 0)
        return softmax_pv(a + 1, 1, carry, None)

    init = tuple((jnp.full((1, tq), -jnp.inf, F32), jnp.zeros((1, tq), F32),
                  jnp.zeros((HEAD_DIM, tq), F32)) for _ in range(hp))
    scores(0, 0)
    carry = lax.fori_loop(0, i, pair, init)
    scores(2 * i + 1, 1, lo=tk)
    carry = softmax_pv(2 * i, 0, carry, 0)
    carry = softmax_pv(2 * i + 1, 1, carry, 1)
    for hh in range(hp):
        _, l, acc = carry[hh]
        o_ref[0, :, hh * HEAD_DIM:(hh + 1) * HEAD_DIM] = (acc / l).T.astype(BF16)


def _attn(q, kp, vt, fc, tq=1024, hp=2):
    b, s, d = q.shape
    w = hp * HEAD_DIM
    tk = tq // 2
    return pl.pallas_call(
        functools.partial(_attn_kernel, tq=tq, tk=tk, hp=hp),
        grid=(b, N_HEADS // hp, s // tq),
        in_specs=[pl.BlockSpec((1, tq, w), lambda bb, h, i: (bb, i, h)),
                  pl.BlockSpec((1, s, 2 * w), lambda bb, h, i: (bb, 0, h)),
                  pl.BlockSpec((1, w, s), lambda bb, h, i: (bb, h, 0)),
                  pl.BlockSpec((1, tq, LANES), lambda bb, h, i: (bb, i, 0))],
        out_specs=pl.BlockSpec((1, tq, w), lambda bb, h, i: (bb, i, h)),
        out_shape=jax.ShapeDtypeStruct((b, s, d), BF16),
        scratch_shapes=[pltpu.VMEM((2, hp, tk, tq), F32)],
        compiler_params=_params(("parallel", "parallel", "parallel")),
        name="attn",
    )(q, kp, vt, fc)


def _oproj_kernel(x_ref, a_ref, ada_ref, w_ref, o_ref):
    gate = ada_ref[0, 5:6, :]
    o_ref[...] = x_ref[...] + gate * _dot(a_ref[...], w_ref[...])


def _oproj(x, attn, ada, w_o, j, seq, tm=1024):
    t, d = x.shape
    per_batch = seq // tm
    return pl.pallas_call(
        _oproj_kernel,
        grid=(t // tm,),
        in_specs=[pl.BlockSpec((tm, d), lambda i: (i, 0)),
                  pl.BlockSpec((tm, d), lambda i: (i, 0)),
                  pl.BlockSpec((1, 9, d), lambda i: (i // per_batch, 0, 0)),
                  pl.BlockSpec((None, d, d), lambda i: (j, 0, 0), pipeline_mode=pl.Buffered(1))],
        out_specs=pl.BlockSpec((tm, d), lambda i: (i, 0)),
        out_shape=jax.ShapeDtypeStruct((t, d), F32),
        compiler_params=_params(("parallel",)),
        name="oproj",
    )(x, attn, ada, w_o)


def kernel(x, c, norm_g, w_ada, b_ada, w_ffn_in, w_ffn_out, w_conv_in, conv_w, conv_b, w_conv_out,
           kv_norm_g, w_ada_kv, b_ada_kv, w_kvf, b_fgate, w_q, w_o, final_g):
    b, s, d = x.shape
    depth = norm_g.shape[0]
    n_a = w_conv_in.shape[0]
    t = b * s
    assert w_q.shape[0] == 1 and depth == n_a + 1

    w_in_b = w_ffn_in[0, 0].astype(BF16)
    w_out_b = w_ffn_out[0, 0].astype(BF16)
    w_f_b = jnp.pad(w_kvf[:, 2 * d:], ((0, 0), (0, LANES - N_HEADS))).astype(BF16)
    b_f = jnp.pad(b_fgate, (0, LANES - N_HEADS)).reshape(1, LANES)
    conv_b3 = conv_b.reshape(n_a, 1, d)
    attn_casts = [(w_q, (0,)), (w_o, (0,)), (w_kvf, ())]

    c_pad = jnp.pad(c, ((0, SUBLANES - b), (0, 0)))
    xf = x.reshape(t, d)
    for l in range(depth):
        ada = _ada(c_pad, w_ada, b_ada, l)[:b].reshape(b, 9, d)
        if l < n_a:
            mixer_casts = [(w_conv_in, (l,)), (w_conv_out, (l,))]
        else:
            mixer_casts = attn_casts if n_a == 0 else []
        xf, w_in_b, w_out_b, *mixer_w = _ffn(xf, ada, norm_g[l, 0], w_in_b, w_out_b, 0, s,
                                             cast_next=(w_ffn_in, w_ffn_out, l, 1), casts=mixer_casts)
        if l < n_a:
            w_conv_in_b, w_conv_out_b = mixer_w
            if l == n_a - 1:
                xf, *attn_w = _conv(xf, ada, norm_g[l, 1], w_conv_in_b, conv_w, conv_b3, w_conv_out_b, l, s,
                                    casts=attn_casts)
            else:
                xf = _conv(xf, ada, norm_g[l, 1], w_conv_in_b, conv_w, conv_b3, w_conv_out_b, l, s)
        else:
            if n_a == 0:
                attn_w = mixer_w
            w_q_b, w_o_b, w_kv_b = attn_w
            ada_kv = _ada(c_pad, w_ada_kv[None], b_ada_kv[None], 0)[:b].reshape(b, 2, d)
            q, kp, vt, fc = _qkv(xf, ada, ada_kv, norm_g[l, 1], kv_norm_g, w_q_b[None], w_kv_b, w_f_b, b_f, 0, s)
            attn = _attn(q.reshape(b, s, d), kp.reshape(b, s, 2 * d), vt, fc.reshape(b, s, LANES))
            xf = _oproj(xf, attn.reshape(t, d), ada, w_o_b[None], 0, s)
        if l == depth - 1:
            xf = _ffn(xf, ada, norm_g[l, 2], w_in_b, w_out_b, 2, s, final_g=final_g)
        else:
            xf, w_in_b, w_out_b = _ffn(xf, ada, norm_g[l, 2], w_in_b, w_out_b, 2, s,
                                       cast_next=(w_ffn_in, w_ffn_out, l + 1, 0))
    return xf.reshape(b, s, d)
```

```python
import functools
import math

import jax
import jax.numpy as jnp
from jax import lax
from jax.experimental import pallas as pl
from jax.experimental.pallas import tpu as pltpu

EPS = 1e-6
N_HEADS = 16
HEAD_DIM = 128
CONV_WIDTH = 3
LOG2E = math.log2(math.e)
LANES = 128
SUBLANES = 8
MXU_DIM = 256
VMEM_LIMIT_BYTES = 59 * 1024 * 1024

F32 = jnp.float32
BF16 = jnp.bfloat16


def _params(semantics):
    return pltpu.CompilerParams(dimension_semantics=semantics,
                                vmem_limit_bytes=VMEM_LIMIT_BYTES)


def _dot(a, b):
    return jnp.dot(a, b, preferred_element_type=F32)


def _rmsnorm(x, g):
    return x * lax.rsqrt(jnp.mean(x * x, axis=-1, keepdims=True) + EPS) * g


NORM_ROWS = 16
NORM_UNROLL = 8
def _modulated_norm_to(x_ref, *targets):
    params = [(h_ref, g_ref[...] * (1.0 + scale), shift) for h_ref, g_ref, shift, scale in targets]

    def body(r, _):
        rows = pl.ds(pl.multiple_of(r * NORM_ROWS, NORM_ROWS), NORM_ROWS)
        x = x_ref[rows, :]
        xhat = x * lax.rsqrt(jnp.mean(x * x, axis=-1, keepdims=True) + EPS)
        for h_ref, gain, shift in params:
            h_ref[rows, :] = (xhat * gain + shift).astype(BF16)
        return 0
    lax.fori_loop(0, x_ref.shape[0] // NORM_ROWS, body, 0, unroll=NORM_UNROLL)


def _residual_to(o_ref, x_ref, coef, final_g=None, rs_ref=None):
    if final_g is None:
        o_ref[...] = x_ref[...] + coef * o_ref[...]
        return
    n = x_ref.shape[0] // NORM_ROWS

    def rows_of(r):
        return pl.ds(pl.multiple_of(r * NORM_ROWS, NORM_ROWS), NORM_ROWS)

    def stats(r, _):
        y = x_ref[rows_of(r), :] + coef * o_ref[rows_of(r), :]
        rs = lax.rsqrt(jnp.mean(y * y, axis=-1, keepdims=True) + EPS)
        rs_ref[rows_of(r), :] = jnp.broadcast_to(rs, (NORM_ROWS, LANES))
        return 0
    lax.fori_loop(0, n, stats, 0, unroll=NORM_UNROLL)

    def scale(r, _):
        y = x_ref[rows_of(r), :] + coef * o_ref[rows_of(r), :]
        o_ref[rows_of(r), :] = y * rs_ref[rows_of(r), 0:1] * final_g
        return 0
    lax.fori_loop(0, n, scale, 0, unroll=NORM_UNROLL)


def _ada_rows(ada_ref, sub):
    return tuple(ada_ref[0, 3 * sub + k:3 * sub + k + 1, :] for k in range(3))


def _ada_kernel(c_ref, w_ref, b_ref, o_ref):
    c = c_ref[...]
    cond = c * jax.nn.sigmoid(c)
    o_ref[...] = _dot(cond.astype(BF16), w_ref[...].astype(BF16)) + b_ref[...]


def _ada(c_pad, w, b, l, tn=1024):
    nl, d, n = w.shape
    return pl.pallas_call(
        _ada_kernel,
        grid=(n // tn,),
        in_specs=[pl.BlockSpec((SUBLANES, d), lambda j: (0, 0)),
                  pl.BlockSpec((None, d, tn), lambda j: (l, 0, j)),
                  pl.BlockSpec((None, 1, tn), lambda j: (l, 0, j))],
        out_specs=pl.BlockSpec((SUBLANES, tn), lambda j: (0, j)),
        out_shape=jax.ShapeDtypeStruct((SUBLANES, n), F32),
        compiler_params=_params(("parallel",)),
        name="ada",
    )(c_pad, w, b.reshape(nl, 1, n))


def _row_cast_specs(w, lead, n_steps, step_of, row_tile=16):
    rows, cols = w.shape[-2:]
    cols -= cols % LANES
    n_tiles = rows // row_tile
    assert rows % row_tile == 0 and n_tiles <= n_steps

    def tile(*g):
        return jnp.minimum(step_of(*g), n_tiles - 1)
    in_spec = pl.BlockSpec((None,) * len(lead) + (row_tile, cols), lambda *g: tuple(lead) + (tile(*g), 0))
    out_spec = pl.BlockSpec((row_tile, cols), lambda *g: (tile(*g), 0))
    return in_spec, out_spec, jax.ShapeDtypeStruct((rows, cols), BF16)


def _cast_tiles(in_refs, out_refs):
    for src, dst in zip(in_refs, out_refs):
        dst[...] = src[...].astype(BF16)


def _ffn_kernel(x_ref, ada_ref, g_ref, wg_ref, wu_ref, wo_ref, *rest, sub, final, n_cast):
    rest = list(rest)
    fg_ref = rest.pop(0) if final else None
    cast_in = [rest.pop(0) for _ in range(n_cast)]
    o_ref = rest.pop(0)
    cast_out = [rest.pop(0) for _ in range(n_cast)]
    h_ref = rest.pop(0)
    rs_ref = rest.pop(0) if final else None
    f = pl.program_id(1)

    shift, scale, gate = _ada_rows(ada_ref, sub)

    @pl.when(f == 0)
    def _():
        _modulated_norm_to(x_ref, (h_ref, g_ref, shift, scale))
        o_ref[...] = jnp.zeros_like(o_ref)

    _cast_tiles(cast_in, cast_out)
    h = h_ref[...]
    a = _dot(h, wg_ref[...])
    b = _dot(h, wu_ref[...])
    act = (a * jax.nn.sigmoid(a) * b).astype(BF16)
    o_ref[...] += _dot(act, wo_ref[...])

    @pl.when(f == pl.num_programs(1) - 1)
    def _():
        if final:
            _residual_to(o_ref, x_ref, 0.5 * gate, fg_ref[...], rs_ref)
        else:
            _residual_to(o_ref, x_ref, 0.5 * gate)


def _ffn(x, ada, g, w_in, w_out, sub, seq, final_g=None, cast_next=None, casts=(), tf=512):
    t, d = x.shape
    ff = w_out.shape[0]
    nf = ff // tf
    final = final_g is not None
    tm = 1024
    per_batch = seq // tm
    ni = t // tm
    in_specs = [pl.BlockSpec((tm, d), lambda i, f: (i, 0)),
                pl.BlockSpec((1, 9, d), lambda i, f: (i // per_batch, 0, 0)),
                pl.BlockSpec((1, d), lambda i, f: (0, 0)),
                pl.BlockSpec((d, tf), lambda i, f: (0, f)),
                pl.BlockSpec((d, tf), lambda i, f: (0, nf + f)),
                pl.BlockSpec((tf, d), lambda i, f: (f, 0))]
    args = [x, ada, g.reshape(1, d), w_in, w_in, w_out]
    out_specs = [pl.BlockSpec((tm, d), lambda i, f: (i, 0))]
    out_shape = [jax.ShapeDtypeStruct((t, d), F32)]
    if final:
        in_specs.append(pl.BlockSpec((1, d), lambda i, f: (0, 0)))
        args.append(final_g.reshape(1, d))
    n_cast = 0
    if cast_next:
        w_in32, w_out32, l2, idx2 = cast_next
        in_tile = (d // ni, 2 * ff // nf)
        out_tile = (ff // (ni * nf), d)
        in_specs += [pl.BlockSpec((None, None) + in_tile, lambda i, f: (l2, idx2, i, f)),
                     pl.BlockSpec((None, None) + out_tile, lambda i, f: (l2, idx2, i * nf + f, 0))]
        args += [w_in32, w_out32]
        out_specs += [pl.BlockSpec(in_tile, lambda i, f: (i, f)),
                      pl.BlockSpec(out_tile, lambda i, f: (i * nf + f, 0))]
        out_shape += [jax.ShapeDtypeStruct((d, 2 * ff), BF16), jax.ShapeDtypeStruct((ff, d), BF16)]
        n_cast += 2
    for w, lead in casts:
        i_spec, o_spec, o_shape = _row_cast_specs(w, lead, ni * nf, lambda i, f: i * nf + f)
        in_specs.append(i_spec)
        args.append(w)
        out_specs.append(o_spec)
        out_shape.append(o_shape)
        n_cast += 1
    outs = pl.pallas_call(
        functools.partial(_ffn_kernel, sub=sub, final=final, n_cast=n_cast),
        grid=(ni, nf),
        in_specs=in_specs,
        out_specs=out_specs,
        out_shape=out_shape,
        scratch_shapes=[pltpu.VMEM((tm, d), BF16)] + ([pltpu.VMEM((tm, LANES), F32)] if final else []),
        compiler_params=_params(("parallel", "arbitrary")),
        name="ffn_final" if final else "ffn",
    )(*args)
    return outs if n_cast else outs[0]


def _conv_kernel(x_ref, ada_ref, g_ref, wb_ref, wc_ref, wx_ref, cw_ref, cb_ref, wo_ref,
                 *rest, per_batch, n_cast):
    cast_in, (o_ref, *cast_out), (h_ref, tail_ref) = (rest[:n_cast], rest[n_cast:2 * n_cast + 1],
                                                      rest[2 * n_cast + 1:])
    i = pl.program_id(0)
    c = pl.program_id(1)
    tm = x_ref.shape[0]

    shift, scale, gate = _ada_rows(ada_ref, 1)

    @pl.when(c == 0)
    def _():
        _modulated_norm_to(x_ref, (h_ref, g_ref, shift, scale))
        o_ref[...] = jnp.zeros_like(o_ref)

    @pl.when(i % per_batch == 0)
    def _():
        tail_ref[c] = jnp.zeros(tail_ref.shape[1:], F32)

    _cast_tiles(cast_in, cast_out)
    h = h_ref[...]
    bg = _dot(h, wb_ref[...])
    cg = _dot(h, wc_ref[...])
    xv = _dot(h, wx_ref[...])
    u = cg * xv
    tail = tail_ref[c]
    p1 = tail[SUBLANES - 1:SUBLANES, :]
    p2 = tail[SUBLANES - 2:SUBLANES - 1, :]
    row = lax.broadcasted_iota(jnp.int32, u.shape, 0)
    u1 = jnp.where(row == 0, p1, pltpu.roll(u, 1, 0))
    u2 = jnp.where(row == 0, p2, jnp.where(row == 1, p1, pltpu.roll(u, 2, 0)))
    conv = cw_ref[0:1, :] * u2 + cw_ref[1:2, :] * u1 + cw_ref[2:3, :] * u + cb_ref[...]
    tail_ref[c] = u[tm - SUBLANES:, :]
    o_ref[...] += _dot((bg * conv).astype(BF16), wo_ref[...])

    @pl.when(c == pl.num_programs(1) - 1)
    def _():
        _residual_to(o_ref, x_ref, gate)


def _conv(x, ada, g, w_in, conv_w, conv_b, w_out, l, seq, casts=(), tm=512, tc=512):
    t, d = x.shape
    nc = d // tc
    ni = t // tm
    per_batch = seq // tm
    in_specs = [pl.BlockSpec((tm, d), lambda i, c: (i, 0)),
                pl.BlockSpec((1, 9, d), lambda i, c: (i // per_batch, 0, 0)),
                pl.BlockSpec((1, d), lambda i, c: (0, 0)),
                pl.BlockSpec((d, tc), lambda i, c: (0, c)),
                pl.BlockSpec((d, tc), lambda i, c: (0, nc + c)),
                pl.BlockSpec((d, tc), lambda i, c: (0, 2 * nc + c)),
                pl.BlockSpec((None, CONV_WIDTH, tc), lambda i, c: (l, 0, c)),
                pl.BlockSpec((None, 1, tc), lambda i, c: (l, 0, c)),
                pl.BlockSpec((tc, d), lambda i, c: (c, 0))]
    args = [x, ada, g.reshape(1, d), w_in, w_in, w_in, conv_w, conv_b, w_out]
    out_specs = [pl.BlockSpec((tm, d), lambda i, c: (i, 0))]
    out_shape = [jax.ShapeDtypeStruct((t, d), F32)]
    for w, lead in casts:
        i_spec, o_spec, o_shape = _row_cast_specs(w, lead, ni * nc, lambda i, c: i * nc + c)
        in_specs.append(i_spec)
        args.append(w)
        out_specs.append(o_spec)
        out_shape.append(o_shape)
    outs = pl.pallas_call(
        functools.partial(_conv_kernel, per_batch=per_batch, n_cast=len(casts)),
        grid=(ni, nc),
        in_specs=in_specs,
        out_specs=out_specs,
        out_shape=out_shape,
        scratch_shapes=[pltpu.VMEM((tm, d), BF16),
                        pltpu.VMEM((nc, SUBLANES, tc), F32)],
        compiler_params=_params(("arbitrary", "arbitrary")),
        name="conv",
    )(*args)
    return outs if casts else outs[0]


def _qkv_kernel(x_ref, ada_ref, adakv_ref, g_ref, gkv_ref, wq_ref, wk_ref, wv_ref, wf_ref, bf_ref,
                q_ref, k_ref, v_ref, fc_ref, hq_ref, hkv_ref, carry_ref, *, per_batch, q_scale):
    i = pl.program_id(0)
    c = pl.program_id(1)
    tm = x_ref.shape[0]

    @pl.when(c == 0)
    def _():
        shift, scale, _ = _ada_rows(ada_ref, 1)
        _modulated_norm_to(x_ref, (hq_ref, g_ref, shift, scale),
                           (hkv_ref, gkv_ref, adakv_ref[0, 0:1, :], adakv_ref[0, 1:2, :]))
        hkv = hkv_ref[...]

        @pl.when(i % per_batch == 0)
        def _():
            carry_ref[...] = jnp.zeros_like(carry_ref)

        zf = _dot(hkv, wf_ref[...]) + bf_ref[...]
        ls = jnp.minimum(zf, 0.0) - jnp.log1p(jnp.exp(-jnp.abs(zf)))
        hi = ls.astype(BF16)
        r1 = ls - hi.astype(F32)
        mid = r1.astype(BF16)
        lo = (r1 - mid.astype(F32)).astype(BF16)
        rr = lax.broadcasted_iota(jnp.int32, (tm, tm), 0)
        cc = lax.broadcasted_iota(jnp.int32, (tm, tm), 1)
        tri = (rr >= cc).astype(BF16)
        parts = _dot(tri, jnp.concatenate([hi, mid, lo], axis=1))
        cum = (parts[:, :LANES] + parts[:, LANES:2 * LANES]) + parts[:, 2 * LANES:] + carry_ref[0:1, :]
        carry_ref[...] = jnp.broadcast_to(cum[tm - 1:tm, :], carry_ref.shape)
        fc_ref[...] = cum

    hkv = hkv_ref[...]
    q_ref[...] = (_dot(hq_ref[...], wq_ref[...]) * q_scale).astype(BF16)
    kc = _dot(hkv, wk_ref[...])
    cum = fc_ref[...]
    lane = lax.broadcasted_iota(jnp.int32, (tm, LANES), 1)
    for hh in range(kc.shape[1] // HEAD_DIM):
        head = c * (kc.shape[1] // HEAD_DIM) + hh
        col = jnp.sum(jnp.where(lane == head, cum, 0.0), axis=1, keepdims=True) * LOG2E
        hi, mid, lo = _split3(col)
        aug = jnp.where(lane < 3, 1.0,
                        jnp.where(lane == 3, -hi, jnp.where(lane == 4, -mid, jnp.where(lane == 5, -lo, 0.0))))
        k_ref[:, 2 * hh * HEAD_DIM:(2 * hh + 1) * HEAD_DIM] = kc[:, hh * HEAD_DIM:(hh + 1) * HEAD_DIM].astype(BF16)
        k_ref[:, (2 * hh + 1) * HEAD_DIM:(2 * hh + 2) * HEAD_DIM] = aug.astype(BF16)
    v_ref[0] = _dot(hkv, wv_ref[...]).T.astype(BF16)


def _qkv(x, ada, ada_kv, g, gkv, w_q, w_kv, w_f, b_f, j, seq, tm=1024, tn=512):
    t, d = x.shape
    nn = d // tn
    per_batch = seq // tm
    q_scale = LOG2E / math.sqrt(HEAD_DIM)
    row_blk = pl.BlockSpec((tm, tn), lambda i, c: (i, c))
    return pl.pallas_call(
        functools.partial(_qkv_kernel, per_batch=per_batch, q_scale=q_scale),
        grid=(t // tm, nn),
        in_specs=[pl.BlockSpec((tm, d), lambda i, c: (i, 0)),
                  pl.BlockSpec((1, 9, d), lambda i, c: (i // per_batch, 0, 0)),
                  pl.BlockSpec((1, 2, d), lambda i, c: (i // per_batch, 0, 0)),
                  pl.BlockSpec((1, d), lambda i, c: (0, 0)),
                  pl.BlockSpec((1, d), lambda i, c: (0, 0)),
                  pl.BlockSpec((None, d, tn), lambda i, c: (j, 0, c)),
                  pl.BlockSpec((d, tn), lambda i, c: (0, c)),
                  pl.BlockSpec((d, tn), lambda i, c: (0, nn + c)),
                  pl.BlockSpec((d, LANES), lambda i, c: (0, 0)),
                  pl.BlockSpec((1, LANES), lambda i, c: (0, 0))],
        out_specs=[row_blk,
                   pl.BlockSpec((tm, 2 * tn), lambda i, c: (i, c)),
                   pl.BlockSpec((1, tn, tm), lambda i, c: (i // per_batch, c, i % per_batch)),
                   pl.BlockSpec((tm, LANES), lambda i, c: (i, 0))],
        out_shape=[jax.ShapeDtypeStruct((t, d), BF16),
                   jax.ShapeDtypeStruct((t, 2 * d), BF16),
                   jax.ShapeDtypeStruct((t // seq, d, seq), BF16),
                   jax.ShapeDtypeStruct((t, LANES), F32)],
        scratch_shapes=[pltpu.VMEM((tm, d), BF16),
                        pltpu.VMEM((tm, d), BF16),
                        pltpu.VMEM((SUBLANES, LANES), F32)],
        compiler_params=_params(("arbitrary", "arbitrary")),
        name="qkv",
    )(x, ada, ada_kv, g.reshape(1, d), gkv.reshape(1, d), w_q, w_kv, w_kv, w_f, b_f)


def _split3(f):
    hi = f.astype(BF16)
    r1 = f - hi.astype(F32)
    mid = r1.astype(BF16)
    lo = (r1 - mid.astype(F32)).astype(BF16)
    return hi.astype(F32), mid.astype(F32), lo.astype(F32)


def _attn_kernel(q_ref, k_ref, vt_ref, fc_ref, o_ref, st_ref, *, tq, tk, hp):
    hg = pl.program_id(1)
    i = pl.program_id(2)

    fblk = fc_ref[0]
    lane = lax.broadcasted_iota(jnp.int32, (tq, LANES), 1)
    qts = []
    for hh in range(hp):
        fq = jnp.sum(jnp.where(lane == hg * hp + hh, fblk, 0.0), axis=1, keepdims=True) * LOG2E
        hi, mid, lo = _split3(fq)
        aq = jnp.where(lane == 0, hi,
                       jnp.where(lane == 1, mid,
                                 jnp.where(lane == 2, lo, jnp.where(lane < 6, 1.0, 0.0))))
        qh = q_ref[0, :, hh * HEAD_DIM:(hh + 1) * HEAD_DIM].astype(F32)
        qts.append(jnp.concatenate([qh.T, aq.T], axis=0).astype(BF16))

    def scores(j, slot, lo=0):
        start = pl.multiple_of(j * tk, tk)
        for hh in range(hp):
            kj = k_ref[0, pl.ds(start, tk), 2 * hh * HEAD_DIM:2 * (hh + 1) * HEAD_DIM]
            st_ref[slot, hh, :, lo:] = _dot(kj, qts[hh][:, lo:])

    def softmax_pv(j, slot, carry, diag):
        start = pl.multiple_of(j * tk, tk)
        lo = 0 if diag is None else diag * tk
        out = []
        for hh in range(hp):
            m0, l0, acc0 = carry[hh]
            m, l, acc = m0[:, lo:], l0[:, lo:], acc0[:, lo:]
            st = st_ref[slot, hh, :, lo:]
            if diag is not None:
                key = lax.broadcasted_iota(jnp.int32, st.shape, 0)
                qry = lax.broadcasted_iota(jnp.int32, st.shape, 1)
                st = jnp.where(key <= qry, st, -jnp.inf)
            m_new = jnp.maximum(m, jnp.max(st, axis=0, keepdims=True))
            alpha = jnp.exp2(m - m_new)
            pt = jnp.exp2(st - m_new)
            l = alpha * l + jnp.sum(pt, axis=0, keepdims=True)
            vj = vt_ref[0, hh * HEAD_DIM:(hh + 1) * HEAD_DIM, pl.ds(start, tk)]
            acc = alpha * acc + _dot(vj, pt.astype(BF16))
            if lo:
                m_new, l, acc = (jnp.concatenate([old[:, :lo], new], axis=1)
                                 for old, new in ((m0, m_new), (l0, l), (acc0, acc)))
            out.append((m_new, l, acc))
        return tuple(out)

    def pair(jj, carry):
        a = 2 * jj
        scores(a + 1, 1)
        carry = softmax_pv(a, 0, carry, None)
        scores(a + 2, 0)
        return softmax_pv(a + 1, 1, carry, None)

    init = tuple((jnp.full((1, tq), -jnp.inf, F32), jnp.zeros((1, tq), F32),
                  jnp.zeros((HEAD_DIM, tq), F32)) for _ in range(hp))
    scores(0, 0)
    carry = lax.fori_loop(0, i, pair, init)
    scores(2 * i + 1, 1, lo=tk)
    carry = softmax_pv(2 * i, 0, carry, 0)
    carry = softmax_pv(2 * i + 1, 1, carry, 1)
    for hh in range(hp):
        _, l, acc = carry[hh]
        o_ref[0, :, hh * HEAD_DIM:(hh + 1) * HEAD_DIM] = (acc / l).T.astype(BF16)


def _attn(q, kp, vt, fc, tq=1024, hp=2):
    b, s, d = q.shape
    w = hp * HEAD_DIM
    tk = tq // 2
    return pl.pallas_call(
        functools.partial(_attn_kernel, tq=tq, tk=tk, hp=hp),
        grid=(b, N_HEADS // hp, s // tq),
        in_specs=[pl.BlockSpec((1, tq, w), lambda bb, h, i: (bb, i, h)),
                  pl.BlockSpec((1, s, 2 * w), lambda bb, h, i: (bb, 0, h)),
                  pl.BlockSpec((1, w, s), lambda bb, h, i: (bb, h, 0)),
                  pl.BlockSpec((1, tq, LANES), lambda bb, h, i: (bb, i, 0))],
        out_specs=pl.BlockSpec((1, tq, w), lambda bb, h, i: (bb, i, h)),
        out_shape=jax.ShapeDtypeStruct((b, s, d), BF16),
        scratch_shapes=[pltpu.VMEM((2, hp, tk, tq), F32)],
        compiler_params=_params(("parallel", "parallel", "parallel")),
        name="attn",
    )(q, kp, vt, fc)


def _oproj_kernel(x_ref, a_ref, ada_ref, w_ref, o_ref):
    gate = ada_ref[0, 5:6, :]
    o_ref[...] = x_ref[...] + gate * _dot(a_ref[...], w_ref[...])


def _oproj(x, attn, ada, w_o, j, seq, tm=1024):
    t, d = x.shape
    per_batch = seq // tm
    return pl.pallas_call(
        _oproj_kernel,
        grid=(t // tm,),
        in_specs=[pl.BlockSpec((tm, d), lambda i: (i, 0)),
                  pl.BlockSpec((tm, d), lambda i: (i, 0)),
                  pl.BlockSpec((1, 9, d), lambda i: (i // per_batch, 0, 0)),
                  pl.BlockSpec((None, d, d), lambda i: (j, 0, 0), pipeline_mode=pl.Buffered(1))],
        out_specs=pl.BlockSpec((tm, d), lambda i: (i, 0)),
        out_shape=jax.ShapeDtypeStruct((t, d), F32),
        compiler_params=_params(("parallel",)),
        name="oproj",
    )(x, attn, ada, w_o)


def kernel(x, c, norm_g, w_ada, b_ada, w_ffn_in, w_ffn_out, w_conv_in, conv_w, conv_b, w_conv_out,
           kv_norm_g, w_ada_kv, b_ada_kv, w_kvf, b_fgate, w_q, w_o, final_g):
    b, s, d = x.shape
    depth = norm_g.shape[0]
    n_a = w_conv_in.shape[0]
    t = b * s
    assert w_q.shape[0] == 1 and depth == n_a + 1

    w_in_b = w_ffn_in[0, 0].astype(BF16)
    w_out_b = w_ffn_out[0, 0].astype(BF16)
    w_f_b = jnp.pad(w_kvf[:, 2 * d:], ((0, 0), (0, LANES - N_HEADS))).astype(BF16)
    b_f = jnp.pad(b_fgate, (0, LANES - N_HEADS)).reshape(1, LANES)
    conv_b3 = conv_b.reshape(n_a, 1, d)
    attn_casts = [(w_q, (0,)), (w_o, (0,)), (w_kvf, ())]

    c_pad = jnp.pad(c, ((0, SUBLANES - b), (0, 0)))
    xf = x.reshape(t, d)
    for l in range(depth):
        ada = _ada(c_pad, w_ada, b_ada, l)[:b].reshape(b, 9, d)
        if l < n_a:
            mixer_casts = [(w_conv_in, (l,)), (w_conv_out, (l,))]
        else:
            mixer_casts = attn_casts if n_a == 0 else []
        xf, w_in_b, w_out_b, *mixer_w = _ffn(xf, ada, norm_g[l, 0], w_in_b, w_out_b, 0, s,
                                             cast_next=(w_ffn_in, w_ffn_out, l, 1), casts=mixer_casts)
        if l < n_a:
            w_conv_in_b, w_conv_out_b = mixer_w
            if l == n_a - 1:
                xf, *attn_w = _conv(xf, ada, norm_g[l, 1], w_conv_in_b, conv_w, conv_b3, w_conv_out_b, l, s,
                                    casts=attn_casts)
            else:
                xf = _conv(xf, ada, norm_g[l, 1], w_conv_in_b, conv_w, conv_b3, w_conv_out_b, l, s)
        else:
            if n_a == 0:
                attn_w = mixer_w
            w_q_b, w_o_b, w_kv_b = attn_w
            ada_kv = _ada(c_pad, w_ada_kv[None], b_ada_kv[None], 0)[:b].reshape(b, 2, d)
            q, kp, vt, fc = _qkv(xf, ada, ada_kv, norm_g[l, 1], kv_norm_g, w_q_b[None], w_kv_b, w_f_b, b_f, 0, s)
            attn = _attn(q.reshape(b, s, d), kp.reshape(b, s, 2 * d), vt, fc.reshape(b, s, LANES))
            xf = _oproj(xf, attn.reshape(t, d), ada, w_o_b[None], 0, s)
        if l == depth - 1:
            xf = _ffn(xf, ada, norm_g[l, 2], w_in_b, w_out_b, 2, s, final_g=final_g)
        else:
            xf, w_in_b, w_out_b = _ffn(xf, ada, norm_g[l, 2], w_in_b, w_out_b, 2, s,
                                       cast_next=(w_ffn_in, w_ffn_out, l + 1, 0))
    return xf.reshape(b, s, d)
```

```python
import functools
import math

import jax
import jax.numpy as jnp
from jax import lax
from jax.experimental import pallas as pl
from jax.experimental.pallas import tpu as pltpu

EPS = 1e-6
N_HEADS = 16
HEAD_DIM = 128
CONV_WIDTH = 3
LOG2E = math.log2(math.e)
LANES = 128
SUBLANES = 8
MXU_DIM = 256
VMEM_LIMIT_BYTES = 59 * 1024 * 1024

F32 = jnp.float32
BF16 = jnp.bfloat16


def _params(semantics):
    return pltpu.CompilerParams(dimension_semantics=semantics,
                                vmem_limit_bytes=VMEM_LIMIT_BYTES)


def _dot(a, b):
    return jnp.dot(a, b, preferred_element_type=F32)


def _rmsnorm(x, g):
    return x * lax.rsqrt(jnp.mean(x * x, axis=-1, keepdims=True) + EPS) * g


NORM_ROWS = 16
NORM_UNROLL = 8
def _modulated_norm_to(x_ref, *targets):
    params = [(h_ref, g_ref[...] * (1.0 + scale), shift) for h_ref, g_ref, shift, scale in targets]

    def body(r, _):
        rows = pl.ds(pl.multiple_of(r * NORM_ROWS, NORM_ROWS), NORM_ROWS)
        x = x_ref[rows, :]
        xhat = x * lax.rsqrt(jnp.mean(x * x, axis=-1, keepdims=True) + EPS)
        for h_ref, gain, shift in params:
            h_ref[rows, :] = (xhat * gain + shift).astype(BF16)
        return 0
    lax.fori_loop(0, x_ref.shape[0] // NORM_ROWS, body, 0, unroll=NORM_UNROLL)


def _residual_to(o_ref, x_ref, coef, final_g=None, rs_ref=None):
    if final_g is None:
        o_ref[...] = x_ref[...] + coef * o_ref[...]
        return
    n = x_ref.shape[0] // NORM_ROWS

    def rows_of(r):
        return pl.ds(pl.multiple_of(r * NORM_ROWS, NORM_ROWS), NORM_ROWS)

    def stats(r, _):
        y = x_ref[rows_of(r), :] + coef * o_ref[rows_of(r), :]
        rs = lax.rsqrt(jnp.mean(y * y, axis=-1, keepdims=True) + EPS)
        rs_ref[rows_of(r), :] = jnp.broadcast_to(rs, (NORM_ROWS, LANES))
        return 0
    lax.fori_loop(0, n, stats, 0, unroll=NORM_UNROLL)

    def scale(r, _):
        y = x_ref[rows_of(r), :] + coef * o_ref[rows_of(r), :]
        o_ref[rows_of(r), :] = y * rs_ref[rows_of(r), 0:1] * final_g
        return 0
    lax.fori_loop(0, n, scale, 0, unroll=NORM_UNROLL)


def _ada_rows(ada_ref, sub):
    return tuple(ada_ref[0, 3 * sub + k:3 * sub + k + 1, :] for k in range(3))


def _ada_kernel(c_ref, w_ref, b_ref, o_ref):
    c = c_ref[...]
    cond = c * jax.nn.sigmoid(c)
    o_ref[...] = _dot(cond.astype(BF16), w_ref[...].astype(BF16)) + b_ref[...]


def _ada(c_pad, w, b, l, tn=1024):
    nl, d, n = w.shape
    return pl.pallas_call(
        _ada_kernel,
        grid=(n // tn,),
        in_specs=[pl.BlockSpec((SUBLANES, d), lambda j: (0, 0)),
                  pl.BlockSpec((None, d, tn), lambda j: (l, 0, j)),
                  pl.BlockSpec((None, 1, tn), lambda j: (l, 0, j))],
        out_specs=pl.BlockSpec((SUBLANES, tn), lambda j: (0, j)),
        out_shape=jax.ShapeDtypeStruct((SUBLANES, n), F32),
        compiler_params=_params(("parallel",)),
        name="ada",
    )(c_pad, w, b.reshape(nl, 1, n))


def _row_cast_specs(w, lead, n_steps, step_of, row_tile=16):
    rows, cols = w.shape[-2:]
    cols -= cols % LANES
    n_tiles = rows // row_tile
    assert rows % row_tile == 0 and n_tiles <= n_steps

    def tile(*g):
        return jnp.minimum(step_of(*g), n_tiles - 1)
    in_spec = pl.BlockSpec((None,) * len(lead) + (row_tile, cols), lambda *g: tuple(lead) + (tile(*g), 0))
    out_spec = pl.BlockSpec((row_tile, cols), lambda *g: (tile(*g), 0))
    return in_spec, out_spec, jax.ShapeDtypeStruct((rows, cols), BF16)


def _ada_job_specs(job, n_steps, step_of):
    c_pad, w, b, l = job
    nl, d, n = w.shape
    width = LANES * pl.cdiv(n // LANES, n_steps)
    n_tiles = n // width
    assert n % width == 0

    def tile(*g):
        return jnp.minimum(step_of(*g), n_tiles - 1)
    in_specs = [pl.BlockSpec((SUBLANES, d), lambda *g: (0, 0)),
                pl.BlockSpec((None, d, width), lambda *g: (l, 0, tile(*g))),
                pl.BlockSpec((None, 1, width), lambda *g: (l, 0, tile(*g)))]
    out_spec = pl.BlockSpec((SUBLANES, width), lambda *g: (0, tile(*g)))
    return in_specs, [c_pad, w, b.reshape(nl, 1, n)], out_spec, jax.ShapeDtypeStruct((SUBLANES, n), F32)


def _cast_tiles(in_refs, out_refs):
    for src, dst in zip(in_refs, out_refs):
        dst[...] = src[...].astype(BF16)


def _ffn_kernel(x_ref, ada_ref, g_ref, wg_ref, wu_ref, wo_ref, *rest, sub, final, n_cast):
    rest = list(rest)
    fg_ref = rest.pop(0) if final else None
    cast_in = [rest.pop(0) for _ in range(n_cast)]
    o_ref = rest.pop(0)
    cast_out = [rest.pop(0) for _ in range(n_cast)]
    h_ref = rest.pop(0)
    rs_ref = rest.pop(0) if final else None
    f = pl.program_id(1)

    shift, scale, gate = _ada_rows(ada_ref, sub)

    @pl.when(f == 0)
    def _():
        _modulated_norm_to(x_ref, (h_ref, g_ref, shift, scale))
        o_ref[...] = jnp.zeros_like(o_ref)

    _cast_tiles(cast_in, cast_out)
    h = h_ref[...]
    a = _dot(h, wg_ref[...])
    b = _dot(h, wu_ref[...])
    act = (a * jax.nn.sigmoid(a) * b).astype(BF16)
    o_ref[...] += _dot(act, wo_ref[...])

    @pl.when(f == pl.num_programs(1) - 1)
    def _():
        if final:
            _residual_to(o_ref, x_ref, 0.5 * gate, fg_ref[...], rs_ref)
        else:
            _residual_to(o_ref, x_ref, 0.5 * gate)


def _ffn(x, ada, g, w_in, w_out, sub, seq, final_g=None, cast_next=None, casts=(), tf=512):
    t, d = x.shape
    ff = w_out.shape[0]
    nf = ff // tf
    final = final_g is not None
    tm = 1024
    per_batch = seq // tm
    ni = t // tm
    in_specs = [pl.BlockSpec((tm, d), lambda i, f: (i, 0)),
                pl.BlockSpec((1, 9, d), lambda i, f: (i // per_batch, 0, 0)),
                pl.BlockSpec((1, d), lambda i, f: (0, 0)),
                pl.BlockSpec((d, tf), lambda i, f: (0, f)),
                pl.BlockSpec((d, tf), lambda i, f: (0, nf + f)),
                pl.BlockSpec((tf, d), lambda i, f: (f, 0))]
    args = [x, ada, g.reshape(1, d), w_in, w_in, w_out]
    out_specs = [pl.BlockSpec((tm, d), lambda i, f: (i, 0))]
    out_shape = [jax.ShapeDtypeStruct((t, d), F32)]
    if final:
        in_specs.append(pl.BlockSpec((1, d), lambda i, f: (0, 0)))
        args.append(final_g.reshape(1, d))
    n_cast = 0
    if cast_next:
        w_in32, w_out32, l2, idx2 = cast_next
        in_tile = (d // ni, 2 * ff // nf)
        out_tile = (ff // (ni * nf), d)
        in_specs += [pl.BlockSpec((None, None) + in_tile, lambda i, f: (l2, idx2, i, f)),
                     pl.BlockSpec((None, None) + out_tile, lambda i, f: (l2, idx2, i * nf + f, 0))]
        args += [w_in32, w_out32]
        out_specs += [pl.BlockSpec(in_tile, lambda i, f: (i, f)),
                      pl.BlockSpec(out_tile, lambda i, f: (i * nf + f, 0))]
        out_shape += [jax.ShapeDtypeStruct((d, 2 * ff), BF16), jax.ShapeDtypeStruct((ff, d), BF16)]
        n_cast += 2
    for w, lead in casts:
        i_spec, o_spec, o_shape = _row_cast_specs(w, lead, ni * nf, lambda i, f: i * nf + f)
        in_specs.append(i_spec)
        args.append(w)
        out_specs.append(o_spec)
        out_shape.append(o_shape)
        n_cast += 1
    outs = pl.pallas_call(
        functools.partial(_ffn_kernel, sub=sub, final=final, n_cast=n_cast),
        grid=(ni, nf),
        in_specs=in_specs,
        out_specs=out_specs,
        out_shape=out_shape,
        scratch_shapes=[pltpu.VMEM((tm, d), BF16)] + ([pltpu.VMEM((tm, LANES), F32)] if final else []),
        compiler_params=_params(("parallel", "arbitrary")),
        name="ffn_final" if final else "ffn",
    )(*args)
    return outs if n_cast else outs[0]


def _conv_kernel(x_ref, ada_ref, g_ref, wb_ref, wc_ref, wx_ref, cw_ref, cb_ref, wo_ref,
                 *rest, per_batch, n_cast, n_ada):
    rest = list(rest)
    cast_in = [rest.pop(0) for _ in range(n_cast)]
    ada_in = [tuple(rest.pop(0) for _ in range(3)) for _ in range(n_ada)]
    o_ref = rest.pop(0)
    cast_out = [rest.pop(0) for _ in range(n_cast)]
    ada_out = [rest.pop(0) for _ in range(n_ada)]
    h_ref, tail_ref = rest
    i = pl.program_id(0)
    c = pl.program_id(1)
    tm = x_ref.shape[0]

    shift, scale, gate = _ada_rows(ada_ref, 1)

    @pl.when(c == 0)
    def _():
        _modulated_norm_to(x_ref, (h_ref, g_ref, shift, scale))
        o_ref[...] = jnp.zeros_like(o_ref)

    @pl.when(i % per_batch == 0)
    def _():
        tail_ref[c] = jnp.zeros(tail_ref.shape[1:], F32)

    _cast_tiles(cast_in, cast_out)
    for refs, out_ref in zip(ada_in, ada_out):
        _ada_kernel(*refs, out_ref)
    h = h_ref[...]
    bg = _dot(h, wb_ref[...])
    cg = _dot(h, wc_ref[...])
    xv = _dot(h, wx_ref[...])
    u = cg * xv
    tail = tail_ref[c]
    p1 = tail[SUBLANES - 1:SUBLANES, :]
    p2 = tail[SUBLANES - 2:SUBLANES - 1, :]
    row = lax.broadcasted_iota(jnp.int32, u.shape, 0)
    u1 = jnp.where(row == 0, p1, pltpu.roll(u, 1, 0))
    u2 = jnp.where(row == 0, p2, jnp.where(row == 1, p1, pltpu.roll(u, 2, 0)))
    conv = cw_ref[0:1, :] * u2 + cw_ref[1:2, :] * u1 + cw_ref[2:3, :] * u + cb_ref[...]
    tail_ref[c] = u[tm - SUBLANES:, :]
    o_ref[...] += _dot((bg * conv).astype(BF16), wo_ref[...])

    @pl.when(c == pl.num_programs(1) - 1)
    def _():
        _residual_to(o_ref, x_ref, gate)


def _conv(x, ada, g, w_in, conv_w, conv_b, w_out, l, seq, casts=(), ada_jobs=(), tm=512, tc=512):
    t, d = x.shape
    nc = d // tc
    ni = t // tm
    per_batch = seq // tm
    in_specs = [pl.BlockSpec((tm, d), lambda i, c: (i, 0)),
                pl.BlockSpec((1, 9, d), lambda i, c: (i // per_batch, 0, 0)),
                pl.BlockSpec((1, d), lambda i, c: (0, 0)),
                pl.BlockSpec((d, tc), lambda i, c: (0, c)),
                pl.BlockSpec((d, tc), lambda i, c: (0, nc + c)),
                pl.BlockSpec((d, tc), lambda i, c: (0, 2 * nc + c)),
                pl.BlockSpec((None, CONV_WIDTH, tc), lambda i, c: (l, 0, c)),
                pl.BlockSpec((None, 1, tc), lambda i, c: (l, 0, c)),
                pl.BlockSpec((tc, d), lambda i, c: (c, 0))]
    args = [x, ada, g.reshape(1, d), w_in, w_in, w_in, conv_w, conv_b, w_out]
    out_specs = [pl.BlockSpec((tm, d), lambda i, c: (i, 0))]
    out_shape = [jax.ShapeDtypeStruct((t, d), F32)]
    for w, lead in casts:
        i_spec, o_spec, o_shape = _row_cast_specs(w, lead, ni * nc, lambda i, c: i * nc + c)
        in_specs.append(i_spec)
        args.append(w)
        out_specs.append(o_spec)
        out_shape.append(o_shape)
    for job in ada_jobs:
        i_specs, i_args, o_spec, o_shape = _ada_job_specs(job, ni * nc, lambda i, c: i * nc + c)
        in_specs += i_specs
        args += i_args
        out_specs.append(o_spec)
        out_shape.append(o_shape)
    outs = pl.pallas_call(
        functools.partial(_conv_kernel, per_batch=per_batch, n_cast=len(casts), n_ada=len(ada_jobs)),
        grid=(ni, nc),
        in_specs=in_specs,
        out_specs=out_specs,
        out_shape=out_shape,
        scratch_shapes=[pltpu.VMEM((tm, d), BF16),
                        pltpu.VMEM((nc, SUBLANES, tc), F32)],
        compiler_params=_params(("arbitrary", "arbitrary")),
        name="conv",
    )(*args)
    return outs if casts or ada_jobs else outs[0]


def _qkv_kernel(x_ref, ada_ref, adakv_ref, g_ref, gkv_ref, wq_ref, wk_ref, wv_ref, wf_ref, bf_ref,
                q_ref, k_ref, v_ref, fc_ref, hq_ref, hkv_ref, carry_ref, *, per_batch, q_scale):
    i = pl.program_id(0)
    c = pl.program_id(1)
    tm = x_ref.shape[0]

    @pl.when(c == 0)
    def _():
        shift, scale, _ = _ada_rows(ada_ref, 1)
        _modulated_norm_to(x_ref, (hq_ref, g_ref, shift, scale),
                           (hkv_ref, gkv_ref, adakv_ref[0, 0:1, :], adakv_ref[0, 1:2, :]))
        hkv = hkv_ref[...]

        @pl.when(i % per_batch == 0)
        def _():
            carry_ref[...] = jnp.zeros_like(carry_ref)

        zf = _dot(hkv, wf_ref[...]) + bf_ref[...]
        ls = jnp.minimum(zf, 0.0) - jnp.log1p(jnp.exp(-jnp.abs(zf)))
        hi = ls.astype(BF16)
        r1 = ls - hi.astype(F32)
        mid = r1.astype(BF16)
        lo = (r1 - mid.astype(F32)).astype(BF16)
        rr = lax.broadcasted_iota(jnp.int32, (tm, tm), 0)
        cc = lax.broadcasted_iota(jnp.int32, (tm, tm), 1)
        tri = (rr >= cc).astype(BF16)
        parts = _dot(tri, jnp.concatenate([hi, mid, lo], axis=1))
        cum = (parts[:, :LANES] + parts[:, LANES:2 * LANES]) + parts[:, 2 * LANES:] + carry_ref[0:1, :]
        carry_ref[...] = jnp.broadcast_to(cum[tm - 1:tm, :], carry_ref.shape)
        fc_ref[...] = cum

    hkv = hkv_ref[...]
    q_ref[...] = (_dot(hq_ref[...], wq_ref[...]) * q_scale).astype(BF16)
    kc = _dot(hkv, wk_ref[...])
    cum = fc_ref[...]
    lane = lax.broadcasted_iota(jnp.int32, (tm, LANES), 1)
    for hh in range(kc.shape[1] // HEAD_DIM):
        head = c * (kc.shape[1] // HEAD_DIM) + hh
        col = jnp.sum(jnp.where(lane == head, cum, 0.0), axis=1, keepdims=True) * LOG2E
        hi, mid, lo = _split3(col)
        aug = jnp.where(lane < 3, 1.0,
                        jnp.where(lane == 3, -hi, jnp.where(lane == 4, -mid, jnp.where(lane == 5, -lo, 0.0))))
        k_ref[:, 2 * hh * HEAD_DIM:(2 * hh + 1) * HEAD_DIM] = kc[:, hh * HEAD_DIM:(hh + 1) * HEAD_DIM].astype(BF16)
        k_ref[:, (2 * hh + 1) * HEAD_DIM:(2 * hh + 2) * HEAD_DIM] = aug.astype(BF16)
    v_ref[0] = _dot(hkv, wv_ref[...]).T.astype(BF16)


def _qkv(x, ada, ada_kv, g, gkv, w_q, w_kv, w_f, b_f, j, seq, tm=1024, tn=512):
    t, d = x.shape
    nn = d // tn
    per_batch = seq // tm
    q_scale = LOG2E / math.sqrt(HEAD_DIM)
    row_blk = pl.BlockSpec((tm, tn), lambda i, c: (i, c))
    return pl.pallas_call(
        functools.partial(_qkv_kernel, per_batch=per_batch, q_scale=q_scale),
        grid=(t // tm, nn),
        in_specs=[pl.BlockSpec((tm, d), lambda i, c: (i, 0)),
                  pl.BlockSpec((1, 9, d), lambda i, c: (i // per_batch, 0, 0)),
                  pl.BlockSpec((1, 2, d), lambda i, c: (i // per_batch, 0, 0)),
                  pl.BlockSpec((1, d), lambda i, c: (0, 0)),
                  pl.BlockSpec((1, d), lambda i, c: (0, 0)),
                  pl.BlockSpec((None, d, tn), lambda i, c: (j, 0, c)),
                  pl.BlockSpec((d, tn), lambda i, c: (0, c)),
                  pl.BlockSpec((d, tn), lambda i, c: (0, nn + c)),
                  pl.BlockSpec((d, LANES), lambda i, c: (0, 0)),
                  pl.BlockSpec((1, LANES), lambda i, c: (0, 0))],
        out_specs=[row_blk,
                   pl.BlockSpec((tm, 2 * tn), lambda i, c: (i, c)),
                   pl.BlockSpec((1, tn, tm), lambda i, c: (i // per_batch, c, i % per_batch)),
                   pl.BlockSpec((tm, LANES), lambda i, c: (i, 0))],
        out_shape=[jax.ShapeDtypeStruct((t, d), BF16),
                   jax.ShapeDtypeStruct((t, 2 * d), BF16),
                   jax.ShapeDtypeStruct((t // seq, d, seq), BF16),
                   jax.ShapeDtypeStruct((t, LANES), F32)],
        scratch_shapes=[pltpu.VMEM((tm, d), BF16),
                        pltpu.VMEM((tm, d), BF16),
                        pltpu.VMEM((SUBLANES, LANES), F32)],
        compiler_params=_params(("arbitrary", "arbitrary")),
        name="qkv",
    )(x, ada, ada_kv, g.reshape(1, d), gkv.reshape(1, d), w_q, w_kv, w_kv, w_f, b_f)


def _split3(f):
    hi = f.astype(BF16)
    r1 = f - hi.astype(F32)
    mid = r1.astype(BF16)
    lo = (r1 - mid.astype(F32)).astype(BF16)
    return hi.astype(F32), mid.astype(F32), lo.astype(F32)


def _attn_kernel(q_ref, k_ref, vt_ref, fc_ref, o_ref, st_ref, *, tq, tk, hp):
    hg = pl.program_id(1)
    i = pl.program_id(2)

    fblk = fc_ref[0]
    lane = lax.broadcasted_iota(jnp.int32, (tq, LANES), 1)
    qts = []
    for hh in range(hp):
        fq = jnp.sum(jnp.where(lane == hg * hp + hh, fblk, 0.0), axis=1, keepdims=True) * LOG2E
        hi, mid, lo = _split3(fq)
        aq = jnp.where(lane == 0, hi,
                       jnp.where(lane == 1, mid,
                                 jnp.where(lane == 2, lo, jnp.where(lane < 6, 1.0, 0.0))))
        qh = q_ref[0, :, hh * HEAD_DIM:(hh + 1) * HEAD_DIM].astype(F32)
        qts.append(jnp.concatenate([qh.T, aq.T], axis=0).astype(BF16))

    def scores(j, slot, lo=0):
        start = pl.multiple_of(j * tk, tk)
        for hh in range(hp):
            kj = k_ref[0, pl.ds(start, tk), 2 * hh * HEAD_DIM:2 * (hh + 1) * HEAD_DIM]
            st_ref[slot, hh, :, lo:] = _dot(kj, qts[hh][:, lo:])

    def softmax_pv(j, slot, carry, diag):
        start = pl.multiple_of(j * tk, tk)
        lo = 0 if diag is None else diag * tk
        out = []
        for hh in range(hp):
            m0, l0, acc0 = carry[hh]
            m, l, acc = m0[:, lo:], l0[:, lo:], acc0[:, lo:]
            st = st_ref[slot, hh, :, lo:]
            if diag is not None:
                key = lax.broadcasted_iota(jnp.int32, st.shape, 0)
                qry = lax.broadcasted_iota(jnp.int32, st.shape, 1)
                st = jnp.where(key <= qry, st, -jnp.inf)
            m_new = jnp.maximum(m, jnp.max(st, axis=0, keepdims=True))
            alpha = jnp.exp2(m - m_new)
            pt = jnp.exp2(st - m_new)
            l = alpha * l + jnp.sum(pt, axis=0, keepdims=True)
            vj = vt_ref[0, hh * HEAD_DIM:(hh + 1) * HEAD_DIM, pl.ds(start, tk)]
            acc = alpha * acc + _dot(vj, pt.astype(BF16))
            if lo:
                m_new, l, acc = (jnp.concatenate([old[:, :lo], new], axis=1)
                                 for old, new in ((m0, m_new), (l0, l), (acc0, acc)))
            out.append((m_new, l, acc))
        return tuple(out)

    def pair(jj, carry):
        a = 2 * jj
        scores(a + 1, 1)
        carry = softmax_pv(a, 0, carry, None)
        scores(a + 2, 0)
        return softmax_pv(a + 1, 1, carry, None)

    init = tuple((jnp.full((1, tq), -jnp.inf, F32), jnp.zeros((1, tq), F32),
                  jnp.zeros((HEAD_DIM, tq), F32)) for _ in range(hp))
    scores(0, 0)
    carry = lax.fori_loop(0, i, pair, init)
    scores(2 * i + 1, 1, lo=tk)
    carry = softmax_pv(2 * i, 0, carry, 0)
    carry = softmax_pv(2 * i + 1, 1, carry, 1)
    for hh in range(hp):
        _, l, acc = carry[hh]
        o_ref[0, :, hh * HEAD_DIM:(hh + 1) * HEAD_DIM] = (acc / l).T.astype(BF16)


def _attn(q, kp, vt, fc, tq=1024, hp=2):
    b, s, d = q.shape
    w = hp * HEAD_DIM
    tk = tq // 2
    return pl.pallas_call(
        functools.partial(_attn_kernel, tq=tq, tk=tk, hp=hp),
        grid=(b, N_HEADS // hp, s // tq),
        in_specs=[pl.BlockSpec((1, tq, w), lambda bb, h, i: (bb, i, h)),
                  pl.BlockSpec((1, s, 2 * w), lambda bb, h, i: (bb, 0, h)),
                  pl.BlockSpec((1, w, s), lambda bb, h, i: (bb, h, 0)),
                  pl.BlockSpec((1, tq, LANES), lambda bb, h, i: (bb, i, 0))],
        out_specs=pl.BlockSpec((1, tq, w), lambda bb, h, i: (bb, i, h)),
        out_shape=jax.ShapeDtypeStruct((b, s, d), BF16),
        scratch_shapes=[pltpu.VMEM((2, hp, tk, tq), F32)],
        compiler_params=_params(("parallel", "parallel", "parallel")),
        name="attn",
    )(q, kp, vt, fc)


def _oproj_kernel(x_ref, a_ref, ada_ref, w_ref, o_ref):
    gate = ada_ref[0, 5:6, :]
    o_ref[...] = x_ref[...] + gate * _dot(a_ref[...], w_ref[...])


def _oproj(x, attn, ada, w_o, j, seq, tm=1024):
    t, d = x.shape
    per_batch = seq // tm
    return pl.pallas_call(
        _oproj_kernel,
        grid=(t // tm,),
        in_specs=[pl.BlockSpec((tm, d), lambda i: (i, 0)),
                  pl.BlockSpec((tm, d), lambda i: (i, 0)),
                  pl.BlockSpec((1, 9, d), lambda i: (i // per_batch, 0, 0)),
                  pl.BlockSpec((None, d, d), lambda i: (j, 0, 0), pipeline_mode=pl.Buffered(1))],
        out_specs=pl.BlockSpec((tm, d), lambda i: (i, 0)),
        out_shape=jax.ShapeDtypeStruct((t, d), F32),
        compiler_params=_params(("parallel",)),
        name="oproj",
    )(x, attn, ada, w_o)


def kernel(x, c, norm_g, w_ada, b_ada, w_ffn_in, w_ffn_out, w_conv_in, conv_w, conv_b, w_conv_out,
           kv_norm_g, w_ada_kv, b_ada_kv, w_kvf, b_fgate, w_q, w_o, final_g):
    b, s, d = x.shape
    depth = norm_g.shape[0]
    n_a = w_conv_in.shape[0]
    t = b * s
    assert w_q.shape[0] == 1 and depth == n_a + 1

    w_in_b = w_ffn_in[0, 0].astype(BF16)
    w_out_b = w_ffn_out[0, 0].astype(BF16)
    w_f_b = jnp.pad(w_kvf[:, 2 * d:], ((0, 0), (0, LANES - N_HEADS))).astype(BF16)
    b_f = jnp.pad(b_fgate, (0, LANES - N_HEADS)).reshape(1, LANES)
    conv_b3 = conv_b.reshape(n_a, 1, d)
    attn_casts = [(w_q, (0,)), (w_o, (0,)), (w_kvf, ())]

    c_pad = jnp.pad(c, ((0, SUBLANES - b), (0, 0)))
    xf = x.reshape(t, d)
    ada = _ada(c_pad, w_ada, b_ada, 0)[:b].reshape(b, 9, d)
    kv_job = (c_pad, w_ada_kv[None], b_ada_kv[None], 0)
    if n_a == 0:
        ada_kv = _ada(*kv_job)[:b].reshape(b, 2, d)
    for l in range(depth):
        if l < n_a:
            mixer_casts = [(w_conv_in, (l,)), (w_conv_out, (l,))]
        else:
            mixer_casts = attn_casts if n_a == 0 else []
        xf, w_in_b, w_out_b, *mixer_w = _ffn(xf, ada, norm_g[l, 0], w_in_b, w_out_b, 0, s,
                                             cast_next=(w_ffn_in, w_ffn_out, l, 1), casts=mixer_casts)
        if l < n_a:
            w_conv_in_b, w_conv_out_b = mixer_w
            casts = attn_casts if l == n_a - 1 else []
            jobs = [(c_pad, w_ada, b_ada, l + 1)] + ([kv_job] if l == n_a - 1 else [])
            xf, *side = _conv(xf, ada, norm_g[l, 1], w_conv_in_b, conv_w, conv_b3, w_conv_out_b, l, s,
                              casts=casts, ada_jobs=jobs)
            if casts:
                attn_w = side[:len(casts)]
            ada_next = side[len(casts)][:b].reshape(b, 9, d)
            if l == n_a - 1:
                ada_kv = side[len(casts) + 1][:b].reshape(b, 2, d)
        else:
            if n_a == 0:
                attn_w = mixer_w
            w_q_b, w_o_b, w_kv_b = attn_w
            q, kp, vt, fc = _qkv(xf, ada, ada_kv, norm_g[l, 1], kv_norm_g, w_q_b[None], w_kv_b, w_f_b, b_f, 0, s)
            attn = _attn(q.reshape(b, s, d), kp.reshape(b, s, 2 * d), vt, fc.reshape(b, s, LANES))
            xf = _oproj(xf, attn.reshape(t, d), ada, w_o_b[None], 0, s)
            ada_next = None
        if l == depth - 1:
            xf = _ffn(xf, ada, norm_g[l, 2], w_in_b, w_out_b, 2, s, final_g=final_g)
        else:
            xf, w_in_b, w_out_b = _ffn(xf, ada, norm_g[l, 2], w_in_b, w_out_b, 2, s,
                                       cast_next=(w_ffn_in, w_ffn_out, l + 1, 0))
        ada = ada_next
    return xf.reshape(b, s, d)
```

```python
import functools
import math

import jax
import jax.numpy as jnp
from jax import lax
from jax.experimental import pallas as pl
from jax.experimental.pallas import tpu as pltpu

EPS = 1e-6
N_HEADS = 16
HEAD_DIM = 128
CONV_WIDTH = 3
LOG2E = math.log2(math.e)
LANES = 128
SUBLANES = 8
VMEM_LIMIT_BYTES = 59 * 1024 * 1024
NORM_ROWS = 16
NORM_UNROLL = 8

F32 = jnp.float32
BF16 = jnp.bfloat16


def _params(semantics):
    return pltpu.CompilerParams(dimension_semantics=semantics,
                                vmem_limit_bytes=VMEM_LIMIT_BYTES)


def _dot(a, b):
    return jnp.dot(a, b, preferred_element_type=F32)


def _rmsnorm(x, g):
    return x * lax.rsqrt(jnp.mean(x * x, axis=-1, keepdims=True) + EPS) * g


def _modulated_norm_to(x_ref, *targets):
    params = [(h_ref, g_ref[...] * (1.0 + scale), shift) for h_ref, g_ref, shift, scale in targets]

    def body(r, _):
        rows = pl.ds(pl.multiple_of(r * NORM_ROWS, NORM_ROWS), NORM_ROWS)
        x = x_ref[rows, :]
        xhat = x * lax.rsqrt(jnp.mean(x * x, axis=-1, keepdims=True) + EPS)
        for h_ref, gain, shift in params:
            h_ref[rows, :] = (xhat * gain + shift).astype(BF16)
        return 0
    lax.fori_loop(0, x_ref.shape[0] // NORM_ROWS, body, 0, unroll=NORM_UNROLL)


def _residual_to(o_ref, x_ref, coef, final_g=None, rs_ref=None):
    if final_g is None:
        o_ref[...] = x_ref[...] + coef * o_ref[...]
        return
    n = x_ref.shape[0] // NORM_ROWS

    def rows_of(r):
        return pl.ds(pl.multiple_of(r * NORM_ROWS, NORM_ROWS), NORM_ROWS)

    def stats(r, _):
        y = x_ref[rows_of(r), :] + coef * o_ref[rows_of(r), :]
        rs = lax.rsqrt(jnp.mean(y * y, axis=-1, keepdims=True) + EPS)
        rs_ref[rows_of(r), :] = jnp.broadcast_to(rs, (NORM_ROWS, LANES))
        return 0
    lax.fori_loop(0, n, stats, 0, unroll=NORM_UNROLL)

    def scale(r, _):
        y = x_ref[rows_of(r), :] + coef * o_ref[rows_of(r), :]
        o_ref[rows_of(r), :] = y * rs_ref[rows_of(r), 0:1] * final_g
        return 0
    lax.fori_loop(0, n, scale, 0, unroll=NORM_UNROLL)


def _ada_rows(ada_ref, sub):
    return tuple(ada_ref[0, 3 * sub + k:3 * sub + k + 1, :] for k in range(3))


def _ada_kernel(c_ref, w_ref, b_ref, o_ref):
    c = c_ref[...]
    cond = c * jax.nn.sigmoid(c)
    o_ref[...] = _dot(cond.astype(BF16), w_ref[...].astype(BF16)) + b_ref[...]


def _ada(c_pad, w, b, l, tn=1024):
    nl, d, n = w.shape
    return pl.pallas_call(
        _ada_kernel,
        grid=(n // tn,),
        in_specs=[pl.BlockSpec((SUBLANES, d), lambda j: (0, 0)),
                  pl.BlockSpec((None, d, tn), lambda j: (l, 0, j)),
                  pl.BlockSpec((None, 1, tn), lambda j: (l, 0, j))],
        out_specs=pl.BlockSpec((SUBLANES, tn), lambda j: (0, j)),
        out_shape=jax.ShapeDtypeStruct((SUBLANES, n), F32),
        compiler_params=_params(("parallel",)),
        name="ada",
    )(c_pad, w, b.reshape(nl, 1, n))


def _row_cast_specs(w, lead, n_steps, step_of, row_tile=16):
    rows, cols = w.shape[-2:]
    cols -= cols % LANES
    n_tiles = rows // row_tile
    assert rows % row_tile == 0 and n_tiles <= n_steps

    def tile(*g):
        return jnp.minimum(step_of(*g), n_tiles - 1)
    in_spec = pl.BlockSpec((None,) * len(lead) + (row_tile, cols), lambda *g: tuple(lead) + (tile(*g), 0))
    out_spec = pl.BlockSpec((row_tile, cols), lambda *g: (tile(*g), 0))
    return in_spec, out_spec, jax.ShapeDtypeStruct((rows, cols), BF16)


def _cast_tiles(in_refs, out_refs):
    for src, dst in zip(in_refs, out_refs):
        dst[...] = src[...].astype(BF16)


def _ffn_kernel(x_ref, ada_ref, g_ref, wg_ref, wu_ref, wo_ref, *rest, sub, final, n_cast):
    rest = list(rest)
    fg_ref = rest.pop(0) if final else None
    cast_in = [rest.pop(0) for _ in range(n_cast)]
    o_ref = rest.pop(0)
    cast_out = [rest.pop(0) for _ in range(n_cast)]
    h_ref = rest.pop(0)
    rs_ref = rest.pop(0) if final else None
    f = pl.program_id(1)

    shift, scale, gate = _ada_rows(ada_ref, sub)

    @pl.when(f == 0)
    def _():
        _modulated_norm_to(x_ref, (h_ref, g_ref, shift, scale))
        o_ref[...] = jnp.zeros_like(o_ref)

    _cast_tiles(cast_in, cast_out)
    h = h_ref[...]
    a = _dot(h, wg_ref[...])
    b = _dot(h, wu_ref[...])
    act = (a * jax.nn.sigmoid(a) * b).astype(BF16)
    o_ref[...] += _dot(act, wo_ref[...])

    @pl.when(f == pl.num_programs(1) - 1)
    def _():
        if final:
            _residual_to(o_ref, x_ref, 0.5 * gate, fg_ref[...], rs_ref)
        else:
            _residual_to(o_ref, x_ref, 0.5 * gate)


def _ffn(x, ada, g, w_in, w_out, sub, seq, final_g=None, cast_next=None, casts=(), tf=512):
    t, d = x.shape
    ff = w_out.shape[0]
    nf = ff // tf
    final = final_g is not None
    tm = 1024
    per_batch = seq // tm
    ni = t // tm
    in_specs = [pl.BlockSpec((tm, d), lambda i, f: (i, 0)),
                pl.BlockSpec((1, 9, d), lambda i, f: (i // per_batch, 0, 0)),
                pl.BlockSpec((1, d), lambda i, f: (0, 0)),
                pl.BlockSpec((d, tf), lambda i, f: (0, f)),
                pl.BlockSpec((d, tf), lambda i, f: (0, nf + f)),
                pl.BlockSpec((tf, d), lambda i, f: (f, 0))]
    args = [x, ada, g.reshape(1, d), w_in, w_in, w_out]
    out_specs = [pl.BlockSpec((tm, d), lambda i, f: (i, 0))]
    out_shape = [jax.ShapeDtypeStruct((t, d), F32)]
    if final:
        in_specs.append(pl.BlockSpec((1, d), lambda i, f: (0, 0)))
        args.append(final_g.reshape(1, d))
    n_cast = 0
    if cast_next:
        w_in32, w_out32, l2, idx2 = cast_next
        in_tile = (d // ni, 2 * ff // nf)
        out_tile = (ff // (ni * nf), d)
        in_specs += [pl.BlockSpec((None, None) + in_tile, lambda i, f: (l2, idx2, i, f)),
                     pl.BlockSpec((None, None) + out_tile, lambda i, f: (l2, idx2, i * nf + f, 0))]
        args += [w_in32, w_out32]
        out_specs += [pl.BlockSpec(in_tile, lambda i, f: (i, f)),
                      pl.BlockSpec(out_tile, lambda i, f: (i * nf + f, 0))]
        out_shape += [jax.ShapeDtypeStruct((d, 2 * ff), BF16), jax.ShapeDtypeStruct((ff, d), BF16)]
        n_cast += 2
    for w, lead in casts:
        i_spec, o_spec, o_shape = _row_cast_specs(w, lead, ni * nf, lambda i, f: i * nf + f)
        in_specs.append(i_spec)
        args.append(w)
        out_specs.append(o_spec)
        out_shape.append(o_shape)
        n_cast += 1
    outs = pl.pallas_call(
        functools.partial(_ffn_kernel, sub=sub, final=final, n_cast=n_cast),
        grid=(ni, nf),
        in_specs=in_specs,
        out_specs=out_specs,
        out_shape=out_shape,
        scratch_shapes=[pltpu.VMEM((tm, d), BF16)] + ([pltpu.VMEM((tm, LANES), F32)] if final else []),
        compiler_params=_params(("parallel", "arbitrary")),
        name="ffn_final" if final else "ffn",
    )(*args)
    return outs if n_cast else outs[0]


def _conv_kernel(x_ref, ada_ref, g_ref, wb_ref, wc_ref, wx_ref, cw_ref, cb_ref, wo_ref,
                 *rest, per_batch, n_cast):
    cast_in, (o_ref, *cast_out), (h_ref, tail_ref) = (rest[:n_cast], rest[n_cast:2 * n_cast + 1],
                                                      rest[2 * n_cast + 1:])
    i = pl.program_id(0)
    c = pl.program_id(1)
    tm = x_ref.shape[0]

    shift, scale, gate = _ada_rows(ada_ref, 1)

    @pl.when(c == 0)
    def _():
        _modulated_norm_to(x_ref, (h_ref, g_ref, shift, scale))
        o_ref[...] = jnp.zeros_like(o_ref)

    @pl.when(i % per_batch == 0)
    def _():
        tail_ref[c] = jnp.zeros(tail_ref.shape[1:], F32)

    _cast_tiles(cast_in, cast_out)
    h = h_ref[...]
    bg = _dot(h, wb_ref[...])
    cg = _dot(h, wc_ref[...])
    xv = _dot(h, wx_ref[...])
    u = cg * xv
    tail = tail_ref[c]
    p1 = tail[SUBLANES - 1:SUBLANES, :]
    p2 = tail[SUBLANES - 2:SUBLANES - 1, :]
    row = lax.broadcasted_iota(jnp.int32, u.shape, 0)
    u1 = jnp.where(row == 0, p1, pltpu.roll(u, 1, 0))
    u2 = jnp.where(row == 0, p2, jnp.where(row == 1, p1, pltpu.roll(u, 2, 0)))
    conv = cw_ref[0:1, :] * u2 + cw_ref[1:2, :] * u1 + cw_ref[2:3, :] * u + cb_ref[...]
    tail_ref[c] = u[tm - SUBLANES:, :]
    o_ref[...] += _dot((bg * conv).astype(BF16), wo_ref[...])

    @pl.when(c == pl.num_programs(1) - 1)
    def _():
        _residual_to(o_ref, x_ref, gate)


def _conv(x, ada, g, w_in, conv_w, conv_b, w_out, l, seq, casts=(), tm=512, tc=512):
    t, d = x.shape
    nc = d // tc
    ni = t // tm
    per_batch = seq // tm
    in_specs = [pl.BlockSpec((tm, d), lambda i, c: (i, 0)),
                pl.BlockSpec((1, 9, d), lambda i, c: (i // per_batch, 0, 0)),
                pl.BlockSpec((1, d), lambda i, c: (0, 0)),
                pl.BlockSpec((d, tc), lambda i, c: (0, c)),
                pl.BlockSpec((d, tc), lambda i, c: (0, nc + c)),
                pl.BlockSpec((d, tc), lambda i, c: (0, 2 * nc + c)),
                pl.BlockSpec((None, CONV_WIDTH, tc), lambda i, c: (l, 0, c)),
                pl.BlockSpec((None, 1, tc), lambda i, c: (l, 0, c)),
                pl.BlockSpec((tc, d), lambda i, c: (c, 0))]
    args = [x, ada, g.reshape(1, d), w_in, w_in, w_in, conv_w, conv_b, w_out]
    out_specs = [pl.BlockSpec((tm, d), lambda i, c: (i, 0))]
    out_shape = [jax.ShapeDtypeStruct((t, d), F32)]
    for w, lead in casts:
        i_spec, o_spec, o_shape = _row_cast_specs(w, lead, ni * nc, lambda i, c: i * nc + c)
        in_specs.append(i_spec)
        args.append(w)
        out_specs.append(o_spec)
        out_shape.append(o_shape)
    outs = pl.pallas_call(
        functools.partial(_conv_kernel, per_batch=per_batch, n_cast=len(casts)),
        grid=(ni, nc),
        in_specs=in_specs,
        out_specs=out_specs,
        out_shape=out_shape,
        scratch_shapes=[pltpu.VMEM((tm, d), BF16),
                        pltpu.VMEM((nc, SUBLANES, tc), F32)],
        compiler_params=_params(("arbitrary", "arbitrary")),
        name="conv",
    )(*args)
    return outs if casts else outs[0]


def _qkv_kernel(x_ref, ada_ref, adakv_ref, g_ref, gkv_ref, wq_ref, wk_ref, wv_ref, wf_ref, bf_ref,
                q_ref, k_ref, v_ref, fc_ref, hq_ref, hkv_ref, carry_ref, *, per_batch, q_scale):
    i = pl.program_id(0)
    c = pl.program_id(1)
    tm = x_ref.shape[0]

    @pl.when(c == 0)
    def _():
        shift, scale, _ = _ada_rows(ada_ref, 1)
        _modulated_norm_to(x_ref, (hq_ref, g_ref, shift, scale),
                           (hkv_ref, gkv_ref, adakv_ref[0, 0:1, :], adakv_ref[0, 1:2, :]))
        hkv = hkv_ref[...]

        @pl.when(i % per_batch == 0)
        def _():
            carry_ref[...] = jnp.zeros_like(carry_ref)

        zf = _dot(hkv, wf_ref[...]) + bf_ref[...]
        ls = jnp.minimum(zf, 0.0) - jnp.log1p(jnp.exp(-jnp.abs(zf)))
        hi = ls.astype(BF16)
        r1 = ls - hi.astype(F32)
        mid = r1.astype(BF16)
        lo = (r1 - mid.astype(F32)).astype(BF16)
        rr = lax.broadcasted_iota(jnp.int32, (tm, tm), 0)
        cc = lax.broadcasted_iota(jnp.int32, (tm, tm), 1)
        tri = (rr >= cc).astype(BF16)
        parts = _dot(tri, jnp.concatenate([hi, mid, lo], axis=1))
        cum = (parts[:, :LANES] + parts[:, LANES:2 * LANES]) + parts[:, 2 * LANES:] + carry_ref[0:1, :]
        carry_ref[...] = jnp.broadcast_to(cum[tm - 1:tm, :], carry_ref.shape)
        fc_ref[...] = cum

    hkv = hkv_ref[...]
    q_ref[...] = (_dot(hq_ref[...], wq_ref[...]) * q_scale).astype(BF16)
    kc = _dot(hkv, wk_ref[...])
    cum = fc_ref[...]
    lane = lax.broadcasted_iota(jnp.int32, (tm, LANES), 1)
    for hh in range(kc.shape[1] // HEAD_DIM):
        head = c * (kc.shape[1] // HEAD_DIM) + hh
        col = jnp.sum(jnp.where(lane == head, cum, 0.0), axis=1, keepdims=True) * LOG2E
        hi, mid, lo = _split3(col)
        aug = jnp.where(lane < 3, 1.0,
                        jnp.where(lane == 3, -hi, jnp.where(lane == 4, -mid, jnp.where(lane == 5, -lo, 0.0))))
        k_ref[:, 2 * hh * HEAD_DIM:(2 * hh + 1) * HEAD_DIM] = kc[:, hh * HEAD_DIM:(hh + 1) * HEAD_DIM].astype(BF16)
        k_ref[:, (2 * hh + 1) * HEAD_DIM:(2 * hh + 2) * HEAD_DIM] = aug.astype(BF16)
    v_ref[0] = _dot(hkv, wv_ref[...]).T.astype(BF16)


def _qkv(x, ada, ada_kv, g, gkv, w_q, w_kv, w_f, b_f, j, seq, tm=1024, tn=512):
    t, d = x.shape
    nn = d // tn
    per_batch = seq // tm
    q_scale = LOG2E / math.sqrt(HEAD_DIM)
    row_blk = pl.BlockSpec((tm, tn), lambda i, c: (i, c))
    return pl.pallas_call(
        functools.partial(_qkv_kernel, per_batch=per_batch, q_scale=q_scale),
        grid=(t // tm, nn),
        in_specs=[pl.BlockSpec((tm, d), lambda i, c: (i, 0)),
                  pl.BlockSpec((1, 9, d), lambda i, c: (i // per_batch, 0, 0)),
                  pl.BlockSpec((1, 2, d), lambda i, c: (i // per_batch, 0, 0)),
                  pl.BlockSpec((1, d), lambda i, c: (0, 0)),
                  pl.BlockSpec((1, d), lambda i, c: (0, 0)),
                  pl.BlockSpec((None, d, tn), lambda i, c: (j, 0, c)),
                  pl.BlockSpec((d, tn), lambda i, c: (0, c)),
                  pl.BlockSpec((d, tn), lambda i, c: (0, nn + c)),
                  pl.BlockSpec((d, LANES), lambda i, c: (0, 0)),
                  pl.BlockSpec((1, LANES), lambda i, c: (0, 0))],
        out_specs=[row_blk,
                   pl.BlockSpec((tm, 2 * tn), lambda i, c: (i, c)),
                   pl.BlockSpec((1, tn, tm), lambda i, c: (i // per_batch, c, i % per_batch)),
                   pl.BlockSpec((tm, LANES), lambda i, c: (i, 0))],
        out_shape=[jax.ShapeDtypeStruct((t, d), BF16),
                   jax.ShapeDtypeStruct((t, 2 * d), BF16),
                   jax.ShapeDtypeStruct((t // seq, d, seq), BF16),
                   jax.ShapeDtypeStruct((t, LANES), F32)],
        scratch_shapes=[pltpu.VMEM((tm, d), BF16),
                        pltpu.VMEM((tm, d), BF16),
                        pltpu.VMEM((SUBLANES, LANES), F32)],
        compiler_params=_params(("arbitrary", "arbitrary")),
        name="qkv",
    )(x, ada, ada_kv, g.reshape(1, d), gkv.reshape(1, d), w_q, w_kv, w_kv, w_f, b_f)


def _split3(f):
    hi = f.astype(BF16)
    r1 = f - hi.astype(F32)
    mid = r1.astype(BF16)
    lo = (r1 - mid.astype(F32)).astype(BF16)
    return hi.astype(F32), mid.astype(F32), lo.astype(F32)


def _attn_kernel(q_ref, k_ref, vt_ref, fc_ref, o_ref, st_ref, *, tq, tk, hp):
    hg = pl.program_id(1)
    i = pl.program_id(2)

    fblk = fc_ref[0]
    lane = lax.broadcasted_iota(jnp.int32, (tq, LANES), 1)
    qts = []
    for hh in range(hp):
        fq = jnp.sum(jnp.where(lane == hg * hp + hh, fblk, 0.0), axis=1, keepdims=True) * LOG2E
        hi, mid, lo = _split3(fq)
        aq = jnp.where(lane == 0, hi,
                       jnp.where(lane == 1, mid,
                                 jnp.where(lane == 2, lo, jnp.where(lane < 6, 1.0, 0.0))))
        qh = q_ref[0, :, hh * HEAD_DIM:(hh + 1) * HEAD_DIM].astype(F32)
        qts.append(jnp.concatenate([qh.T, aq.T], axis=0).astype(BF16))

    def scores(j, slot, lo=0):
        start = pl.multiple_of(j * tk, tk)
        for hh in range(hp):
            kj = k_ref[0, pl.ds(start, tk), 2 * hh * HEAD_DIM:2 * (hh + 1) * HEAD_DIM]
            st_ref[slot, hh, :, lo:] = _dot(kj, qts[hh][:, lo:])

    def softmax_pv(j, slot, carry, diag):
        start = pl.multiple_of(j * tk, tk)
        lo = 0 if diag is None else diag * tk
        out = []
        for hh in range(hp):
            m0, l0, acc0 = carry[hh]
            m, l, acc = m0[:, lo:], l0[:, lo:], acc0[:, lo:]
            st = st_ref[slot, hh, :, lo:]
            if diag is not None:
                key = lax.broadcasted_iota(jnp.int32, st.shape, 0)
                qry = lax.broadcasted_iota(jnp.int32, st.shape, 1)
                st = jnp.where(key <= qry, st, -jnp.inf)
            m_new = jnp.maximum(m, jnp.max(st, axis=0, keepdims=True))
            alpha = jnp.exp2(m - m_new)
            pt = jnp.exp2(st - m_new)
            l = alpha * l + jnp.sum(pt, axis=0, keepdims=True)
            vj = vt_ref[0, hh * HEAD_DIM:(hh + 1) * HEAD_DIM, pl.ds(start, tk)]
            acc = alpha * acc + _dot(vj, pt.astype(BF16))
            if lo:
                m_new, l, acc = (jnp.concatenate([old[:, :lo], new], axis=1)
                                 for old, new in ((m0, m_new), (l0, l), (acc0, acc)))
            out.append((m_new, l, acc))
        return tuple(out)

    def pair(jj, carry):
        a = 2 * jj
        scores(a + 1, 1)
        carry = softmax_pv(a, 0, carry, None)
        scores(a + 2, 0)
        return softmax_pv(a + 1, 1, carry, None)

    init = tuple((jnp.full((1, tq), -jnp.inf, F32), jnp.zeros((1, tq), F32),
                  jnp.zeros((HEAD_DIM, tq), F32)) for _ in range(hp))
    scores(0, 0)
    carry = lax.fori_loop(0, i, pair, init)
    scores(2 * i + 1, 1, lo=tk)
    carry = softmax_pv(2 * i, 0, carry, 0)
    carry = softmax_pv(2 * i + 1, 1, carry, 1)
    for hh in range(hp):
        _, l, acc = carry[hh]
        o_ref[0, :, hh * HEAD_DIM:(hh + 1) * HEAD_DIM] = (acc / l).T.astype(BF16)


def _attn(q, kp, vt, fc, tq=1024, hp=2):
    b, s, d = q.shape
    w = hp * HEAD_DIM
    tk = tq // 2
    return pl.pallas_call(
        functools.partial(_attn_kernel, tq=tq, tk=tk, hp=hp),
        grid=(b, N_HEADS // hp, s // tq),
        in_specs=[pl.BlockSpec((1, tq, w), lambda bb, h, i: (bb, i, h)),
                  pl.BlockSpec((1, s, 2 * w), lambda bb, h, i: (bb, 0, h)),
                  pl.BlockSpec((1, w, s), lambda bb, h, i: (bb, h, 0)),
                  pl.BlockSpec((1, tq, LANES), lambda bb, h, i: (bb, i, 0))],
        out_specs=pl.BlockSpec((1, tq, w), lambda bb, h, i: (bb, i, h)),
        out_shape=jax.ShapeDtypeStruct((b, s, d), BF16),
        scratch_shapes=[pltpu.VMEM((2, hp, tk, tq), F32)],
        compiler_params=_params(("parallel", "parallel", "parallel")),
        name="attn",
    )(q, kp, vt, fc)


def _oproj_kernel(x_ref, a_ref, ada_ref, w_ref, o_ref):
    gate = ada_ref[0, 5:6, :]
    o_ref[...] = x_ref[...] + gate * _dot(a_ref[...], w_ref[...])


def _oproj(x, attn, ada, w_o, j, seq, tm=1024):
    t, d = x.shape
    per_batch = seq // tm
    return pl.pallas_call(
        _oproj_kernel,
        grid=(t // tm,),
        in_specs=[pl.BlockSpec((tm, d), lambda i: (i, 0)),
                  pl.BlockSpec((tm, d), lambda i: (i, 0)),
                  pl.BlockSpec((1, 9, d), lambda i: (i // per_batch, 0, 0)),
                  pl.BlockSpec((None, d, d), lambda i: (j, 0, 0), pipeline_mode=pl.Buffered(1))],
        out_specs=pl.BlockSpec((tm, d), lambda i: (i, 0)),
        out_shape=jax.ShapeDtypeStruct((t, d), F32),
        compiler_params=_params(("parallel",)),
        name="oproj",
    )(x, attn, ada, w_o)


def kernel(x, c, norm_g, w_ada, b_ada, w_ffn_in, w_ffn_out, w_conv_in, conv_w, conv_b, w_conv_out,
           kv_norm_g, w_ada_kv, b_ada_kv, w_kvf, b_fgate, w_q, w_o, final_g):
    b, s, d = x.shape
    depth = norm_g.shape[0]
    n_a = w_conv_in.shape[0]
    t = b * s
    assert w_q.shape[0] == 1 and depth == n_a + 1

    w_in_b = w_ffn_in[0, 0].astype(BF16)
    w_out_b = w_ffn_out[0, 0].astype(BF16)
    w_f_b = jnp.pad(w_kvf[:, 2 * d:], ((0, 0), (0, LANES - N_HEADS))).astype(BF16)
    b_f = jnp.pad(b_fgate, (0, LANES - N_HEADS)).reshape(1, LANES)
    conv_b3 = conv_b.reshape(n_a, 1, d)
    attn_casts = [(w_q, (0,)), (w_o, (0,)), (w_kvf, ())]

    c_pad = jnp.pad(c, ((0, SUBLANES - b), (0, 0)))
    xf = x.reshape(t, d)
    for l in range(depth):
        ada = _ada(c_pad, w_ada, b_ada, l)[:b].reshape(b, 9, d)
        if l < n_a:
            mixer_casts = [(w_conv_in, (l,)), (w_conv_out, (l,))]
        else:
            mixer_casts = attn_casts if n_a == 0 else []
        xf, w_in_b, w_out_b, *mixer_w = _ffn(xf, ada, norm_g[l, 0], w_in_b, w_out_b, 0, s,
                                             cast_next=(w_ffn_in, w_ffn_out, l, 1), casts=mixer_casts)
        if l < n_a:
            w_conv_in_b, w_conv_out_b = mixer_w
            if l == n_a - 1:
                xf, *attn_w = _conv(xf, ada, norm_g[l, 1], w_conv_in_b, conv_w, conv_b3, w_conv_out_b, l, s,
                                    casts=attn_casts)
            else:
                xf = _conv(xf, ada, norm_g[l, 1], w_conv_in_b, conv_w, conv_b3, w_conv_out_b, l, s)
        else:
            if n_a == 0:
                attn_w = mixer_w
            w_q_b, w_o_b, w_kv_b = attn_w
            ada_kv = _ada(c_pad, w_ada_kv[None], b_ada_kv[None], 0)[:b].reshape(b, 2, d)
            q, kp, vt, fc = _qkv(xf, ada, ada_kv, norm_g[l, 1], kv_norm_g, w_q_b[None], w_kv_b, w_f_b, b_f, 0, s)
            attn = _attn(q.reshape(b, s, d), kp.reshape(b, s, 2 * d), vt, fc.reshape(b, s, LANES))
            xf = _oproj(xf, attn.reshape(t, d), ada, w_o_b[None], 0, s)
        if l == depth - 1:
            xf = _ffn(xf, ada, norm_g[l, 2], w_in_b, w_out_b, 2, s, final_g=final_g)
        else:
            xf, w_in_b, w_out_b = _ffn(xf, ada, norm_g[l, 2], w_in_b, w_out_b, 2, s,
                                       cast_next=(w_ffn_in, w_ffn_out, l + 1, 0))
    return xf.reshape(b, s, d)
```

```python
import functools
import math

import jax
import jax.numpy as jnp
from jax import lax
from jax.experimental import pallas as pl
from jax.experimental.pallas import tpu as pltpu

EPS = 1e-6
N_HEADS = 16
HEAD_DIM = 128
CONV_WIDTH = 3
LOG2E = math.log2(math.e)
LANES = 128
SUBLANES = 8
MXU_DIM = 256
VMEM_LIMIT_BYTES = 59 * 1024 * 1024

F32 = jnp.float32
BF16 = jnp.bfloat16


def _params(semantics):
    return pltpu.CompilerParams(dimension_semantics=semantics,
                                vmem_limit_bytes=VMEM_LIMIT_BYTES)


def _dot(a, b):
    return jnp.dot(a, b, preferred_element_type=F32)


def _rmsnorm(x, g):
    return x * lax.rsqrt(jnp.mean(x * x, axis=-1, keepdims=True) + EPS) * g


NORM_ROWS = 16
NORM_UNROLL = 8
def _modulated_norm_to(x_ref, *targets):
    params = [(h_ref, g_ref[...] * (1.0 + scale), shift) for h_ref, g_ref, shift, scale in targets]

    def body(r, _):
        rows = pl.ds(pl.multiple_of(r * NORM_ROWS, NORM_ROWS), NORM_ROWS)
        x = x_ref[rows, :]
        xhat = x * lax.rsqrt(jnp.mean(x * x, axis=-1, keepdims=True) + EPS)
        for h_ref, gain, shift in params:
            h_ref[rows, :] = (xhat * gain + shift).astype(BF16)
        return 0
    lax.fori_loop(0, x_ref.shape[0] // NORM_ROWS, body, 0, unroll=NORM_UNROLL)


def _residual_to(o_ref, x_ref, coef, final_g=None, rs_ref=None):
    if final_g is None:
        o_ref[...] = x_ref[...] + coef * o_ref[...]
        return
    n = x_ref.shape[0] // NORM_ROWS

    def rows_of(r):
        return pl.ds(pl.multiple_of(r * NORM_ROWS, NORM_ROWS), NORM_ROWS)

    def stats(r, _):
        y = x_ref[rows_of(r), :] + coef * o_ref[rows_of(r), :]
        rs = lax.rsqrt(jnp.mean(y * y, axis=-1, keepdims=True) + EPS)
        rs_ref[rows_of(r), :] = jnp.broadcast_to(rs, (NORM_ROWS, LANES))
        return 0
    lax.fori_loop(0, n, stats, 0, unroll=NORM_UNROLL)

    def scale(r, _):
        y = x_ref[rows_of(r), :] + coef * o_ref[rows_of(r), :]
        o_ref[rows_of(r), :] = y * rs_ref[rows_of(r), 0:1] * final_g
        return 0
    lax.fori_loop(0, n, scale, 0, unroll=NORM_UNROLL)


def _ada_rows(ada_ref, sub):
    return tuple(ada_ref[0, 3 * sub + k:3 * sub + k + 1, :] for k in range(3))


def _ada_kernel(c_ref, w_ref, b_ref, o_ref):
    c = c_ref[...]
    cond = c * jax.nn.sigmoid(c)
    o_ref[...] = _dot(cond.astype(BF16), w_ref[...].astype(BF16)) + b_ref[...]


def _ada(c_pad, w, b, l, tn=1024):
    nl, d, n = w.shape
    return pl.pallas_call(
        _ada_kernel,
        grid=(n // tn,),
        in_specs=[pl.BlockSpec((SUBLANES, d), lambda j: (0, 0)),
                  pl.BlockSpec((None, d, tn), lambda j: (l, 0, j)),
                  pl.BlockSpec((None, 1, tn), lambda j: (l, 0, j))],
        out_specs=pl.BlockSpec((SUBLANES, tn), lambda j: (0, j)),
        out_shape=jax.ShapeDtypeStruct((SUBLANES, n), F32),
        compiler_params=_params(("parallel",)),
        name="ada",
    )(c_pad, w, b.reshape(nl, 1, n))


def _row_cast_specs(w, lead, n_steps, step_of, row_tile=16):
    rows, cols = w.shape[-2:]
    cols -= cols % LANES
    n_tiles = rows // row_tile
    assert rows % row_tile == 0 and n_tiles <= n_steps

    def tile(*g):
        return jnp.minimum(step_of(*g), n_tiles - 1)
    in_spec = pl.BlockSpec((None,) * len(lead) + (row_tile, cols), lambda *g: tuple(lead) + (tile(*g), 0))
    out_spec = pl.BlockSpec((row_tile, cols), lambda *g: (tile(*g), 0))
    return in_spec, out_spec, jax.ShapeDtypeStruct((rows, cols), BF16)


def _cast_tiles(in_refs, out_refs):
    for src, dst in zip(in_refs, out_refs):
        dst[...] = src[...].astype(BF16)


def _ffn_kernel(x_ref, ada_ref, g_ref, wg_ref, wu_ref, wo_ref, *rest, sub, final, n_cast):
    rest = list(rest)
    fg_ref = rest.pop(0) if final else None
    cast_in = [rest.pop(0) for _ in range(n_cast)]
    o_ref = rest.pop(0)
    cast_out = [rest.pop(0) for _ in range(n_cast)]
    h_ref = rest.pop(0)
    rs_ref = rest.pop(0) if final else None
    f = pl.program_id(1)

    shift, scale, gate = _ada_rows(ada_ref, sub)

    @pl.when(f == 0)
    def _():
        _modulated_norm_to(x_ref, (h_ref, g_ref, shift, scale))
        o_ref[...] = jnp.zeros_like(o_ref)

    _cast_tiles(cast_in, cast_out)
    h = h_ref[...]
    a = _dot(h, wg_ref[...])
    b = _dot(h, wu_ref[...])
    act = (a * jax.nn.sigmoid(a) * b).astype(BF16)
    o_ref[...] += _dot(act, wo_ref[...])

    @pl.when(f == pl.num_programs(1) - 1)
    def _():
        if final:
            _residual_to(o_ref, x_ref, 0.5 * gate, fg_ref[...], rs_ref)
        else:
            _residual_to(o_ref, x_ref, 0.5 * gate)


def _ffn(x, ada, g, w_in, w_out, sub, seq, final_g=None, cast_next=None, casts=(), tf=512):
    t, d = x.shape
    ff = w_out.shape[0]
    nf = ff // tf
    final = final_g is not None
    tm = 1024
    per_batch = seq // tm
    ni = t // tm
    in_specs = [pl.BlockSpec((tm, d), lambda i, f: (i, 0)),
                pl.BlockSpec((1, 9, d), lambda i, f: (i // per_batch, 0, 0)),
                pl.BlockSpec((1, d), lambda i, f: (0, 0)),
                pl.BlockSpec((d, tf), lambda i, f: (0, f)),
                pl.BlockSpec((d, tf), lambda i, f: (0, nf + f)),
                pl.BlockSpec((tf, d), lambda i, f: (f, 0))]
    args = [x, ada, g.reshape(1, d), w_in, w_in, w_out]
    out_specs = [pl.BlockSpec((tm, d), lambda i, f: (i, 0))]
    out_shape = [jax.ShapeDtypeStruct((t, d), F32)]
    if final:
        in_specs.append(pl.BlockSpec((1, d), lambda i, f: (0, 0)))
        args.append(final_g.reshape(1, d))
    n_cast = 0
    if cast_next:
        w_in32, w_out32, l2, idx2 = cast_next
        in_tile = (d // ni, 2 * ff // nf)
        out_tile = (ff // (ni * nf), d)
        in_specs += [pl.BlockSpec((None, None) + in_tile, lambda i, f: (l2, idx2, i, f)),
                     pl.BlockSpec((None, None) + out_tile, lambda i, f: (l2, idx2, i * nf + f, 0))]
        args += [w_in32, w_out32]
        out_specs += [pl.BlockSpec(in_tile, lambda i, f: (i, f)),
                      pl.BlockSpec(out_tile, lambda i, f: (i * nf + f, 0))]
        out_shape += [jax.ShapeDtypeStruct((d, 2 * ff), BF16), jax.ShapeDtypeStruct((ff, d), BF16)]
        n_cast += 2
    for w, lead in casts:
        i_spec, o_spec, o_shape = _row_cast_specs(w, lead, ni * nf, lambda i, f: i * nf + f)
        in_specs.append(i_spec)
        args.append(w)
        out_specs.append(o_spec)
        out_shape.append(o_shape)
        n_cast += 1
    outs = pl.pallas_call(
        functools.partial(_ffn_kernel, sub=sub, final=final, n_cast=n_cast),
        grid=(ni, nf),
        in_specs=in_specs,
        out_specs=out_specs,
        out_shape=out_shape,
        scratch_shapes=[pltpu.VMEM((tm, d), BF16)] + ([pltpu.VMEM((tm, LANES), F32)] if final else []),
        compiler_params=_params(("parallel", "arbitrary")),
        name="ffn_final" if final else "ffn",
    )(*args)
    return outs if n_cast else outs[0]


def _conv_kernel(x_ref, ada_ref, g_ref, wb_ref, wc_ref, wx_ref, cw_ref, cb_ref, wo_ref,
                 *rest, per_batch, n_cast):
    cast_in, (o_ref, *cast_out), (h_ref, tail_ref) = (rest[:n_cast], rest[n_cast:2 * n_cast + 1],
                                                      rest[2 * n_cast + 1:])
    i = pl.program_id(0)
    c = pl.program_id(1)
    tm = x_ref.shape[0]

    shift, scale, gate = _ada_rows(ada_ref, 1)

    @pl.when(c == 0)
    def _():
        _modulated_norm_to(x_ref, (h_ref, g_ref, shift, scale))
        o_ref[...] = jnp.zeros_like(o_ref)

    @pl.when(i % per_batch == 0)
    def _():
        tail_ref[c] = jnp.zeros(tail_ref.shape[1:], F32)

    _cast_tiles(cast_in, cast_out)
    h = h_ref[...]
    bg = _dot(h, wb_ref[...])
    cg = _dot(h, wc_ref[...])
    xv = _dot(h, wx_ref[...])
    u = cg * xv
    tail = tail_ref[c]
    p1 = tail[SUBLANES - 1:SUBLANES, :]
    p2 = tail[SUBLANES - 2:SUBLANES - 1, :]
    row = lax.broadcasted_iota(jnp.int32, u.shape, 0)
    u1 = jnp.where(row == 0, p1, pltpu.roll(u, 1, 0))
    u2 = jnp.where(row == 0, p2, jnp.where(row == 1, p1, pltpu.roll(u, 2, 0)))
    conv = cw_ref[0:1, :] * u2 + cw_ref[1:2, :] * u1 + cw_ref[2:3, :] * u + cb_ref[...]
    tail_ref[c] = u[tm - SUBLANES:, :]
    o_ref[...] += _dot((bg * conv).astype(BF16), wo_ref[...])

    @pl.when(c == pl.num_programs(1) - 1)
    def _():
        _residual_to(o_ref, x_ref, gate)


def _conv(x, ada, g, w_in, conv_w, conv_b, w_out, l, seq, casts=(), tm=512, tc=512):
    t, d = x.shape
    nc = d // tc
    ni = t // tm
    per_batch = seq // tm
    in_specs = [pl.BlockSpec((tm, d), lambda i, c: (i, 0)),
                pl.BlockSpec((1, 9, d), lambda i, c: (i // per_batch, 0, 0)),
                pl.BlockSpec((1, d), lambda i, c: (0, 0)),
                pl.BlockSpec((d, tc), lambda i, c: (0, c)),
                pl.BlockSpec((d, tc), lambda i, c: (0, nc + c)),
                pl.BlockSpec((d, tc), lambda i, c: (0, 2 * nc + c)),
                pl.BlockSpec((None, CONV_WIDTH, tc), lambda i, c: (l, 0, c)),
                pl.BlockSpec((None, 1, tc), lambda i, c: (l, 0, c)),
                pl.BlockSpec((tc, d), lambda i, c: (c, 0))]
    args = [x, ada, g.reshape(1, d), w_in, w_in, w_in, conv_w, conv_b, w_out]
    out_specs = [pl.BlockSpec((tm, d), lambda i, c: (i, 0))]
    out_shape = [jax.ShapeDtypeStruct((t, d), F32)]
    for w, lead in casts:
        i_spec, o_spec, o_shape = _row_cast_specs(w, lead, ni * nc, lambda i, c: i * nc + c)
        in_specs.append(i_spec)
        args.append(w)
        out_specs.append(o_spec)
        out_shape.append(o_shape)
    outs = pl.pallas_call(
        functools.partial(_conv_kernel, per_batch=per_batch, n_cast=len(casts)),
        grid=(ni, nc),
        in_specs=in_specs,
        out_specs=out_specs,
        out_shape=out_shape,
        scratch_shapes=[pltpu.VMEM((tm, d), BF16),
                        pltpu.VMEM((nc, SUBLANES, tc), F32)],
        compiler_params=_params(("arbitrary", "arbitrary")),
        name="conv",
    )(*args)
    return outs if casts else outs[0]


def _qkv_kernel(x_ref, ada_ref, adakv_ref, g_ref, gkv_ref, wq_ref, wk_ref, wv_ref, wf_ref, bf_ref,
                q_ref, k_ref, v_ref, fc_ref, hq_ref, hkv_ref, carry_ref, *, per_batch, q_scale):
    i = pl.program_id(0)
    c = pl.program_id(1)
    tm = x_ref.shape[0]

    @pl.when(c == 0)
    def _():
        shift, scale, _ = _ada_rows(ada_ref, 1)
        _modulated_norm_to(x_ref, (hq_ref, g_ref, shift, scale),
                           (hkv_ref, gkv_ref, adakv_ref[0, 0:1, :], adakv_ref[0, 1:2, :]))
        hkv = hkv_ref[...]

        @pl.when(i % per_batch == 0)
        def _():
            carry_ref[...] = jnp.zeros_like(carry_ref)

        zf = _dot(hkv, wf_ref[...]) + bf_ref[...]
        ls = jnp.minimum(zf, 0.0) - jnp.log1p(jnp.exp(-jnp.abs(zf)))
        hi = ls.astype(BF16)
        r1 = ls - hi.astype(F32)
        mid = r1.astype(BF16)
        lo = (r1 - mid.astype(F32)).astype(BF16)
        rr = lax.broadcasted_iota(jnp.int32, (tm, tm), 0)
        cc = lax.broadcasted_iota(jnp.int32, (tm, tm), 1)
        tri = (rr >= cc).astype(BF16)
        parts = _dot(tri, jnp.concatenate([hi, mid, lo], axis=1))
        cum = (parts[:, :LANES] + parts[:, LANES:2 * LANES]) + parts[:, 2 * LANES:] + carry_ref[0:1, :]
        carry_ref[...] = jnp.broadcast_to(cum[tm - 1:tm, :], carry_ref.shape)
        fc_ref[...] = cum

    hkv = hkv_ref[...]
    q_ref[...] = (_dot(hq_ref[...], wq_ref[...]) * q_scale).astype(BF16)
    kc = _dot(hkv, wk_ref[...])
    cum = fc_ref[...]
    lane = lax.broadcasted_iota(jnp.int32, (tm, LANES), 1)
    for hh in range(kc.shape[1] // HEAD_DIM):
        head = c * (kc.shape[1] // HEAD_DIM) + hh
        col = jnp.sum(jnp.where(lane == head, cum, 0.0), axis=1, keepdims=True) * LOG2E
        hi, mid, lo = _split3(col)
        aug = jnp.where(lane < 3, 1.0,
                        jnp.where(lane == 3, -hi, jnp.where(lane == 4, -mid, jnp.where(lane == 5, -lo, 0.0))))
        k_ref[:, 2 * hh * HEAD_DIM:(2 * hh + 1) * HEAD_DIM] = kc[:, hh * HEAD_DIM:(hh + 1) * HEAD_DIM].astype(BF16)
        k_ref[:, (2 * hh + 1) * HEAD_DIM:(2 * hh + 2) * HEAD_DIM] = aug.astype(BF16)
    v_ref[0] = _dot(hkv, wv_ref[...]).T.astype(BF16)


def _qkv(x, ada, ada_kv, g, gkv, w_q, w_kv, w_f, b_f, j, seq, tm=1024, tn=512):
    t, d = x.shape
    nn = d // tn
    per_batch = seq // tm
    q_scale = LOG2E / math.sqrt(HEAD_DIM)
    row_blk = pl.BlockSpec((tm, tn), lambda i, c: (i, c))
    return pl.pallas_call(
        functools.partial(_qkv_kernel, per_batch=per_batch, q_scale=q_scale),
        grid=(t // tm, nn),
        in_specs=[pl.BlockSpec((tm, d), lambda i, c: (i, 0)),
                  pl.BlockSpec((1, 9, d), lambda i, c: (i // per_batch, 0, 0)),
                  pl.BlockSpec((1, 2, d), lambda i, c: (i // per_batch, 0, 0)),
                  pl.BlockSpec((1, d), lambda i, c: (0, 0)),
                  pl.BlockSpec((1, d), lambda i, c: (0, 0)),
                  pl.BlockSpec((None, d, tn), lambda i, c: (j, 0, c)),
                  pl.BlockSpec((d, tn), lambda i, c: (0, c)),
                  pl.BlockSpec((d, tn), lambda i, c: (0, nn + c)),
                  pl.BlockSpec((d, LANES), lambda i, c: (0, 0)),
                  pl.BlockSpec((1, LANES), lambda i, c: (0, 0))],
        out_specs=[row_blk,
                   pl.BlockSpec((tm, 2 * tn), lambda i, c: (i, c)),
                   pl.BlockSpec((1, tn, tm), lambda i, c: (i // per_batch, c, i % per_batch)),
                   pl.BlockSpec((tm, LANES), lambda i, c: (i, 0))],
        out_shape=[jax.ShapeDtypeStruct((t, d), BF16),
                   jax.ShapeDtypeStruct((t, 2 * d), BF16),
                   jax.ShapeDtypeStruct((t // seq, d, seq), BF16),
                   jax.ShapeDtypeStruct((t, LANES), F32)],
        scratch_shapes=[pltpu.VMEM((tm, d), BF16),
                        pltpu.VMEM((tm, d), BF16),
                        pltpu.VMEM((SUBLANES, LANES), F32)],
        compiler_params=_params(("arbitrary", "arbitrary")),
        name="qkv",
    )(x, ada, ada_kv, g.reshape(1, d), gkv.reshape(1, d), w_q, w_kv, w_kv, w_f, b_f)


def _split3(f):
    hi = f.astype(BF16)
    r1 = f - hi.astype(F32)
    mid = r1.astype(BF16)
    lo = (r1 - mid.astype(F32)).astype(BF16)
    return hi.astype(F32), mid.astype(F32), lo.astype(F32)


def _attn_kernel(q_ref, k_ref, vt_ref, fc_ref, o_ref, st_ref, *, tq, tk, hp):
    hg = pl.program_id(1)
    i = pl.program_id(2)

    fblk = fc_ref[0]
    lane = lax.broadcasted_iota(jnp.int32, (tq, LANES), 1)
    qts = []
    for hh in range(hp):
        fq = jnp.sum(jnp.where(lane == hg * hp + hh, fblk, 0.0), axis=1, keepdims=True) * LOG2E
        hi, mid, lo = _split3(fq)
        aq = jnp.where(lane == 0, hi,
                       jnp.where(lane == 1, mid,
                                 jnp.where(lane == 2, lo, jnp.where(lane < 6, 1.0, 0.0))))
        qh = q_ref[0, :, hh * HEAD_DIM:(hh + 1) * HEAD_DIM].astype(F32)
        qts.append(jnp.concatenate([qh.T, aq.T], axis=0).astype(BF16))

    def scores(j, slot, lo=0):
        start = pl.multiple_of(j * tk, tk)
        for hh in range(hp):
            kj = k_ref[0, pl.ds(start, tk), 2 * hh * HEAD_DIM:2 * (hh + 1) * HEAD_DIM]
            st_ref[slot, hh, :, lo:] = _dot(kj, qts[hh][:, lo:])

    def softmax_pv(j, slot, carry, diag):
        start = pl.multiple_of(j * tk, tk)
        lo = 0 if diag is None else diag * tk
        out = []
        for hh in range(hp):
            m0, l0, acc0 = carry[hh]
            m, l, acc = m0[:, lo:], l0[:, lo:], acc0[:, lo:]
            st = st_ref[slot, hh, :, lo:]
            if diag is not None:
                key = lax.broadcasted_iota(jnp.int32, st.shape, 0)
                qry = lax.broadcasted_iota(jnp.int32, st.shape, 1)
                st = jnp.where(key <= qry, st, -jnp.inf)
            m_new = jnp.maximum(m, jnp.max(st, axis=0, keepdims=True))
            alpha = jnp.exp2(m - m_new)
            pt = jnp.exp2(st - m_new)
            l = alpha * l + jnp.sum(pt, axis=0, keepdims=True)
            vj = vt_ref[0, hh * HEAD_DIM:(hh + 1) * HEAD_DIM, pl.ds(start, tk)]
            acc = alpha * acc + _dot(vj, pt.astype(BF16))
            if lo:
                m_new, l, acc = (jnp.concatenate([old[:, :lo], new], axis=1)
                                 for old, new in ((m0, m_new), (l0, l), (acc0, acc)))
            out.append((m_new, l, acc))
        return tuple(out)

    def pair(jj, carry):
        a = 2 * jj
        scores(a + 1, 1)
        carry = softmax_pv(a, 0, carry, None)
        scores(a + 2, 0)
        return softmax_pv(a + 1, 1, carry, None)

    init = tuple((jnp.full((1, tq), -jnp.inf, F32), jnp.zeros((1, tq), F32),
                  jnp.zeros((HEAD_DIM, tq), F32)) for _ in range(hp))
    scores(0, 0)
    carry = lax.fori_loop(0, i, pair, init)
    scores(2 * i + 1, 1, lo=tk)
    carry = softmax_pv(2 * i, 0, carry, 0)
    carry = softmax_pv(2 * i + 1, 1, carry, 1)
    for hh in range(hp):
        _, l, acc = carry[hh]
        o_ref[0, :, hh * HEAD_DIM:(hh + 1) * HEAD_DIM] = (acc / l).T.astype(BF16)


def _attn(q, kp, vt, fc, tq=1024, hp=2):
    b, s, d = q.shape
    w = hp * HEAD_DIM
    tk = tq // 2
    return pl.pallas_call(
        functools.partial(_attn_kernel, tq=tq, tk=tk, hp=hp),
        grid=(b, N_HEADS // hp, s // tq),
        in_specs=[pl.BlockSpec((1, tq, w), lambda bb, h, i: (bb, i, h)),
                  pl.BlockSpec((1, s, 2 * w), lambda bb, h, i: (bb, 0, h)),
                  pl.BlockSpec((1, w, s), lambda bb, h, i: (bb, h, 0)),
                  pl.BlockSpec((1, tq, LANES), lambda bb, h, i: (bb, i, 0))],
        out_specs=pl.BlockSpec((1, tq, w), lambda bb, h, i: (bb, i, h)),
        out_shape=jax.ShapeDtypeStruct((b, s, d), BF16),
        scratch_shapes=[pltpu.VMEM((2, hp, tk, tq), F32)],
        compiler_params=_params(("parallel", "parallel", "parallel")),
        name="attn",
    )(q, kp, vt, fc)


def _oproj_kernel(x_ref, a_ref, ada_ref, w_ref, o_ref):
    gate = ada_ref[0, 5:6, :]
    o_ref[...] = x_ref[...] + gate * _dot(a_ref[...], w_ref[...])


def _oproj(x, attn, ada, w_o, j, seq, tm=1024):
    t, d = x.shape
    per_batch = seq // tm
    return pl.pallas_call(
        _oproj_kernel,
        grid=(t // tm,),
        in_specs=[pl.BlockSpec((tm, d), lambda i: (i, 0)),
                  pl.BlockSpec((tm, d), lambda i: (i, 0)),
                  pl.BlockSpec((1, 9, d), lambda i: (i // per_batch, 0, 0)),
                  pl.BlockSpec((None, d, d), lambda i: (j, 0, 0), pipeline_mode=pl.Buffered(1))],
        out_specs=pl.BlockSpec((tm, d), lambda i: (i, 0)),
        out_shape=jax.ShapeDtypeStruct((t, d), F32),
        compiler_params=_params(("parallel",)),
        name="oproj",
    )(x, attn, ada, w_o)


def kernel(x, c, norm_g, w_ada, b_ada, w_ffn_in, w_ffn_out, w_conv_in, conv_w, conv_b, w_conv_out,
           kv_norm_g, w_ada_kv, b_ada_kv, w_kvf, b_fgate, w_q, w_o, final_g):
    b, s, d = x.shape
    depth = norm_g.shape[0]
    n_a = w_conv_in.shape[0]
    t = b * s
    assert w_q.shape[0] == 1 and depth == n_a + 1

    w_in_b = w_ffn_in[0, 0].astype(BF16)
    w_out_b = w_ffn_out[0, 0].astype(BF16)
    w_f_b = jnp.pad(w_kvf[:, 2 * d:], ((0, 0), (0, LANES - N_HEADS))).astype(BF16)
    b_f = jnp.pad(b_fgate, (0, LANES - N_HEADS)).reshape(1, LANES)
    conv_b3 = conv_b.reshape(n_a, 1, d)
    w_kv_b = w_kvf[:, :2 * d].astype(BF16)
    attn_casts = [(w_q, (0,)), (w_o, (0,))]

    c_pad = jnp.pad(c, ((0, SUBLANES - b), (0, 0)))
    xf = x.reshape(t, d)
    for l in range(depth):
        ada = _ada(c_pad, w_ada, b_ada, l)[:b].reshape(b, 9, d)
        if l < n_a:
            mixer_casts = [(w_conv_in, (l,)), (w_conv_out, (l,))]
        else:
            mixer_casts = attn_casts if n_a == 0 else []
        xf, w_in_b, w_out_b, *mixer_w = _ffn(xf, ada, norm_g[l, 0], w_in_b, w_out_b, 0, s,
                                             cast_next=(w_ffn_in, w_ffn_out, l, 1), casts=mixer_casts)
        if l < n_a:
            w_conv_in_b, w_conv_out_b = mixer_w
            if l == n_a - 1:
                xf, *attn_w = _conv(xf, ada, norm_g[l, 1], w_conv_in_b, conv_w, conv_b3, w_conv_out_b, l, s,
                                    casts=attn_casts)
            else:
                xf = _conv(xf, ada, norm_g[l, 1], w_conv_in_b, conv_w, conv_b3, w_conv_out_b, l, s)
        else:
            if n_a == 0:
                attn_w = mixer_w
            w_q_b, w_o_b = attn_w
            ada_kv = _ada(c_pad, w_ada_kv[None], b_ada_kv[None], 0)[:b].reshape(b, 2, d)
            q, kp, vt, fc = _qkv(xf, ada, ada_kv, norm_g[l, 1], kv_norm_g, w_q_b[None], w_kv_b, w_f_b, b_f, 0, s)
            attn = _attn(q.reshape(b, s, d), kp.reshape(b, s, 2 * d), vt, fc.reshape(b, s, LANES))
            xf = _oproj(xf, attn.reshape(t, d), ada, w_o_b[None], 0, s)
        if l == depth - 1:
            xf = _ffn(xf, ada, norm_g[l, 2], w_in_b, w_out_b, 2, s, final_g=final_g)
        else:
            xf, w_in_b, w_out_b = _ffn(xf, ada, norm_g[l, 2], w_in_b, w_out_b, 2, s,
                                       cast_next=(w_ffn_in, w_ffn_out, l + 1, 0))
    return xf.reshape(b, s, d)
```
